```python
import jax, jax.numpy as jnp
from jax import lax
import numpy as np

D_MODEL = 2048
BATCH = 2
SEQ = 4096
DEPTH = 2

N_A_LAYERS = DEPTH // 2
N_B_LAYERS = DEPTH - N_A_LAYERS
EPS = 1e-6
GLA_HEADS = 4
GLA_DK = D_MODEL // 2 // GLA_HEADS
GLA_DV = D_MODEL // GLA_HEADS
GLA_GATE_RANK = 16
GLA_GATE_TAU = 16.0
GLA_CHUNK = 64
GLA_QK_W = GLA_HEADS * GLA_DK
GLA_V_W = GLA_HEADS * GLA_DV
GLA_IN_W = 2 * GLA_QK_W + 2 * GLA_V_W + GLA_GATE_RANK
SB_HEADS = 16
SB_HEAD_DIM = D_MODEL // SB_HEADS
SB_W = SB_HEADS * SB_HEAD_DIM
SB_BLOCK = 128
D_FF = -(-8 * D_MODEL // (3 * 256)) * 256

kernel_name = "yoco_gla_stick_breaking_hybrid"


def rmsnorm(x, w):
    xf = x.astype(jnp.float32)
    xf = xf * lax.rsqrt(jnp.mean(xf * xf, axis=-1, keepdims=True) + EPS)
    return xf.astype(x.dtype) * w


def split_heads(t, n_heads):
    b, s, _ = t.shape
    return t.reshape(b, s, n_heads, -1).transpose(0, 2, 1, 3)


def swiglu(h, w_gate_up, w_down):
    gate, up = jnp.split(h @ w_gate_up, 2, axis=-1)
    return (jax.nn.silu(gate) * up) @ w_down


def gla_chunked(q, k, v, g):
    out_dtype = v.dtype
    q, k, v, g = (t.astype(jnp.float32) for t in (q, k, v, g))
    B, H, S, DK = q.shape
    DV = v.shape[-1]
    C = GLA_CHUNK
    NC = S // C

    def to_chunks(t):
        return t.reshape(B, H, NC, C, t.shape[-1]).transpose(2, 0, 1, 3, 4)

    causal = jnp.tril(jnp.ones((C, C), dtype=bool))[:, :, None]

    def step(state, inp):
        qi, ki, vi, gi = inp
        b = jnp.cumsum(gi, axis=-2)
        o_inter = jnp.einsum('bhck,bhkv->bhcv', qi * jnp.exp(b), state)
        rel = b[..., :, None, :] - b[..., None, :, :]
        decay = jnp.where(causal, jnp.exp(jnp.minimum(rel, 0.0)), 0.0)
        scores = jnp.einsum('bhik,bhjk,bhijk->bhij', qi, ki, decay)
        o_intra = jnp.einsum('bhij,bhjv->bhiv', scores, vi)
        b_last = b[..., -1:, :]
        k_dec = ki * jnp.exp(b_last - b)
        new_state = state * jnp.exp(b_last)[..., 0, :, None] + jnp.einsum('bhck,bhcv->bhkv', k_dec, vi)
        return new_state, o_inter + o_intra

    state0 = jnp.zeros((B, H, DK, DV), jnp.float32)
    _, o = lax.scan(step, state0, (to_chunks(q), to_chunks(k), to_chunks(v), to_chunks(g)))
    return o.transpose(1, 2, 0, 3, 4).reshape(B, H, S, DV).astype(out_dtype)


def gla_mixer(h, w_in, w_gate_up, b_gate, gnorm_w, w_out):
    B, S, _ = h.shape
    proj = h @ w_in
    q, k, v, r, gl = jnp.split(
        proj, [GLA_QK_W, 2 * GLA_QK_W, 2 * GLA_QK_W + GLA_V_W, 2 * GLA_QK_W + 2 * GLA_V_W], axis=-1)
    log_alpha = jax.nn.log_sigmoid((gl @ w_gate_up + b_gate).astype(jnp.float32)) / GLA_GATE_TAU
    o = gla_chunked(split_heads(q, GLA_HEADS) * (GLA_DK ** -0.5), split_heads(k, GLA_HEADS),
                    split_heads(v, GLA_HEADS), split_heads(log_alpha, GLA_HEADS))
    o = rmsnorm(o.transpose(0, 2, 1, 3), gnorm_w)
    o = o * jax.nn.silu(r).reshape(B, S, GLA_HEADS, GLA_DV)
    return o.reshape(B, S, GLA_V_W) @ w_out


def stick_breaking_attention(q, k, v):
    B, H, S, hd = q.shape
    NB = S // SB_BLOCK
    qb = q.reshape(B, H, NB, SB_BLOCK, hd).transpose(2, 0, 1, 3, 4)
    kpos = jnp.arange(S)

    def block(args):
        qi, i = args
        z = jnp.einsum('bhqd,bhkd->bhqk', qi, k).astype(jnp.float32) * (hd ** -0.5)
        qpos = i * SB_BLOCK + jnp.arange(SB_BLOCK)
        mask = kpos[None, :] < qpos[:, None]
        log_fail = jnp.where(mask, jax.nn.log_sigmoid(-z), 0.0)
        after = lax.cumsum(log_fail, axis=log_fail.ndim - 1, reverse=True) - log_fail
        a = jnp.where(mask, jnp.exp(jax.nn.log_sigmoid(z) + after), 0.0)
        return jnp.einsum('bhqk,bhkd->bhqd', a.astype(v.dtype), v)

    o = lax.map(block, (qb, jnp.arange(NB)))
    return o.transpose(1, 2, 0, 3, 4).reshape(B, H, S, hd)


def setup_inputs(seed: int = 0) -> dict:
    key = jax.random.key(seed)
    ks = jax.random.split(key, 20)
    f32 = jnp.float32

    def nrm(k, shape, fan_in):
        return jax.random.normal(k, shape, f32) * (fan_in ** -0.5)

    def gain(k, shape):
        return 1.0 + 0.01 * jax.random.normal(k, shape, f32)

    return {
        "x": jax.random.normal(ks[0], (BATCH, SEQ, D_MODEL), f32),
        "attn_norm_w": gain(ks[1], (DEPTH, D_MODEL)),
        "ffn_norm_w": gain(ks[2], (DEPTH, D_MODEL)),
        "gla_w_in": nrm(ks[3], (N_A_LAYERS, D_MODEL, GLA_IN_W), D_MODEL),
        "gla_w_gate_up": nrm(ks[4], (N_A_LAYERS, GLA_GATE_RANK, GLA_QK_W), GLA_GATE_RANK),
        "gla_b_gate": 0.1 * jax.random.normal(ks[5], (N_A_LAYERS, GLA_QK_W), f32),
        "gla_gnorm_w": gain(ks[6], (N_A_LAYERS, GLA_DV)),
        "gla_w_out": nrm(ks[7], (N_A_LAYERS, GLA_V_W, D_MODEL), GLA_V_W),
        "kv_norm_w": gain(ks[8], (D_MODEL,)),
        "sb_w_kv": nrm(ks[9], (D_MODEL, 2 * SB_W), D_MODEL),
        "sb_w_q": nrm(ks[10], (N_B_LAYERS, D_MODEL, SB_W), D_MODEL),
        "sb_w_out": nrm(ks[11], (N_B_LAYERS, SB_W, D_MODEL), SB_W),
        "ffn_w_gate_up": nrm(ks[12], (DEPTH, D_MODEL, 2 * D_FF), D_MODEL),
        "ffn_w_down": nrm(ks[13], (DEPTH, D_FF, D_MODEL), D_FF),
        "final_norm_w": gain(ks[14], (D_MODEL,)),
    }


def reference(x, attn_norm_w, ffn_norm_w, gla_w_in, gla_w_gate_up, gla_b_gate, gla_gnorm_w, gla_w_out,
              kv_norm_w, sb_w_kv, sb_w_q, sb_w_out, ffn_w_gate_up, ffn_w_down, final_norm_w):
    h = x
    k_shared = None
    v_shared = None
    for layer in range(DEPTH):
        a = rmsnorm(h, attn_norm_w[layer])
        if layer < N_A_LAYERS:
            i = layer
            h = h + gla_mixer(a, gla_w_in[i], gla_w_gate_up[i], gla_b_gate[i], gla_gnorm_w[i], gla_w_out[i])
        else:
            j = layer - N_A_LAYERS
            q = split_heads(a @ sb_w_q[j], SB_HEADS)
            o = stick_breaking_attention(q, k_shared, v_shared)
            o = o.transpose(0, 2, 1, 3).reshape(h.shape[0], h.shape[1], SB_W)
            h = h + o @ sb_w_out[j]
        h = h + swiglu(rmsnorm(h, ffn_norm_w[layer]), ffn_w_gate_up[layer], ffn_w_down[layer])
        if layer == N_A_LAYERS - 1:
            kv = rmsnorm(h, kv_norm_w) @ sb_w_kv
            k_c, v_c = jnp.split(kv, 2, axis=-1)
            k_shared = split_heads(k_c, SB_HEADS)
            v_shared = split_heads(v_c, SB_HEADS)
    return rmsnorm(h, final_norm_w)
```

```python
import functools

import jax
import jax.numpy as jnp
from jax import lax
from jax.experimental import pallas as pl
from jax.experimental.pallas import tpu as pltpu

EPS = 1e-6
GLA_HEADS = 4
GLA_GATE_TAU = 16.0
SB_HEADS = 16
BF = jnp.bfloat16
F32 = jnp.float32
LANES = 128
MIB = 2 ** 20

_NT = (((1,), (1,)), ((), ()))
_TN = (((0,), (0,)), ((), ()))


def _params(sem, vmem_mib):
    return pltpu.CompilerParams(dimension_semantics=sem, vmem_limit_bytes=vmem_mib * MIB)


def _dot(a, b):
    return jnp.dot(a, b, preferred_element_type=F32)


def _rms(x, w):
    return x * lax.rsqrt(jnp.mean(x * x, axis=-1, keepdims=True) + EPS) * w


def _split2(x):
    hi = x.astype(BF)
    lo = (x - hi.astype(F32)).astype(BF)
    return hi, lo


def _log_sigmoid(x):
    return jnp.minimum(x, 0.0) - jnp.log(1.0 + jnp.exp(-jnp.abs(x)))


def _silu(x):
    return x * (1.0 / (1.0 + jnp.exp(-x)))


def _norm_matmul_kernel(x_ref, nw_ref, w_ref, o_ref, xn_ref, *, out_scale):
    @pl.when(pl.program_id(1) == 0)
    def _():
        xn_ref[...] = _rms(x_ref[...], nw_ref[...]).astype(BF)

    acc = _dot(xn_ref[...], w_ref[...])
    if out_scale != 1.0:
        acc = acc * out_scale
    o_ref[...] = acc.astype(o_ref.dtype)


def _norm_matmul(x, nw, w, *, tm, tn, out_scale=1.0):
    t, d = x.shape
    n = w.shape[1]
    return pl.pallas_call(
        functools.partial(_norm_matmul_kernel, out_scale=out_scale),
        grid=(t // tm, n // tn),
        in_specs=[
            pl.BlockSpec((tm, d), lambda i, j: (i, 0)),
            pl.BlockSpec((1, d), lambda i, j: (0, 0)),
            pl.BlockSpec((d, tn), lambda i, j: (0, j)),
        ],
        out_specs=pl.BlockSpec((tm, tn), lambda i, j: (i, j)),
        out_shape=jax.ShapeDtypeStruct((t, n), BF),
        scratch_shapes=[pltpu.VMEM((tm, d), BF)],
        compiler_params=_params(("parallel", "arbitrary"), 48),
        name="norm_matmul",
    )(x, nw, w)


def _gla_inproj_kernel(x_ref, nw_ref, w_ref, wgl_ref, wup_ref, bg_ref, o_ref, la_ref, xn_ref):
    @pl.when(pl.program_id(1) == 0)
    def _():
        xn = _rms(x_ref[...], nw_ref[...]).astype(BF)
        xn_ref[...] = xn
        gl_hi, gl_lo = _split2(_dot(xn, wgl_ref[...]))
        up_hi, up_lo = _split2(wup_ref[...])
        g = _dot(gl_hi, up_hi) + _dot(gl_lo, up_hi) + _dot(gl_hi, up_lo) + bg_ref[...]
        la_ref[...] = _log_sigmoid(g) * (1.0 / GLA_GATE_TAU)

    o_ref[...] = _dot(xn_ref[...], w_ref[...]).astype(o_ref.dtype)


def _gla_inproj(x, nw, w, wgl, wup, bg, *, tm, tn):
    t, d = x.shape
    n = w.shape[1]
    qk_w = wup.shape[1]
    return pl.pallas_call(
        _gla_inproj_kernel,
        grid=(t // tm, n // tn),
        in_specs=[
            pl.BlockSpec((tm, d), lambda i, j: (i, 0)),
            pl.BlockSpec((1, d), lambda i, j: (0, 0)),
            pl.BlockSpec((d, tn), lambda i, j: (0, j)),
            pl.BlockSpec(wgl.shape, lambda i, j: (0, 0)),
            pl.BlockSpec(wup.shape, lambda i, j: (0, 0)),
            pl.BlockSpec((1, qk_w), lambda i, j: (0, 0)),
        ],
        out_specs=[
            pl.BlockSpec((tm, tn), lambda i, j: (i, j)),
            pl.BlockSpec((tm, qk_w), lambda i, j: (i, 0)),
        ],
        out_shape=[
            jax.ShapeDtypeStruct((t, n), BF),
            jax.ShapeDtypeStruct((t, qk_w), F32),
        ],
        scratch_shapes=[pltpu.VMEM((tm, d), BF)],
        compiler_params=_params(("parallel", "arbitrary"), 48),
        name="gla_inproj",
    )(x, nw, w, wgl, wup, bg)


def _gla_kernel(q_ref, k_ref, v_ref, r_ref, g_ref, gw_ref, o_ref, st_ref, qh_ref, kh_ref,
                *, chunk, sub, scale):
    n_sub = chunk // sub
    dk = q_ref.shape[-1]

    @pl.when(pl.program_id(2) == 0)
    def _():
        st_ref[...] = jnp.zeros_like(st_ref)
        qh_ref[...] = jnp.zeros_like(qh_ref)
        kh_ref[...] = jnp.zeros_like(kh_ref)

    q = q_ref[0].astype(F32) * scale
    k = k_ref[0].astype(F32)
    v = v_ref[0]
    g = g_ref[0]

    row = lax.broadcasted_iota(jnp.int32, (chunk, chunk), 0)
    col = lax.broadcasted_iota(jnp.int32, (chunk, chunk), 1)
    tri = jnp.where(col <= row, 1.0, 0.0).astype(BF)
    g_hi, g_lo = _split2(g)
    b = _dot(tri, g_hi) + _dot(tri, g_lo)
    b_last = b[chunk - 1:chunk, :]

    st = st_ref[...]
    o = lax.dot_general((q * jnp.exp(b)).astype(BF), st.astype(BF), _NT,
                        preferred_element_type=F32)

    for i in range(1, n_sub):
        beta = b[i * sub - 1:i * sub, :]
        rows = slice(i * sub, (i + 1) * sub)
        cols = slice((i - 1) * dk, i * dk)
        qh_ref[rows, cols] = (q[rows] * jnp.exp(b[rows] - beta)).astype(BF)
        kh_ref[0:i * sub, cols] = (k[0:i * sub] * jnp.exp(beta - b[0:i * sub])).astype(BF)
    scores = lax.dot_general(qh_ref[...], kh_ref[...], _NT, preferred_element_type=F32)

    diff = jnp.where(row // sub == col // sub, row - col, -1)
    for d in range(sub):
        if d == 0:
            p = q * k
        else:
            rel = b - pltpu.roll(b, d, 0)
            p = q * pltpu.roll(k, d, 0) * jnp.exp(jnp.minimum(rel, 0.0))
        scores = jnp.where(diff == d, jnp.sum(p, axis=-1, keepdims=True), scores)

    o = o + _dot(scores.astype(BF), v)

    k_dec = (k * jnp.exp(b_last - b)).astype(BF)
    st_ref[...] = st * jnp.exp(b_last) + lax.dot_general(v, k_dec, _TN, preferred_element_type=F32)

    r = r_ref[0].astype(F32)
    o_ref[0] = (_rms(o, gw_ref[...]) * _silu(r)).astype(o_ref.dtype)


def _gla_core(proj, la, gw, *, chunk, sub):
    bsz, s, _ = proj.shape
    dk = la.shape[-1] // GLA_HEADS
    dv = gw.shape[-1]
    kq_blocks = GLA_HEADS
    v_blocks = 2 * GLA_HEADS * dk // dv
    r_blocks = v_blocks + GLA_HEADS
    return pl.pallas_call(
        functools.partial(_gla_kernel, chunk=chunk, sub=sub, scale=dk ** -0.5),
        grid=(bsz, GLA_HEADS, s // chunk),
        in_specs=[
            pl.BlockSpec((1, chunk, dk), lambda b, h, n: (b, n, h)),
            pl.BlockSpec((1, chunk, dk), lambda b, h, n: (b, n, kq_blocks + h)),
            pl.BlockSpec((1, chunk, dv), lambda b, h, n: (b, n, v_blocks + h)),
            pl.BlockSpec((1, chunk, dv), lambda b, h, n: (b, n, r_blocks + h)),
            pl.BlockSpec((1, chunk, dk), lambda b, h, n: (b, n, h)),
            pl.BlockSpec((1, dv), lambda b, h, n: (0, 0)),
        ],
        out_specs=pl.BlockSpec((1, chunk, dv), lambda b, h, n: (b, n, h)),
        out_shape=jax.ShapeDtypeStruct((bsz, s, GLA_HEADS * dv), BF),
        scratch_shapes=[
            pltpu.VMEM((dv, dk), F32),
            pltpu.VMEM((chunk, (chunk // sub - 1) * dk), BF),
            pltpu.VMEM((chunk, (chunk // sub - 1) * dk), BF),
        ],
        compiler_params=_params(("parallel", "parallel", "arbitrary"), 32),
        name="gla_core",
    )(proj, proj, proj, proj, la, gw)


def _proj_res_kernel(x_ref, w_ref, res_ref, o_ref):
    o_ref[...] = res_ref[...] + _dot(x_ref[...], w_ref[...])


def _proj_res(x, w, res, *, tm):
    t, kdim = x.shape
    n = w.shape[1]
    return pl.pallas_call(
        _proj_res_kernel,
        grid=(t // tm,),
        in_specs=[
            pl.BlockSpec((tm, kdim), lambda i: (i, 0)),
            pl.BlockSpec((kdim, n), lambda i: (0, 0)),
            pl.BlockSpec((tm, n), lambda i: (i, 0)),
        ],
        out_specs=pl.BlockSpec((tm, n), lambda i: (i, 0)),
        out_shape=jax.ShapeDtypeStruct((t, n), F32),
        compiler_params=_params(("parallel",), 48),
        name="proj_res",
    )(x, w, res)


def _ffn_kernel(h_ref, nw_ref, wg_ref, wu_ref, wd_ref, fw_ref, o_ref, xn_ref, acc_ref, *, final_norm):
    f = pl.program_id(1)

    @pl.when(f == 0)
    def _():
        xn_ref[...] = _rms(h_ref[...], nw_ref[...]).astype(BF)
        acc_ref[...] = jnp.zeros_like(acc_ref)

    xn = xn_ref[...]
    act = (_silu(_dot(xn, wg_ref[...])) * _dot(xn, wu_ref[...])).astype(BF)
    acc_ref[...] += _dot(act, wd_ref[...])

    @pl.when(f == pl.num_programs(1) - 1)
    def _():
        y = h_ref[...] + acc_ref[...]
        if final_norm:
            y = _rms(y, fw_ref[...])
        o_ref[...] = y


def _ffn(h, nw, w_gate_up, w_down, fw, *, tm, tf, final_norm):
    t, d = h.shape
    d_ff = w_down.shape[0]
    nf = d_ff // tf
    return pl.pallas_call(
        functools.partial(_ffn_kernel, final_norm=final_norm),
        grid=(t // tm, nf),
        in_specs=[
            pl.BlockSpec((tm, d), lambda i, f: (i, 0)),
            pl.BlockSpec((1, d), lambda i, f: (0, 0)),
            pl.BlockSpec((d, tf), lambda i, f: (0, f)),
            pl.BlockSpec((d, tf), lambda i, f: (0, nf + f)),
            pl.BlockSpec((tf, d), lambda i, f: (f, 0)),
            pl.BlockSpec((1, d), lambda i, f: (0, 0)),
        ],
        out_specs=pl.BlockSpec((tm, d), lambda i, f: (i, 0)),
        out_shape=jax.ShapeDtypeStruct((t, d), F32),
        scratch_shapes=[pltpu.VMEM((tm, d), BF), pltpu.VMEM((tm, d), F32)],
        compiler_params=_params(("parallel", "arbitrary"), 48),
        name="ffn",
    )(h, nw, w_gate_up, w_gate_up, w_down, fw)


def _sb_kernel(q_ref, k_ref, v_ref, o_ref, *, tq, tk):
    i = pl.program_id(2)
    q = q_ref[0]
    hd = q.shape[-1]
    kr = lax.broadcasted_iota(jnp.int32, (tk, tk), 0)
    kc = lax.broadcasted_iota(jnp.int32, (tk, tk), 1)
    later = jnp.where(kr > kc, 1.0, 0.0).astype(BF)

    def block(start, carry, acc, mask):
        kb = k_ref[0, pl.ds(start, tk), :]
        vb = v_ref[0, pl.ds(start, tk), :]
        z = lax.dot_general(q, kb, _NT, preferred_element_type=F32)
        lf = _log_sigmoid(-z)
        if mask is not None:
            lf = jnp.where(mask, lf, 0.0)
        lf_hi, lf_lo = _split2(lf)
        after = _dot(lf_hi, later) + _dot(lf_lo, later) + carry
        a = jnp.exp(z + lf + after)
        if mask is not None:
            a = jnp.where(mask, a, 0.0)
        acc = acc + _dot(a.astype(BF), vb)
        carry = carry + jnp.sum(lf, axis=-1, keepdims=True)
        return carry, acc

    qpos = lax.broadcasted_iota(jnp.int32, (tq, tk), 0)
    kpos = lax.broadcasted_iota(jnp.int32, (tq, tk), 1)
    carry, acc = block(pl.multiple_of(i * tq, tq), jnp.zeros((tq, 1), F32),
                       jnp.zeros((tq, hd), F32), kpos < qpos)

    def body(t, c):
        start = pl.multiple_of((i - 1 - t) * tk, tk)
        return block(start, c[0], c[1], None)

    carry, acc = lax.fori_loop(0, i, body, (carry, acc))
    o_ref[0] = acc.astype(o_ref.dtype)


def _sb_attention(qp, kv, *, tq):
    bsz, s, w = qp.shape
    hd = w // SB_HEADS
    return pl.pallas_call(
        functools.partial(_sb_kernel, tq=tq, tk=tq),
        grid=(bsz, SB_HEADS, s // tq),
        in_specs=[
            pl.BlockSpec((1, tq, hd), lambda b, h, i: (b, i, h)),
            pl.BlockSpec((1, s, hd), lambda b, h, i: (b, 0, h)),
            pl.BlockSpec((1, s, hd), lambda b, h, i: (b, 0, SB_HEADS + h)),
        ],
        out_specs=pl.BlockSpec((1, tq, hd), lambda b, h, i: (b, i, h)),
        out_shape=jax.ShapeDtypeStruct((bsz, s, w), BF),
        compiler_params=_params(("parallel", "parallel", "arbitrary"), 32),
        name="sb_attn",
    )(qp, kv, kv)


def kernel(x, attn_norm_w, ffn_norm_w, gla_w_in, gla_w_gate_up, gla_b_gate, gla_gnorm_w, gla_w_out,
           kv_norm_w, sb_w_kv, sb_w_q, sb_w_out, ffn_w_gate_up, ffn_w_down, final_norm_w):
    bsz, s, d = x.shape
    t = bsz * s
    depth = attn_norm_w.shape[0]
    n_gla = gla_w_in.shape[0]
    rank = gla_w_gate_up.shape[1]
    main_w = gla_w_in.shape[2] - rank
    hd = d // SB_HEADS
    row = lambda v: v.reshape(1, -1)

    h = x.reshape(t, d)
    kv = None
    for layer in range(depth):
        if layer < n_gla:
            w_in = gla_w_in[layer]
            wgl = jnp.pad(w_in[:, main_w:], ((0, 0), (0, LANES - rank))).astype(BF)
            wup = jnp.pad(gla_w_gate_up[layer], ((0, LANES - rank), (0, 0)))
            proj, la = _gla_inproj(h, row(attn_norm_w[layer]), w_in[:, :main_w].astype(BF), wgl, wup,
                                   row(gla_b_gate[layer]), tm=512, tn=1024)
            o = _gla_core(proj.reshape(bsz, s, main_w), la.reshape(bsz, s, -1),
                          row(gla_gnorm_w[layer]), chunk=128, sub=16)
            h = _proj_res(o.reshape(t, -1), gla_w_out[layer].astype(BF), h, tm=512)
        else:
            j = layer - n_gla
            qp = _norm_matmul(h, row(attn_norm_w[layer]), sb_w_q[j].astype(BF), tm=1024, tn=1024,
                              out_scale=hd ** -0.5)
            o = _sb_attention(qp.reshape(bsz, s, -1), kv, tq=256)
            h = _proj_res(o.reshape(t, -1), sb_w_out[j].astype(BF), h, tm=512)
        last = layer == depth - 1
        h = _ffn(h, row(ffn_norm_w[layer]), ffn_w_gate_up[layer].astype(BF), ffn_w_down[layer].astype(BF),
                 row(final_norm_w), tm=512, tf=512, final_norm=last)
        if layer == n_gla - 1:
            kv = _norm_matmul(h, row(kv_norm_w), sb_w_kv.astype(BF), tm=1024, tn=1024).reshape(bsz, s, -1)
    return h.reshape(bsz, s, d)
```

```python
import functools

import jax
import jax.numpy as jnp
from jax import lax
from jax.experimental import pallas as pl
from jax.experimental.pallas import tpu as pltpu

EPS = 1e-6
GLA_HEADS = 4
GLA_GATE_TAU = 16.0
SB_HEADS = 16
BF = jnp.bfloat16
F32 = jnp.float32
LANES = 128
MIB = 2 ** 20

_NT = (((1,), (1,)), ((), ()))
_TN = (((0,), (0,)), ((), ()))


def _params(sem, vmem_mib):
    return pltpu.CompilerParams(dimension_semantics=sem, vmem_limit_bytes=vmem_mib * MIB)


def _dot(a, b):
    return jnp.dot(a, b, preferred_element_type=F32)


def _rms(x, w):
    return x * lax.rsqrt(jnp.mean(x * x, axis=-1, keepdims=True) + EPS) * w


def _split2(x):
    hi = x.astype(BF)
    lo = (x - hi.astype(F32)).astype(BF)
    return hi, lo


def _log_sigmoid(x):
    return jnp.minimum(x, 0.0) - jnp.log(1.0 + jnp.exp(-jnp.abs(x)))


def _silu(x):
    return x * (1.0 / (1.0 + jnp.exp(-x)))


def _norm_matmul_kernel(x_ref, nw_ref, w_ref, o_ref, xn_ref, *, out_scale):
    @pl.when(pl.program_id(1) == 0)
    def _():
        xn_ref[...] = _rms(x_ref[...], nw_ref[...]).astype(BF)

    acc = _dot(xn_ref[...], w_ref[...])
    if out_scale != 1.0:
        acc = acc * out_scale
    o_ref[...] = acc.astype(o_ref.dtype)


def _norm_matmul(x, nw, w, *, tm, tn, out_scale=1.0):
    t, d = x.shape
    n = w.shape[1]
    return pl.pallas_call(
        functools.partial(_norm_matmul_kernel, out_scale=out_scale),
        grid=(t // tm, n // tn),
        in_specs=[
            pl.BlockSpec((tm, d), lambda i, j: (i, 0)),
            pl.BlockSpec((1, d), lambda i, j: (0, 0)),
            pl.BlockSpec((d, tn), lambda i, j: (0, j)),
        ],
        out_specs=pl.BlockSpec((tm, tn), lambda i, j: (i, j)),
        out_shape=jax.ShapeDtypeStruct((t, n), BF),
        scratch_shapes=[pltpu.VMEM((tm, d), BF)],
        compiler_params=_params(("parallel", "arbitrary"), 48),
        name="norm_matmul",
    )(x, nw, w)


def _gla_inproj_kernel(x_ref, nw_ref, w_ref, wgl_ref, wup_ref, bg_ref, o_ref, la_ref, xn_ref):
    @pl.when(pl.program_id(1) == 0)
    def _():
        xn = _rms(x_ref[...], nw_ref[...]).astype(BF)
        xn_ref[...] = xn
        gl_hi, gl_lo = _split2(_dot(xn, wgl_ref[...]))
        up_hi, up_lo = _split2(wup_ref[...])
        g = _dot(gl_hi, up_hi) + _dot(gl_lo, up_hi) + _dot(gl_hi, up_lo) + bg_ref[...]
        la_ref[...] = _log_sigmoid(g) * (1.0 / GLA_GATE_TAU)

    o_ref[...] = _dot(xn_ref[...], w_ref[...]).astype(o_ref.dtype)


def _gla_inproj(x, nw, w, wgl, wup, bg, *, tm, tn):
    t, d = x.shape
    n = w.shape[1]
    qk_w = wup.shape[1]
    return pl.pallas_call(
        _gla_inproj_kernel,
        grid=(t // tm, n // tn),
        in_specs=[
            pl.BlockSpec((tm, d), lambda i, j: (i, 0)),
            pl.BlockSpec((1, d), lambda i, j: (0, 0)),
            pl.BlockSpec((d, tn), lambda i, j: (0, j)),
            pl.BlockSpec(wgl.shape, lambda i, j: (0, 0)),
            pl.BlockSpec(wup.shape, lambda i, j: (0, 0)),
            pl.BlockSpec((1, qk_w), lambda i, j: (0, 0)),
        ],
        out_specs=[
            pl.BlockSpec((tm, tn), lambda i, j: (i, j)),
            pl.BlockSpec((tm, qk_w), lambda i, j: (i, 0)),
        ],
        out_shape=[
            jax.ShapeDtypeStruct((t, n), BF),
            jax.ShapeDtypeStruct((t, qk_w), F32),
        ],
        scratch_shapes=[pltpu.VMEM((tm, d), BF)],
        compiler_params=_params(("parallel", "arbitrary"), 48),
        name="gla_inproj",
    )(x, nw, w, wgl, wup, bg)


def _gla_kernel(q_ref, k_ref, v_ref, r_ref, g_ref, gw_ref, o_ref, st_ref, qh_ref, kh_ref,
                *, chunk, sub, scale):
    n_sub = chunk // sub
    dk = q_ref.shape[-1]

    @pl.when(pl.program_id(2) == 0)
    def _():
        st_ref[...] = jnp.zeros_like(st_ref)
        qh_ref[...] = jnp.zeros_like(qh_ref)
        kh_ref[...] = jnp.zeros_like(kh_ref)

    q = q_ref[0].astype(F32) * scale
    k = k_ref[0].astype(F32)
    v = v_ref[0]
    g = g_ref[0]

    row = lax.broadcasted_iota(jnp.int32, (chunk, chunk), 0)
    col = lax.broadcasted_iota(jnp.int32, (chunk, chunk), 1)
    tri = jnp.where(col <= row, 1.0, 0.0).astype(BF)
    g_hi, g_lo = _split2(g)
    b = _dot(tri, g_hi) + _dot(tri, g_lo)
    b_last = b[chunk - 1:chunk, :]

    st = st_ref[...]
    o = lax.dot_general((q * jnp.exp(b)).astype(BF), st.astype(BF), _NT,
                        preferred_element_type=F32)

    for i in range(1, n_sub):
        beta = b[i * sub - 1:i * sub, :]
        rows = slice(i * sub, (i + 1) * sub)
        cols = slice((i - 1) * dk, i * dk)
        qh_ref[rows, cols] = (q[rows] * jnp.exp(b[rows] - beta)).astype(BF)
        kh_ref[0:i * sub, cols] = (k[0:i * sub] * jnp.exp(beta - b[0:i * sub])).astype(BF)
    scores = lax.dot_general(qh_ref[...], kh_ref[...], _NT, preferred_element_type=F32)

    diff = jnp.where(row // sub == col // sub, row - col, -1)
    for d in range(sub):
        if d == 0:
            p = q * k
        else:
            rel = b - pltpu.roll(b, d, 0)
            p = q * pltpu.roll(k, d, 0) * jnp.exp(jnp.minimum(rel, 0.0))
        scores = jnp.where(diff == d, jnp.sum(p, axis=-1, keepdims=True), scores)

    o = o + _dot(scores.astype(BF), v)

    k_dec = (k * jnp.exp(b_last - b)).astype(BF)
    st_ref[...] = st * jnp.exp(b_last) + lax.dot_general(v, k_dec, _TN, preferred_element_type=F32)

    r = r_ref[0].astype(F32)
    o_ref[0] = (_rms(o, gw_ref[...]) * _silu(r)).astype(o_ref.dtype)


def _gla_core(proj, la, gw, *, chunk, sub):
    bsz, s, _ = proj.shape
    dk = la.shape[-1] // GLA_HEADS
    dv = gw.shape[-1]
    kq_blocks = GLA_HEADS
    v_blocks = 2 * GLA_HEADS * dk // dv
    r_blocks = v_blocks + GLA_HEADS
    return pl.pallas_call(
        functools.partial(_gla_kernel, chunk=chunk, sub=sub, scale=dk ** -0.5),
        grid=(bsz, GLA_HEADS, s // chunk),
        in_specs=[
            pl.BlockSpec((1, chunk, dk), lambda b, h, n: (b, n, h)),
            pl.BlockSpec((1, chunk, dk), lambda b, h, n: (b, n, kq_blocks + h)),
            pl.BlockSpec((1, chunk, dv), lambda b, h, n: (b, n, v_blocks + h)),
            pl.BlockSpec((1, chunk, dv), lambda b, h, n: (b, n, r_blocks + h)),
            pl.BlockSpec((1, chunk, dk), lambda b, h, n: (b, n, h)),
            pl.BlockSpec((1, dv), lambda b, h, n: (0, 0)),
        ],
        out_specs=pl.BlockSpec((1, chunk, dv), lambda b, h, n: (b, n, h)),
        out_shape=jax.ShapeDtypeStruct((bsz, s, GLA_HEADS * dv), BF),
        scratch_shapes=[
            pltpu.VMEM((dv, dk), F32),
            pltpu.VMEM((chunk, (chunk // sub - 1) * dk), BF),
            pltpu.VMEM((chunk, (chunk // sub - 1) * dk), BF),
        ],
        compiler_params=_params(("parallel", "parallel", "arbitrary"), 32),
        name="gla_core",
    )(proj, proj, proj, proj, la, gw)


def _proj_res_kernel(x_ref, w_ref, res_ref, o_ref):
    o_ref[...] = res_ref[...] + _dot(x_ref[...], w_ref[...])


def _proj_res(x, w, res, *, tm):
    t, kdim = x.shape
    n = w.shape[1]
    return pl.pallas_call(
        _proj_res_kernel,
        grid=(t // tm,),
        in_specs=[
            pl.BlockSpec((tm, kdim), lambda i: (i, 0)),
            pl.BlockSpec((kdim, n), lambda i: (0, 0)),
            pl.BlockSpec((tm, n), lambda i: (i, 0)),
        ],
        out_specs=pl.BlockSpec((tm, n), lambda i: (i, 0)),
        out_shape=jax.ShapeDtypeStruct((t, n), F32),
        compiler_params=_params(("parallel",), 48),
        name="proj_res",
    )(x, w, res)


def _ffn_kernel(h_ref, nw_ref, wg_ref, wu_ref, wd_ref, fw_ref, o_ref, xn_ref, acc_ref, *, final_norm):
    f = pl.program_id(1)

    @pl.when(f == 0)
    def _():
        xn_ref[...] = _rms(h_ref[...], nw_ref[...]).astype(BF)
        acc_ref[...] = jnp.zeros_like(acc_ref)

    xn = xn_ref[...]
    act = (_silu(_dot(xn, wg_ref[...])) * _dot(xn, wu_ref[...])).astype(BF)
    acc_ref[...] += _dot(act, wd_ref[...])

    @pl.when(f == pl.num_programs(1) - 1)
    def _():
        y = h_ref[...] + acc_ref[...]
        if final_norm:
            y = _rms(y, fw_ref[...])
        o_ref[...] = y


def _ffn(h, nw, w_gate_up, w_down, fw, *, tm, tf, final_norm):
    t, d = h.shape
    d_ff = w_down.shape[0]
    nf = d_ff // tf
    return pl.pallas_call(
        functools.partial(_ffn_kernel, final_norm=final_norm),
        grid=(t // tm, nf),
        in_specs=[
            pl.BlockSpec((tm, d), lambda i, f: (i, 0)),
            pl.BlockSpec((1, d), lambda i, f: (0, 0)),
            pl.BlockSpec((d, tf), lambda i, f: (0, f)),
            pl.BlockSpec((d, tf), lambda i, f: (0, nf + f)),
            pl.BlockSpec((tf, d), lambda i, f: (f, 0)),
            pl.BlockSpec((1, d), lambda i, f: (0, 0)),
        ],
        out_specs=pl.BlockSpec((tm, d), lambda i, f: (i, 0)),
        out_shape=jax.ShapeDtypeStruct((t, d), F32),
        scratch_shapes=[pltpu.VMEM((tm, d), BF), pltpu.VMEM((tm, d), F32)],
        compiler_params=_params(("parallel", "arbitrary"), 48),
        name="ffn",
    )(h, nw, w_gate_up, w_gate_up, w_down, fw)


_EXP_UNDERFLOW = -88.0


def _sb_kernel(q_ref, k_ref, v_ref, o_ref, *, tq, group):
    s = q_ref.shape[1]
    tw = 2 * tq
    def later(n):
        r = lax.broadcasted_iota(jnp.int32, (n, n), 0)
        c = lax.broadcasted_iota(jnp.int32, (n, n), 1)
        return jnp.where(r > c, 1.0, 0.0).astype(BF)

    def causal(n_keys, offset):
        qr = lax.broadcasted_iota(jnp.int32, (tq, n_keys), 0)
        kc = lax.broadcasted_iota(jnp.int32, (tq, n_keys), 1)
        return kc < qr + offset

    later_w, later_b = later(tw), later(tq)
    diag_mask = causal(tq, 0)
    win_mask = causal(tw, tq)

    def scores(q, start, width, mask):
        kb = k_ref[0, pl.ds(start, width), :]
        z = lax.dot_general(q, kb, _NT, preferred_element_type=F32)
        lf = _log_sigmoid(-z)
        if mask is not None:
            lf = jnp.where(mask, lf, 0.0)
        return z, lf

    def suffix(lf, later, carry):
        lf_hi, lf_lo = _split2(lf)
        after = _dot(lf_hi, later) + _dot(lf_lo, later)
        if carry is not None:
            after = after + carry
        return after, after[:, :1] + lf[:, :1]

    def weighted(z, lf, after, start, width, mask):
        a = jnp.exp(z + lf + after)
        if mask is not None:
            a = jnp.where(mask, a, 0.0)
        return _dot(a.astype(BF), v_ref[0, pl.ds(start, width), :])

    def window(tile):
        if isinstance(tile, int):
            if tile == 0:
                return 0, tq, later_b, diag_mask
            return (tile - 1) * tq, tw, later_w, win_mask
        return pl.multiple_of((tile - 1) * tq, tq), tw, later_w, win_mask

    def row0(tile):
        return tile * tq if isinstance(tile, int) else pl.multiple_of(tile * tq, tq)

    def extend(tile, q, carry, top, acc):
        def cond(st):
            return jnp.logical_and(st[0] >= 0, st[1] > _EXP_UNDERFLOW)

        def body(st):
            j, _, carry, acc = st
            start = pl.multiple_of(j * tq, tq)
            z, lf = scores(q, start, tq, None)
            after, carry = suffix(lf, later_b, carry)
            acc = acc + weighted(z, lf, after, start, tq, None)
            return j - 1, jnp.max(carry), carry, acc

        init = (jnp.asarray(tile - 2, jnp.int32), top, carry, acc)
        return lax.while_loop(cond, body, init)[3]

    def run_group(base):
        tiles = [base + g for g in range(group)]
        wins = [window(t) for t in tiles]
        qs = [q_ref[0, pl.ds(row0(t), tq), :] for t in tiles]
        zl = [scores(q, w[0], w[1], w[3]) for q, w in zip(qs, wins)]
        ac = [suffix(lf, w[2], None) for (_, lf), w in zip(zl, wins)]
        accs = [weighted(z, lf, after, w[0], w[1], w[3]) for (z, lf), (after, _), w in zip(zl, ac, wins)]
        tops = [jnp.max(carry) for _, carry in ac]
        for t, q, (_, carry), top, acc in zip(tiles, qs, ac, tops, accs):
            acc = extend(t, q, carry, top, acc)
            o_ref[0, pl.ds(row0(t), tq), :] = acc.astype(o_ref.dtype)

    run_group(0)

    def loop_body(it, carry):
        run_group(it * group)
        return carry

    lax.fori_loop(1, s // (tq * group), loop_body, 0)


def _sb_attention(qp, kv, *, tq, group):
    bsz, s, w = qp.shape
    hd = w // SB_HEADS
    return pl.pallas_call(
        functools.partial(_sb_kernel, tq=tq, group=group),
        grid=(bsz, SB_HEADS),
        in_specs=[
            pl.BlockSpec((1, s, hd), lambda b, h: (b, 0, h)),
            pl.BlockSpec((1, s, hd), lambda b, h: (b, 0, h)),
            pl.BlockSpec((1, s, hd), lambda b, h: (b, 0, SB_HEADS + h)),
        ],
        out_specs=pl.BlockSpec((1, s, hd), lambda b, h: (b, 0, h)),
        out_shape=jax.ShapeDtypeStruct((bsz, s, w), BF),
        compiler_params=_params(("parallel", "parallel"), 32),
        name="sb_attn",
    )(qp, kv, kv)


def kernel(x, attn_norm_w, ffn_norm_w, gla_w_in, gla_w_gate_up, gla_b_gate, gla_gnorm_w, gla_w_out,
           kv_norm_w, sb_w_kv, sb_w_q, sb_w_out, ffn_w_gate_up, ffn_w_down, final_norm_w):
    bsz, s, d = x.shape
    t = bsz * s
    depth = attn_norm_w.shape[0]
    n_gla = gla_w_in.shape[0]
    rank = gla_w_gate_up.shape[1]
    main_w = gla_w_in.shape[2] - rank
    hd = d // SB_HEADS
    row = lambda v: v.reshape(1, -1)

    h = x.reshape(t, d)
    kv = None
    for layer in range(depth):
        if layer < n_gla:
            w_in = gla_w_in[layer]
            wgl = jnp.pad(w_in[:, main_w:], ((0, 0), (0, LANES - rank))).astype(BF)
            wup = jnp.pad(gla_w_gate_up[layer], ((0, LANES - rank), (0, 0)))
            proj, la = _gla_inproj(h, row(attn_norm_w[layer]), w_in[:, :main_w].astype(BF), wgl, wup,
                                   row(gla_b_gate[layer]), tm=512, tn=1024)
            o = _gla_core(proj.reshape(bsz, s, main_w), la.reshape(bsz, s, -1),
                          row(gla_gnorm_w[layer]), chunk=128, sub=16)
            h = _proj_res(o.reshape(t, -1), gla_w_out[layer].astype(BF), h, tm=512)
        else:
            j = layer - n_gla
            qp = _norm_matmul(h, row(attn_norm_w[layer]), sb_w_q[j].astype(BF), tm=1024, tn=1024,
                              out_scale=hd ** -0.5)
            o = _sb_attention(qp.reshape(bsz, s, -1), kv, tq=128, group=4)
            h = _proj_res(o.reshape(t, -1), sb_w_out[j].astype(BF), h, tm=512)
        last = layer == depth - 1
        h = _ffn(h, row(ffn_norm_w[layer]), ffn_w_gate_up[layer].astype(BF), ffn_w_down[layer].astype(BF),
                 row(final_norm_w), tm=512, tf=512, final_norm=last)
        if layer == n_gla - 1:
            kv = _norm_matmul(h, row(kv_norm_w), sb_w_kv.astype(BF), tm=1024, tn=1024).reshape(bsz, s, -1)
    return h.reshape(bsz, s, d)
```

```python
import functools

import jax
import jax.numpy as jnp
from jax import lax
from jax.experimental import pallas as pl
from jax.experimental.pallas import tpu as pltpu

EPS = 1e-6
GLA_HEADS = 4
GLA_GATE_TAU = 16.0
SB_HEADS = 16
BF = jnp.bfloat16
F32 = jnp.float32
LANES = 128
MIB = 2 ** 20

_NT = (((1,), (1,)), ((), ()))
_TN = (((0,), (0,)), ((), ()))


def _params(sem, vmem_mib):
    return pltpu.CompilerParams(dimension_semantics=sem, vmem_limit_bytes=vmem_mib * MIB)


def _dot(a, b):
    return jnp.dot(a, b, preferred_element_type=F32)


def _rms(x, w):
    return x * lax.rsqrt(jnp.mean(x * x, axis=-1, keepdims=True) + EPS) * w


def _split2(x):
    hi = x.astype(BF)
    lo = (x - hi.astype(F32)).astype(BF)
    return hi, lo


def _log_sigmoid(x):
    return jnp.minimum(x, 0.0) - jnp.log(1.0 + jnp.exp(-jnp.abs(x)))


def _silu(x):
    return x * (1.0 / (1.0 + jnp.exp(-x)))


def _norm_matmul2_kernel(x_ref, nwa_ref, nwb_ref, wa_ref, wb_ref, oa_ref, ob_ref, xn_ref, *, na, scale_a):
    j = pl.program_id(1)

    @pl.when(j == 0)
    def _():
        x = x_ref[...]
        xhat = x * lax.rsqrt(jnp.mean(x * x, axis=-1, keepdims=True) + EPS)
        xn_ref[0] = (xhat * nwa_ref[...]).astype(BF)
        xn_ref[1] = (xhat * nwb_ref[...]).astype(BF)

    @pl.when(j < na)
    def _():
        oa_ref[...] = (_dot(xn_ref[0], wa_ref[...]) * scale_a).astype(oa_ref.dtype)

    @pl.when(j >= na)
    def _():
        ob_ref[...] = _dot(xn_ref[1], wb_ref[...]).astype(ob_ref.dtype)


def _norm_matmul2(x, nwa, nwb, wa, wb, *, tm, tn, scale_a):
    t, d = x.shape
    na, nb = wa.shape[1] // tn, wb.shape[1] // tn
    return pl.pallas_call(
        functools.partial(_norm_matmul2_kernel, na=na, scale_a=scale_a),
        grid=(t // tm, na + nb),
        in_specs=[
            pl.BlockSpec((tm, d), lambda i, j: (i, 0)),
            pl.BlockSpec((1, d), lambda i, j: (0, 0)),
            pl.BlockSpec((1, d), lambda i, j: (0, 0)),
            pl.BlockSpec((d, tn), lambda i, j: (0, jnp.minimum(j, na - 1))),
            pl.BlockSpec((d, tn), lambda i, j: (0, jnp.maximum(j - na, 0))),
        ],
        out_specs=[
            pl.BlockSpec((tm, tn), lambda i, j: (i, jnp.minimum(j, na - 1))),
            pl.BlockSpec((tm, tn), lambda i, j: (i, jnp.maximum(j - na, 0))),
        ],
        out_shape=[
            jax.ShapeDtypeStruct((t, wa.shape[1]), BF),
            jax.ShapeDtypeStruct((t, wb.shape[1]), BF),
        ],
        scratch_shapes=[pltpu.VMEM((2, tm, d), BF)],
        compiler_params=_params(("parallel", "arbitrary"), 56),
        name="norm_matmul2",
    )(x, nwa, nwb, wa, wb)


def _gla_inproj_kernel(x_ref, nw_ref, w_ref, wgl_ref, wup_ref, bg_ref, o_ref, la_ref, xn_ref):
    @pl.when(pl.program_id(1) == 0)
    def _():
        xn = _rms(x_ref[...], nw_ref[...]).astype(BF)
        xn_ref[...] = xn
        gl_hi, gl_lo = _split2(_dot(xn, wgl_ref[...]))
        up_hi, up_lo = _split2(wup_ref[...])
        g = _dot(gl_hi, up_hi) + _dot(gl_lo, up_hi) + _dot(gl_hi, up_lo) + bg_ref[...]
        la_ref[...] = _log_sigmoid(g) * (1.0 / GLA_GATE_TAU)

    o_ref[...] = _dot(xn_ref[...], w_ref[...]).astype(o_ref.dtype)


def _gla_inproj(x, nw, w_all, layer, n, wgl, wup, bg, *, tm, tn):
    t, d = x.shape
    qk_w = wup.shape[1]
    return pl.pallas_call(
        _gla_inproj_kernel,
        grid=(t // tm, n // tn),
        in_specs=[
            pl.BlockSpec((tm, d), lambda i, j: (i, 0)),
            pl.BlockSpec((1, d), lambda i, j: (0, 0)),
            pl.BlockSpec((None, d, tn), lambda i, j: (layer, 0, j)),
            pl.BlockSpec(wgl.shape, lambda i, j: (0, 0)),
            pl.BlockSpec(wup.shape, lambda i, j: (0, 0)),
            pl.BlockSpec((1, qk_w), lambda i, j: (0, 0)),
        ],
        out_specs=[
            pl.BlockSpec((tm, tn), lambda i, j: (i, j)),
            pl.BlockSpec((tm, qk_w), lambda i, j: (i, 0)),
        ],
        out_shape=[
            jax.ShapeDtypeStruct((t, n), BF),
            jax.ShapeDtypeStruct((t, qk_w), F32),
        ],
        scratch_shapes=[pltpu.VMEM((tm, d), BF)],
        compiler_params=_params(("parallel", "arbitrary"), 56),
        name="gla_inproj",
    )(x, nw, w_all, wgl, wup, bg)


def _gla_kernel(q_ref, k_ref, v_ref, r_ref, g_ref, gw_ref, o_ref, st_ref, qh_ref, kh_ref,
                *, chunk, sub, heads, scale):
    n_sub = chunk // sub
    dk = q_ref.shape[-1] // heads
    dv = v_ref.shape[-1] // heads
    hs = range(heads)

    @pl.when(pl.program_id(2) == 0)
    def _():
        st_ref[...] = jnp.zeros_like(st_ref)
        qh_ref[...] = jnp.zeros_like(qh_ref)
        kh_ref[...] = jnp.zeros_like(kh_ref)

    def head_cols(ref, h, width):
        return ref[0, :, h * width:(h + 1) * width]

    q = [head_cols(q_ref, h, dk).astype(F32) * scale for h in hs]
    k = [head_cols(k_ref, h, dk).astype(F32) for h in hs]
    g = [head_cols(g_ref, h, dk) for h in hs]

    row = lax.broadcasted_iota(jnp.int32, (chunk, chunk), 0)
    col = lax.broadcasted_iota(jnp.int32, (chunk, chunk), 1)
    tri = jnp.where(col <= row, 1.0, 0.0).astype(BF)
    diff = jnp.where(row // sub == col // sub, row - col, -1)

    b = []
    for h in hs:
        g_hi, g_lo = _split2(g[h])
        b.append(_dot(tri, g_hi) + _dot(tri, g_lo))

    st = [st_ref[h] for h in hs]
    o = [lax.dot_general((q[h] * jnp.exp(b[h])).astype(BF), st[h].astype(BF), _NT,
                         preferred_element_type=F32) for h in hs]

    scores = []
    for h in hs:
        k_run = None
        for i in range(1, n_sub):
            beta = b[h][i * sub - 1:i * sub, :]
            rows = slice(i * sub, (i + 1) * sub)
            prev = slice((i - 1) * sub, i * sub)
            cols = slice((i - 1) * dk, i * dk)
            qh_ref[h, rows, cols] = (q[h][rows] * jnp.exp(b[h][rows] - beta)).astype(BF)
            k_new = k[h][prev] * jnp.exp(beta - b[h][prev])
            if k_run is None:
                k_run = k_new
            else:
                k_run = jnp.concatenate([k_run * jnp.exp(beta - beta_prev), k_new], axis=0)
            kh_ref[h, 0:i * sub, cols] = k_run.astype(BF)
            beta_prev = beta
        scores.append(lax.dot_general(qh_ref[h], kh_ref[h], _NT, preferred_element_type=F32))

    for h in hs:
        decay = jnp.exp(g[h])
        w = k[h]
        for d in range(sub):
            if d > 0:
                w = decay * pltpu.roll(w, 1, 0)
            scores[h] = jnp.where(diff == d, jnp.sum(q[h] * w, axis=-1, keepdims=True), scores[h])

    v = [head_cols(v_ref, h, dv) for h in hs]
    o = [o[h] + _dot(scores[h].astype(BF), v[h]) for h in hs]

    for h in hs:
        b_last = b[h][chunk - 1:chunk, :]
        k_dec = (k[h] * jnp.exp(b_last - b[h])).astype(BF)
        st_ref[h] = st[h] * jnp.exp(b_last) + lax.dot_general(v[h], k_dec, _TN, preferred_element_type=F32)

    for h in hs:
        r = head_cols(r_ref, h, dv).astype(F32)
        o_ref[0, :, h * dv:(h + 1) * dv] = (_rms(o[h], gw_ref[...]) * _silu(r)).astype(o_ref.dtype)


def _gla_core(proj, la, gw, *, chunk, sub, heads):
    bsz, s, _ = proj.shape
    dk = la.shape[-1] // GLA_HEADS
    dv = gw.shape[-1]
    groups = GLA_HEADS // heads
    k_blocks = groups
    v_blocks = 2 * GLA_HEADS * dk // (heads * dv)
    r_blocks = v_blocks + groups
    expanded = (chunk // sub - 1) * dk
    return pl.pallas_call(
        functools.partial(_gla_kernel, chunk=chunk, sub=sub, heads=heads, scale=dk ** -0.5),
        grid=(bsz, groups, s // chunk),
        in_specs=[
            pl.BlockSpec((1, chunk, heads * dk), lambda b, h, n: (b, n, h)),
            pl.BlockSpec((1, chunk, heads * dk), lambda b, h, n: (b, n, k_blocks + h)),
            pl.BlockSpec((1, chunk, heads * dv), lambda b, h, n: (b, n, v_blocks + h)),
            pl.BlockSpec((1, chunk, heads * dv), lambda b, h, n: (b, n, r_blocks + h)),
            pl.BlockSpec((1, chunk, heads * dk), lambda b, h, n: (b, n, h)),
            pl.BlockSpec((1, dv), lambda b, h, n: (0, 0)),
        ],
        out_specs=pl.BlockSpec((1, chunk, heads * dv), lambda b, h, n: (b, n, h)),
        out_shape=jax.ShapeDtypeStruct((bsz, s, GLA_HEADS * dv), BF),
        scratch_shapes=[
            pltpu.VMEM((heads, dv, dk), F32),
            pltpu.VMEM((heads, chunk, expanded), BF),
            pltpu.VMEM((heads, chunk, expanded), BF),
        ],
        compiler_params=_params(("parallel", "parallel", "arbitrary"), 32),
        name="gla_core",
    )(proj, proj, proj, proj, la, gw)


def _proj_res_kernel(x_ref, w_ref, res_ref, o_ref):
    o_ref[...] = res_ref[...] + _dot(x_ref[...], w_ref[...])


def _proj_res(x, w, res, *, tm):
    t, kdim = x.shape
    n = w.shape[1]
    return pl.pallas_call(
        _proj_res_kernel,
        grid=(t // tm,),
        in_specs=[
            pl.BlockSpec((tm, kdim), lambda i: (i, 0)),
            pl.BlockSpec((kdim, n), lambda i: (0, 0)),
            pl.BlockSpec((tm, n), lambda i: (i, 0)),
        ],
        out_specs=pl.BlockSpec((tm, n), lambda i: (i, 0)),
        out_shape=jax.ShapeDtypeStruct((t, n), F32),
        compiler_params=_params(("parallel",), 48),
        name="proj_res",
    )(x, w, res)


def _ffn_kernel(h_ref, nw_ref, wg_ref, wu_ref, wd_ref, fw_ref, o_ref, xn_ref, *, final_norm):
    f = pl.program_id(1)

    @pl.when(f == 0)
    def _():
        h = h_ref[...]
        xn_ref[...] = _rms(h, nw_ref[...]).astype(BF)
        o_ref[...] = h

    xn = xn_ref[...]
    act = (_silu(_dot(xn, wg_ref[...])) * _dot(xn, wu_ref[...])).astype(BF)
    o_ref[...] += _dot(act, wd_ref[...])

    if final_norm:
        @pl.when(f == pl.num_programs(1) - 1)
        def _():
            o_ref[...] = _rms(o_ref[...], fw_ref[...])


def _ffn(h, nw, w_gate_up, w_down, layer, fw, *, tm, tf, final_norm):
    t, d = h.shape
    d_ff = w_down.shape[1]
    nf = d_ff // tf
    return pl.pallas_call(
        functools.partial(_ffn_kernel, final_norm=final_norm),
        grid=(t // tm, nf),
        in_specs=[
            pl.BlockSpec((tm, d), lambda i, f: (i, 0)),
            pl.BlockSpec((1, d), lambda i, f: (0, 0)),
            pl.BlockSpec((None, d, tf), lambda i, f: (layer, 0, f)),
            pl.BlockSpec((None, d, tf), lambda i, f: (layer, 0, nf + f)),
            pl.BlockSpec((None, tf, d), lambda i, f: (layer, f, 0)),
            pl.BlockSpec((1, d), lambda i, f: (0, 0)),
        ],
        out_specs=pl.BlockSpec((tm, d), lambda i, f: (i, 0)),
        out_shape=jax.ShapeDtypeStruct((t, d), F32),
        scratch_shapes=[pltpu.VMEM((tm, d), BF)],
        compiler_params=_params(("parallel", "arbitrary"), 56),
        name="ffn",
    )(h, nw, w_gate_up, w_gate_up, w_down, fw)


_EXP_UNDERFLOW = -88.0


def _sb_kernel(q_ref, k_ref, v_ref, o_ref, *, tq, group):
    s = q_ref.shape[1]
    tw = 2 * tq
    def later(n):
        r = lax.broadcasted_iota(jnp.int32, (n, n), 0)
        c = lax.broadcasted_iota(jnp.int32, (n, n), 1)
        return jnp.where(r > c, 1.0, 0.0).astype(BF)

    def causal(n_keys, offset):
        qr = lax.broadcasted_iota(jnp.int32, (tq, n_keys), 0)
        kc = lax.broadcasted_iota(jnp.int32, (tq, n_keys), 1)
        return kc < qr + offset

    later_w, later_b = later(tw), later(tq)
    diag_mask = causal(tq, 0)
    win_mask = causal(tw, tq)

    def scores(q, start, width, mask):
        kb = k_ref[0, pl.ds(start, width), :]
        z = lax.dot_general(q, kb, _NT, preferred_element_type=F32)
        lf = _log_sigmoid(-z)
        if mask is not None:
            lf = jnp.where(mask, lf, 0.0)
        return z, lf

    def suffix(lf, later, carry):
        lf_hi, lf_lo = _split2(lf)
        after = _dot(lf_hi, later) + _dot(lf_lo, later)
        if carry is not None:
            after = after + carry
        return after, after[:, :1] + lf[:, :1]

    def weighted(z, lf, after, start, width, mask):
        a = jnp.exp(z + lf + after)
        if mask is not None:
            a = jnp.where(mask, a, 0.0)
        return _dot(a.astype(BF), v_ref[0, pl.ds(start, width), :])

    def window(tile):
        if isinstance(tile, int):
            if tile == 0:
                return 0, tq, later_b, diag_mask
            return (tile - 1) * tq, tw, later_w, win_mask
        return pl.multiple_of((tile - 1) * tq, tq), tw, later_w, win_mask

    def row0(tile):
        return tile * tq if isinstance(tile, int) else pl.multiple_of(tile * tq, tq)

    def extend(tile, q, carry, top, acc):
        def cond(st):
            return jnp.logical_and(st[0] >= 0, st[1] > _EXP_UNDERFLOW)

        def body(st):
            j, _, carry, acc = st
            start = pl.multiple_of(j * tq, tq)
            z, lf = scores(q, start, tq, None)
            after, carry = suffix(lf, later_b, carry)
            acc = acc + weighted(z, lf, after, start, tq, None)
            return j - 1, jnp.max(carry), carry, acc

        init = (jnp.asarray(tile - 2, jnp.int32), top, carry, acc)
        return lax.while_loop(cond, body, init)[3]

    def run_group(base):
        tiles = [base + g for g in range(group)]
        wins = [window(t) for t in tiles]
        qs = [q_ref[0, pl.ds(row0(t), tq), :] for t in tiles]
        zl = [scores(q, w[0], w[1], w[3]) for q, w in zip(qs, wins)]
        ac = [suffix(lf, w[2], None) for (_, lf), w in zip(zl, wins)]
        accs = [weighted(z, lf, after, w[0], w[1], w[3]) for (z, lf), (after, _), w in zip(zl, ac, wins)]
        tops = [jnp.max(carry) for _, carry in ac]
        for t, q, (_, carry), top, acc in zip(tiles, qs, ac, tops, accs):
            acc = extend(t, q, carry, top, acc)
            o_ref[0, pl.ds(row0(t), tq), :] = acc.astype(o_ref.dtype)

    run_group(0)

    def loop_body(it, carry):
        run_group(it * group)
        return carry

    lax.fori_loop(1, s // (tq * group), loop_body, 0)


def _sb_attention(qp, kv, *, tq, group):
    bsz, s, w = qp.shape
    hd = w // SB_HEADS
    return pl.pallas_call(
        functools.partial(_sb_kernel, tq=tq, group=group),
        grid=(bsz, SB_HEADS),
        in_specs=[
            pl.BlockSpec((1, s, hd), lambda b, h: (b, 0, h)),
            pl.BlockSpec((1, s, hd), lambda b, h: (b, 0, h)),
            pl.BlockSpec((1, s, hd), lambda b, h: (b, 0, SB_HEADS + h)),
        ],
        out_specs=pl.BlockSpec((1, s, hd), lambda b, h: (b, 0, h)),
        out_shape=jax.ShapeDtypeStruct((bsz, s, w), BF),
        compiler_params=_params(("parallel", "parallel"), 32),
        name="sb_attn",
    )(qp, kv, kv)


def kernel(x, attn_norm_w, ffn_norm_w, gla_w_in, gla_w_gate_up, gla_b_gate, gla_gnorm_w, gla_w_out,
           kv_norm_w, sb_w_kv, sb_w_q, sb_w_out, ffn_w_gate_up, ffn_w_down, final_norm_w):
    bsz, s, d = x.shape
    t = bsz * s
    depth = attn_norm_w.shape[0]
    n_gla = gla_w_in.shape[0]
    rank = gla_w_gate_up.shape[1]
    main_w = gla_w_in.shape[2] - rank
    hd = d // SB_HEADS
    row = lambda v: v.reshape(1, -1)
    assert n_gla >= 1 and depth - n_gla == 1

    w_in_all = gla_w_in.astype(BF)
    w_gate_up_all = ffn_w_gate_up.astype(BF)
    w_down_all = ffn_w_down.astype(BF)

    h = x.reshape(t, d)
    for layer in range(depth):
        if layer < n_gla:
            wgl = jnp.pad(w_in_all[layer, :, main_w:], ((0, 0), (0, LANES - rank)))
            wup = jnp.pad(gla_w_gate_up[layer], ((0, LANES - rank), (0, 0)))
            proj, la = _gla_inproj(h, row(attn_norm_w[layer]), w_in_all, layer, main_w, wgl, wup,
                                   row(gla_b_gate[layer]), tm=1024, tn=512)
            o = _gla_core(proj.reshape(bsz, s, main_w), la.reshape(bsz, s, -1),
                          row(gla_gnorm_w[layer]), chunk=128, sub=16, heads=GLA_HEADS)
            h = _proj_res(o.reshape(t, -1), gla_w_out[layer].astype(BF), h, tm=512)
        else:
            j = layer - n_gla
            qp, kv = _norm_matmul2(h, row(attn_norm_w[layer]), row(kv_norm_w), sb_w_q[j].astype(BF),
                                   sb_w_kv.astype(BF), tm=1024, tn=512, scale_a=hd ** -0.5)
            o = _sb_attention(qp.reshape(bsz, s, -1), kv.reshape(bsz, s, -1), tq=128, group=4)
            h = _proj_res(o.reshape(t, -1), sb_w_out[j].astype(BF), h, tm=512)
        h = _ffn(h, row(ffn_norm_w[layer]), w_gate_up_all, w_down_all, layer, row(final_norm_w),
                 tm=1024, tf=256, final_norm=layer == depth - 1)
    return h.reshape(bsz, s, d)
```

```python
import functools

import jax
import jax.numpy as jnp
from jax import lax
from jax.experimental import pallas as pl
from jax.experimental.pallas import tpu as pltpu

EPS = 1e-6
GLA_HEADS = 4
GLA_GATE_TAU = 16.0
SB_HEADS = 16
BF = jnp.bfloat16
F32 = jnp.float32
LANES = 128
MIB = 2 ** 20

_NT = (((1,), (1,)), ((), ()))
_TN = (((0,), (0,)), ((), ()))


def _params(sem, vmem_mib):
    return pltpu.CompilerParams(dimension_semantics=sem, vmem_limit_bytes=vmem_mib * MIB)


def _dot(a, b):
    return jnp.dot(a, b, preferred_element_type=F32)


def _rms(x, w):
    return x * lax.rsqrt(jnp.mean(x * x, axis=-1, keepdims=True) + EPS) * w


def _split2(x):
    hi = x.astype(BF)
    lo = (x - hi.astype(F32)).astype(BF)
    return hi, lo


def _log_sigmoid(x):
    return jnp.minimum(x, 0.0) - jnp.log(1.0 + jnp.exp(-jnp.abs(x)))


def _silu(x):
    return x * (1.0 / (1.0 + jnp.exp(-x)))


def _norm_matmul2_kernel(x_ref, nwa_ref, nwb_ref, w_ref, o_ref, xn_ref, *, na, scale_a):
    j = pl.program_id(1)

    @pl.when(j == 0)
    def _():
        x = x_ref[...]
        xhat = x * lax.rsqrt(jnp.mean(x * x, axis=-1, keepdims=True) + EPS)
        xn_ref[0] = (xhat * nwa_ref[...]).astype(BF)
        xn_ref[1] = (xhat * nwb_ref[...]).astype(BF)

    @pl.when(j < na)
    def _():
        o_ref[...] = (_dot(xn_ref[0], w_ref[...]) * scale_a).astype(o_ref.dtype)

    @pl.when(j >= na)
    def _():
        o_ref[...] = _dot(xn_ref[1], w_ref[...]).astype(o_ref.dtype)


def _norm_matmul2(x, nwa, nwb, w, n_a, *, tm, tn, scale_a):
    t, d = x.shape
    n = w.shape[1]
    return pl.pallas_call(
        functools.partial(_norm_matmul2_kernel, na=n_a // tn, scale_a=scale_a),
        grid=(t // tm, n // tn),
        in_specs=[
            pl.BlockSpec((tm, d), lambda i, j: (i, 0)),
            pl.BlockSpec((1, d), lambda i, j: (0, 0)),
            pl.BlockSpec((1, d), lambda i, j: (0, 0)),
            pl.BlockSpec((d, tn), lambda i, j: (0, j)),
        ],
        out_specs=pl.BlockSpec((tm, tn), lambda i, j: (i, j)),
        out_shape=jax.ShapeDtypeStruct((t, n), BF),
        scratch_shapes=[pltpu.VMEM((2, tm, d), BF)],
        compiler_params=_params(("parallel", "arbitrary"), 56),
        name="norm_matmul2",
    )(x, nwa, nwb, w)


def _gla_inproj_kernel(x_ref, nw_ref, w_ref, wgl_ref, wup_ref, bg_ref, o_ref, la_ref, xn_ref, gl_ref):
    @pl.when(pl.program_id(1) == 0)
    def _():
        xn = _rms(x_ref[...], nw_ref[...]).astype(BF)
        xn_ref[...] = xn
        gl_ref[0], gl_ref[1] = _split2(_dot(xn, wgl_ref[...]))

    up_hi, up_lo = _split2(wup_ref[...])
    g = _dot(gl_ref[0], up_hi) + _dot(gl_ref[1], up_hi) + _dot(gl_ref[0], up_lo) + bg_ref[...]
    la_ref[...] = _log_sigmoid(g) * (1.0 / GLA_GATE_TAU)
    o_ref[...] = _dot(xn_ref[...], w_ref[...]).astype(o_ref.dtype)


def _gla_inproj(x, nw, w_all, layer, n, wgl, wup, bg, *, tm, tn):
    t, d = x.shape
    steps = n // tn
    slab = wup.shape[1] // steps
    assert slab * steps == wup.shape[1] and slab % LANES == 0
    return pl.pallas_call(
        _gla_inproj_kernel,
        grid=(t // tm, steps),
        in_specs=[
            pl.BlockSpec((tm, d), lambda i, j: (i, 0)),
            pl.BlockSpec((1, d), lambda i, j: (0, 0)),
            pl.BlockSpec((None, d, tn), lambda i, j: (layer, 0, j)),
            pl.BlockSpec(wgl.shape, lambda i, j: (0, 0)),
            pl.BlockSpec((wup.shape[0], slab), lambda i, j: (0, j)),
            pl.BlockSpec((1, slab), lambda i, j: (0, j)),
        ],
        out_specs=[
            pl.BlockSpec((tm, tn), lambda i, j: (i, j)),
            pl.BlockSpec((tm, slab), lambda i, j: (i, j)),
        ],
        out_shape=[
            jax.ShapeDtypeStruct((t, n), BF),
            jax.ShapeDtypeStruct((t, wup.shape[1]), F32),
        ],
        scratch_shapes=[pltpu.VMEM((tm, d), BF), pltpu.VMEM((2, tm, wgl.shape[1]), BF)],
        compiler_params=_params(("parallel", "arbitrary"), 56),
        name="gla_inproj",
    )(x, nw, w_all, wgl, wup, bg)


def _gla_kernel(q_ref, k_ref, v_ref, r_ref, g_ref, gw_ref, o_ref, st_ref, qh_ref, kh_ref,
                *, chunk, sub, heads, scale):
    n_sub = chunk // sub
    dk = q_ref.shape[-1] // heads
    dv = v_ref.shape[-1] // heads
    hs = range(heads)

    @pl.when(pl.program_id(2) == 0)
    def _():
        st_ref[...] = jnp.zeros_like(st_ref)
        qh_ref[...] = jnp.zeros_like(qh_ref)
        kh_ref[...] = jnp.zeros_like(kh_ref)

    def head_cols(ref, h, width):
        return ref[0, :, h * width:(h + 1) * width]

    q = [head_cols(q_ref, h, dk).astype(F32) * scale for h in hs]
    k = [head_cols(k_ref, h, dk).astype(F32) for h in hs]
    g = [head_cols(g_ref, h, dk) for h in hs]

    row = lax.broadcasted_iota(jnp.int32, (chunk, chunk), 0)
    col = lax.broadcasted_iota(jnp.int32, (chunk, chunk), 1)
    tri = jnp.where(col <= row, 1.0, 0.0).astype(BF)
    diff = jnp.where(row // sub == col // sub, row - col, -1)

    b = []
    for h in hs:
        g_hi, g_lo = _split2(g[h])
        b.append(_dot(tri, g_hi) + _dot(tri, g_lo))

    st = [st_ref[h] for h in hs]
    o = [lax.dot_general((q[h] * jnp.exp(b[h])).astype(BF), st[h].astype(BF), _NT,
                         preferred_element_type=F32) for h in hs]

    scores = []
    for h in hs:
        k_run = None
        for i in range(1, n_sub):
            beta = b[h][i * sub - 1:i * sub, :]
            rows = slice(i * sub, (i + 1) * sub)
            prev = slice((i - 1) * sub, i * sub)
            cols = slice((i - 1) * dk, i * dk)
            qh_ref[h, rows, cols] = (q[h][rows] * jnp.exp(b[h][rows] - beta)).astype(BF)
            k_new = k[h][prev] * jnp.exp(beta - b[h][prev])
            if k_run is None:
                k_run = k_new
            else:
                k_run = jnp.concatenate([k_run * jnp.exp(beta - beta_prev), k_new], axis=0)
            kh_ref[h, 0:i * sub, cols] = k_run.astype(BF)
            beta_prev = beta
        scores.append(lax.dot_general(qh_ref[h], kh_ref[h], _NT, preferred_element_type=F32))

    for h in hs:
        decay = jnp.exp(g[h])
        w = k[h]
        for d in range(sub):
            if d > 0:
                w = decay * pltpu.roll(w, 1, 0)
            scores[h] = jnp.where(diff == d, jnp.sum(q[h] * w, axis=-1, keepdims=True), scores[h])

    v = [head_cols(v_ref, h, dv) for h in hs]
    o = [o[h] + _dot(scores[h].astype(BF), v[h]) for h in hs]

    for h in hs:
        b_last = b[h][chunk - 1:chunk, :]
        k_dec = (k[h] * jnp.exp(b_last - b[h])).astype(BF)
        st_ref[h] = st[h] * jnp.exp(b_last) + lax.dot_general(v[h], k_dec, _TN, preferred_element_type=F32)

    for h in hs:
        r = head_cols(r_ref, h, dv).astype(F32)
        o_ref[0, :, h * dv:(h + 1) * dv] = (_rms(o[h], gw_ref[...]) * _silu(r)).astype(o_ref.dtype)


def _gla_core(proj, la, gw, *, chunk, sub, heads):
    bsz, s, _ = proj.shape
    dk = la.shape[-1] // GLA_HEADS
    dv = gw.shape[-1]
    groups = GLA_HEADS // heads
    k_blocks = groups
    v_blocks = 2 * GLA_HEADS * dk // (heads * dv)
    r_blocks = v_blocks + groups
    expanded = (chunk // sub - 1) * dk
    return pl.pallas_call(
        functools.partial(_gla_kernel, chunk=chunk, sub=sub, heads=heads, scale=dk ** -0.5),
        grid=(bsz, groups, s // chunk),
        in_specs=[
            pl.BlockSpec((1, chunk, heads * dk), lambda b, h, n: (b, n, h)),
            pl.BlockSpec((1, chunk, heads * dk), lambda b, h, n: (b, n, k_blocks + h)),
            pl.BlockSpec((1, chunk, heads * dv), lambda b, h, n: (b, n, v_blocks + h)),
            pl.BlockSpec((1, chunk, heads * dv), lambda b, h, n: (b, n, r_blocks + h)),
            pl.BlockSpec((1, chunk, heads * dk), lambda b, h, n: (b, n, h)),
            pl.BlockSpec((1, dv), lambda b, h, n: (0, 0)),
        ],
        out_specs=pl.BlockSpec((1, chunk, heads * dv), lambda b, h, n: (b, n, h)),
        out_shape=jax.ShapeDtypeStruct((bsz, s, GLA_HEADS * dv), BF),
        scratch_shapes=[
            pltpu.VMEM((heads, dv, dk), F32),
            pltpu.VMEM((heads, chunk, expanded), BF),
            pltpu.VMEM((heads, chunk, expanded), BF),
        ],
        compiler_params=_params(("parallel", "parallel", "arbitrary"), 32),
        name="gla_core",
    )(proj, proj, proj, proj, la, gw)


def _proj_res_kernel(x_ref, w_ref, res_ref, o_ref):
    o_ref[...] = res_ref[...] + _dot(x_ref[...], w_ref[...])


def _proj_res(x, w, res, *, tm):
    t, kdim = x.shape
    n = w.shape[1]
    return pl.pallas_call(
        _proj_res_kernel,
        grid=(t // tm,),
        in_specs=[
            pl.BlockSpec((tm, kdim), lambda i: (i, 0)),
            pl.BlockSpec((kdim, n), lambda i: (0, 0)),
            pl.BlockSpec((tm, n), lambda i: (i, 0)),
        ],
        out_specs=pl.BlockSpec((tm, n), lambda i: (i, 0)),
        out_shape=jax.ShapeDtypeStruct((t, n), F32),
        compiler_params=_params(("parallel",), 48),
        name="proj_res",
    )(x, w, res)


def _ffn_kernel(h_ref, nw_ref, wg_ref, wu_ref, wd_ref, fw_ref, o_ref, xn_ref, *, final_norm):
    f = pl.program_id(1)

    @pl.when(f == 0)
    def _():
        h = h_ref[...]
        xn_ref[...] = _rms(h, nw_ref[...]).astype(BF)
        o_ref[...] = h

    xn = xn_ref[...]
    act = (_silu(_dot(xn, wg_ref[...])) * _dot(xn, wu_ref[...])).astype(BF)
    o_ref[...] += _dot(act, wd_ref[...])

    if final_norm:
        @pl.when(f == pl.num_programs(1) - 1)
        def _():
            o_ref[...] = _rms(o_ref[...], fw_ref[...])


def _ffn(h, nw, w_gate_up, w_down, layer, fw, *, tm, tf, final_norm):
    t, d = h.shape
    d_ff = w_down.shape[1]
    nf = d_ff // tf
    return pl.pallas_call(
        functools.partial(_ffn_kernel, final_norm=final_norm),
        grid=(t // tm, nf),
        in_specs=[
            pl.BlockSpec((tm, d), lambda i, f: (i, 0)),
            pl.BlockSpec((1, d), lambda i, f: (0, 0)),
            pl.BlockSpec((None, d, tf), lambda i, f: (layer, 0, f)),
            pl.BlockSpec((None, d, tf), lambda i, f: (layer, 0, nf + f)),
            pl.BlockSpec((None, tf, d), lambda i, f: (layer, f, 0)),
            pl.BlockSpec((1, d), lambda i, f: (0, 0)),
        ],
        out_specs=pl.BlockSpec((tm, d), lambda i, f: (i, 0)),
        out_shape=jax.ShapeDtypeStruct((t, d), F32),
        scratch_shapes=[pltpu.VMEM((tm, d), BF)],
        compiler_params=_params(("parallel", "arbitrary"), 56),
        name="ffn",
    )(h, nw, w_gate_up, w_gate_up, w_down, fw)


_EXP_UNDERFLOW = -88.0


def _sb_kernel(q_ref, k_ref, v_ref, o_ref, *, tq, group):
    s = q_ref.shape[1]
    tw = 2 * tq
    def later(n):
        r = lax.broadcasted_iota(jnp.int32, (n, n), 0)
        c = lax.broadcasted_iota(jnp.int32, (n, n), 1)
        return jnp.where(r > c, 1.0, 0.0).astype(BF)

    def causal(n_keys, offset):
        qr = lax.broadcasted_iota(jnp.int32, (tq, n_keys), 0)
        kc = lax.broadcasted_iota(jnp.int32, (tq, n_keys), 1)
        return kc < qr + offset

    later_w, later_b = later(tw), later(tq)
    diag_mask = causal(tq, 0)
    win_mask = causal(tw, tq)

    def scores(q, start, width, mask):
        kb = k_ref[0, pl.ds(start, width), :]
        z = lax.dot_general(q, kb, _NT, preferred_element_type=F32)
        lf = _log_sigmoid(-z)
        if mask is not None:
            lf = jnp.where(mask, lf, 0.0)
        return z, lf

    def suffix(lf, later, carry):
        lf_hi, lf_lo = _split2(lf)
        after = _dot(lf_hi, later) + _dot(lf_lo, later)
        if carry is not None:
            after = after + carry
        return after, after[:, :1] + lf[:, :1]

    def weighted(z, lf, after, start, width, mask):
        a = jnp.exp(z + lf + after)
        if mask is not None:
            a = jnp.where(mask, a, 0.0)
        return _dot(a.astype(BF), v_ref[0, pl.ds(start, width), :])

    def window(tile):
        if isinstance(tile, int):
            if tile == 0:
                return 0, tq, later_b, diag_mask
            return (tile - 1) * tq, tw, later_w, win_mask
        return pl.multiple_of((tile - 1) * tq, tq), tw, later_w, win_mask

    def row0(tile):
        return tile * tq if isinstance(tile, int) else pl.multiple_of(tile * tq, tq)

    def extend(tile, q, carry, top, acc):
        def cond(st):
            return jnp.logical_and(st[0] >= 0, st[1] > _EXP_UNDERFLOW)

        def body(st):
            j, _, carry, acc = st
            start = pl.multiple_of(j * tq, tq)
            z, lf = scores(q, start, tq, None)
            after, carry = suffix(lf, later_b, carry)
            acc = acc + weighted(z, lf, after, start, tq, None)
            return j - 1, jnp.max(carry), carry, acc

        init = (jnp.asarray(tile - 2, jnp.int32), top, carry, acc)
        return lax.while_loop(cond, body, init)[3]

    def run_group(base):
        tiles = [base + g for g in range(group)]
        wins = [window(t) for t in tiles]
        qs = [q_ref[0, pl.ds(row0(t), tq), :] for t in tiles]
        zl = [scores(q, w[0], w[1], w[3]) for q, w in zip(qs, wins)]
        ac = [suffix(lf, w[2], None) for (_, lf), w in zip(zl, wins)]
        accs = [weighted(z, lf, after, w[0], w[1], w[3]) for (z, lf), (after, _), w in zip(zl, ac, wins)]
        tops = [jnp.max(carry) for _, carry in ac]
        for t, q, (_, carry), top, acc in zip(tiles, qs, ac, tops, accs):
            acc = extend(t, q, carry, top, acc)
            o_ref[0, pl.ds(row0(t), tq), :] = acc.astype(o_ref.dtype)

    run_group(0)

    def loop_body(it, carry):
        run_group(it * group)
        return carry

    lax.fori_loop(1, s // (tq * group), loop_body, 0)


def _sb_attention(qkv, *, tq, group):
    bsz, s, w3 = qkv.shape
    w = w3 // 3
    hd = w // SB_HEADS
    return pl.pallas_call(
        functools.partial(_sb_kernel, tq=tq, group=group),
        grid=(bsz, SB_HEADS),
        in_specs=[
            pl.BlockSpec((1, s, hd), lambda b, h: (b, 0, h)),
            pl.BlockSpec((1, s, hd), lambda b, h: (b, 0, SB_HEADS + h)),
            pl.BlockSpec((1, s, hd), lambda b, h: (b, 0, 2 * SB_HEADS + h)),
        ],
        out_specs=pl.BlockSpec((1, s, hd), lambda b, h: (b, 0, h)),
        out_shape=jax.ShapeDtypeStruct((bsz, s, w), BF),
        compiler_params=_params(("parallel", "parallel"), 32),
        name="sb_attn",
    )(qkv, qkv, qkv)


def kernel(x, attn_norm_w, ffn_norm_w, gla_w_in, gla_w_gate_up, gla_b_gate, gla_gnorm_w, gla_w_out,
           kv_norm_w, sb_w_kv, sb_w_q, sb_w_out, ffn_w_gate_up, ffn_w_down, final_norm_w):
    bsz, s, d = x.shape
    t = bsz * s
    depth = attn_norm_w.shape[0]
    n_gla = gla_w_in.shape[0]
    rank = gla_w_gate_up.shape[1]
    main_w = gla_w_in.shape[2] - rank
    hd = d // SB_HEADS
    row = lambda v: v.reshape(1, -1)
    assert n_gla >= 1 and depth - n_gla == 1

    w_in_all = gla_w_in.astype(BF)
    w_gate_up_all = ffn_w_gate_up.astype(BF)
    w_down_all = ffn_w_down.astype(BF)

    h = x.reshape(t, d)
    for layer in range(depth):
        if layer < n_gla:
            wgl = jnp.pad(w_in_all[layer, :, main_w:], ((0, 0), (0, LANES - rank)))
            wup = jnp.pad(gla_w_gate_up[layer], ((0, LANES - rank), (0, 0)))
            proj, la = _gla_inproj(h, row(attn_norm_w[layer]), w_in_all, layer, main_w, wgl, wup,
                                   row(gla_b_gate[layer]), tm=1024, tn=768)
            o = _gla_core(proj.reshape(bsz, s, main_w), la.reshape(bsz, s, -1),
                          row(gla_gnorm_w[layer]), chunk=128, sub=16, heads=GLA_HEADS)
            h = _proj_res(o.reshape(t, -1), gla_w_out[layer].astype(BF), h, tm=512)
        else:
            j = layer - n_gla
            w_qkv = jnp.concatenate([sb_w_q[j], sb_w_kv], axis=1).astype(BF)
            qkv = _norm_matmul2(h, row(attn_norm_w[layer]), row(kv_norm_w), w_qkv, sb_w_q.shape[2],
                                tm=1024, tn=1024, scale_a=hd ** -0.5)
            o = _sb_attention(qkv.reshape(bsz, s, -1), tq=128, group=4)
            h = _proj_res(o.reshape(t, -1), sb_w_out[j].astype(BF), h, tm=512)
        h = _ffn(h, row(ffn_norm_w[layer]), w_gate_up_all, w_down_all, layer, row(final_norm_w),
                 tm=1024, tf=256, final_norm=layer == depth - 1)
    return h.reshape(bsz, s, d)
```

```python
import functools

import jax
import jax.numpy as jnp
from jax import lax
from jax.experimental import pallas as pl
from jax.experimental.pallas import tpu as pltpu

EPS = 1e-6
GLA_HEADS = 4
GLA_GATE_TAU = 16.0
SB_HEADS = 16
BF = jnp.bfloat16
F32 = jnp.float32
LANES = 128
MIB = 2 ** 20

_NT = (((1,), (1,)), ((), ()))
_TN = (((0,), (0,)), ((), ()))


def _params(sem, vmem_mib):
    return pltpu.CompilerParams(dimension_semantics=sem, vmem_limit_bytes=vmem_mib * MIB)


def _dot(a, b):
    return jnp.dot(a, b, preferred_element_type=F32)


def _rms(x, w):
    return x * lax.rsqrt(jnp.mean(x * x, axis=-1, keepdims=True) + EPS) * w


def _split2(x):
    hi = x.astype(BF)
    lo = (x - hi.astype(F32)).astype(BF)
    return hi, lo


def _log_sigmoid(x):
    return jnp.minimum(x, 0.0) - jnp.log(1.0 + jnp.exp(-jnp.abs(x)))


def _silu(x):
    return x * (1.0 / (1.0 + jnp.exp(-x)))


def _norm_matmul2_kernel(x_ref, nwa_ref, nwb_ref, w_ref, o_ref, xn_ref, *, na, scale_a):
    j = pl.program_id(1)

    @pl.when(j == 0)
    def _():
        x = x_ref[...]
        xhat = x * lax.rsqrt(jnp.mean(x * x, axis=-1, keepdims=True) + EPS)
        xn_ref[0] = (xhat * nwa_ref[...]).astype(BF)
        xn_ref[1] = (xhat * nwb_ref[...]).astype(BF)

    @pl.when(j < na)
    def _():
        o_ref[...] = (_dot(xn_ref[0], w_ref[...]) * scale_a).astype(o_ref.dtype)

    @pl.when(j >= na)
    def _():
        o_ref[...] = _dot(xn_ref[1], w_ref[...]).astype(o_ref.dtype)


def _norm_matmul2(x, nwa, nwb, w, n_a, *, tm, tn, scale_a):
    t, d = x.shape
    n = w.shape[1]
    return pl.pallas_call(
        functools.partial(_norm_matmul2_kernel, na=n_a // tn, scale_a=scale_a),
        grid=(t // tm, n // tn),
        in_specs=[
            pl.BlockSpec((tm, d), lambda i, j: (i, 0)),
            pl.BlockSpec((1, d), lambda i, j: (0, 0)),
            pl.BlockSpec((1, d), lambda i, j: (0, 0)),
            pl.BlockSpec((d, tn), lambda i, j: (0, j)),
        ],
        out_specs=pl.BlockSpec((tm, tn), lambda i, j: (i, j)),
        out_shape=jax.ShapeDtypeStruct((t, n), BF),
        scratch_shapes=[pltpu.VMEM((2, tm, d), BF)],
        compiler_params=_params(("parallel", "arbitrary"), 56),
        name="norm_matmul2",
    )(x, nwa, nwb, w)


def _gla_inproj_kernel(x_ref, nw_ref, w_ref, wgl_ref, wup_ref, bg_ref, o_ref, la_ref, xn_ref, gl_ref):
    @pl.when(pl.program_id(1) == 0)
    def _():
        xn = _rms(x_ref[...], nw_ref[...]).astype(BF)
        xn_ref[...] = xn
        gl_ref[0], gl_ref[1] = _split2(_dot(xn, wgl_ref[...]))

    up_hi, up_lo = _split2(wup_ref[...])
    g = _dot(gl_ref[0], up_hi) + _dot(gl_ref[1], up_hi) + _dot(gl_ref[0], up_lo) + bg_ref[...]
    la_ref[...] = _log_sigmoid(g) * (1.0 / GLA_GATE_TAU)
    o_ref[...] = _dot(xn_ref[...], w_ref[...]).astype(o_ref.dtype)


def _gla_inproj(x, nw, w_all, layer, n, wgl, wup, bg, *, tm, tn):
    t, d = x.shape
    steps = n // tn
    slab = wup.shape[1] // steps
    assert slab * steps == wup.shape[1] and slab % LANES == 0
    return pl.pallas_call(
        _gla_inproj_kernel,
        grid=(t // tm, steps),
        in_specs=[
            pl.BlockSpec((tm, d), lambda i, j: (i, 0)),
            pl.BlockSpec((1, d), lambda i, j: (0, 0)),
            pl.BlockSpec((None, d, tn), lambda i, j: (layer, 0, j)),
            pl.BlockSpec(wgl.shape, lambda i, j: (0, 0)),
            pl.BlockSpec((wup.shape[0], slab), lambda i, j: (0, j)),
            pl.BlockSpec((1, slab), lambda i, j: (0, j)),
        ],
        out_specs=[
            pl.BlockSpec((tm, tn), lambda i, j: (i, j)),
            pl.BlockSpec((tm, slab), lambda i, j: (i, j)),
        ],
        out_shape=[
            jax.ShapeDtypeStruct((t, n), BF),
            jax.ShapeDtypeStruct((t, wup.shape[1]), F32),
        ],
        scratch_shapes=[pltpu.VMEM((tm, d), BF), pltpu.VMEM((2, tm, wgl.shape[1]), BF)],
        compiler_params=_params(("parallel", "arbitrary"), 56),
        name="gla_inproj",
    )(x, nw, w_all, wgl, wup, bg)


def _gla_kernel(q_ref, k_ref, v_ref, r_ref, g_ref, gw_ref, o_ref, st_ref, qh_ref, kh_ref,
                *, chunk, sub, heads, scale):
    n_sub = chunk // sub
    dk = q_ref.shape[-1] // heads
    dv = v_ref.shape[-1] // heads
    hs = range(heads)

    @pl.when(pl.program_id(2) == 0)
    def _():
        st_ref[...] = jnp.zeros_like(st_ref)
        qh_ref[...] = jnp.zeros_like(qh_ref)
        kh_ref[...] = jnp.zeros_like(kh_ref)

    def head_cols(ref, h, width):
        return ref[0, :, h * width:(h + 1) * width]

    q = [head_cols(q_ref, h, dk).astype(F32) * scale for h in hs]
    k = [head_cols(k_ref, h, dk).astype(F32) for h in hs]
    g = [head_cols(g_ref, h, dk) for h in hs]

    row = lax.broadcasted_iota(jnp.int32, (chunk, chunk), 0)
    col = lax.broadcasted_iota(jnp.int32, (chunk, chunk), 1)
    tri = jnp.where(col <= row, 1.0, 0.0).astype(BF)
    diff = jnp.where(row // sub == col // sub, row - col, -1)

    b = []
    for h in hs:
        g_hi, g_lo = _split2(g[h])
        b.append(_dot(tri, g_hi) + _dot(tri, g_lo))

    st = [st_ref[h] for h in hs]
    o = [lax.dot_general((q[h] * jnp.exp(b[h])).astype(BF), st[h].astype(BF), _NT,
                         preferred_element_type=F32) for h in hs]

    scores = []
    for h in hs:
        k_run = None
        for i in range(1, n_sub):
            beta = b[h][i * sub - 1:i * sub, :]
            rows = slice(i * sub, (i + 1) * sub)
            prev = slice((i - 1) * sub, i * sub)
            cols = slice((i - 1) * dk, i * dk)
            qh_ref[h, rows, cols] = (q[h][rows] * jnp.exp(b[h][rows] - beta)).astype(BF)
            k_new = k[h][prev] * jnp.exp(beta - b[h][prev])
            if k_run is None:
                k_run = k_new
            else:
                k_run = jnp.concatenate([k_run * jnp.exp(beta - beta_prev), k_new], axis=0)
            kh_ref[h, 0:i * sub, cols] = k_run.astype(BF)
            beta_prev = beta
        scores.append(lax.dot_general(qh_ref[h], kh_ref[h], _NT, preferred_element_type=F32))

    for h in hs:
        decay = jnp.exp(g[h])
        w = k[h]
        for d in range(sub):
            if d > 0:
                w = decay * pltpu.roll(w, 1, 0)
            scores[h] = jnp.where(diff == d, jnp.sum(q[h] * w, axis=-1, keepdims=True), scores[h])

    v = [head_cols(v_ref, h, dv) for h in hs]
    o = [o[h] + _dot(scores[h].astype(BF), v[h]) for h in hs]

    for h in hs:
        b_last = b[h][chunk - 1:chunk, :]
        k_dec = (k[h] * jnp.exp(b_last - b[h])).astype(BF)
        st_ref[h] = st[h] * jnp.exp(b_last) + lax.dot_general(v[h], k_dec, _TN, preferred_element_type=F32)

    for h in hs:
        r = head_cols(r_ref, h, dv).astype(F32)
        o_ref[0, :, h * dv:(h + 1) * dv] = (_rms(o[h], gw_ref[...]) * _silu(r)).astype(o_ref.dtype)


def _gla_core(proj, la, gw, *, chunk, sub, heads):
    bsz, s, _ = proj.shape
    dk = la.shape[-1] // GLA_HEADS
    dv = gw.shape[-1]
    groups = GLA_HEADS // heads
    k_blocks = groups
    v_blocks = 2 * GLA_HEADS * dk // (heads * dv)
    r_blocks = v_blocks + groups
    expanded = (chunk // sub - 1) * dk
    return pl.pallas_call(
        functools.partial(_gla_kernel, chunk=chunk, sub=sub, heads=heads, scale=dk ** -0.5),
        grid=(bsz, groups, s // chunk),
        in_specs=[
            pl.BlockSpec((1, chunk, heads * dk), lambda b, h, n: (b, n, h)),
            pl.BlockSpec((1, chunk, heads * dk), lambda b, h, n: (b, n, k_blocks + h)),
            pl.BlockSpec((1, chunk, heads * dv), lambda b, h, n: (b, n, v_blocks + h)),
            pl.BlockSpec((1, chunk, heads * dv), lambda b, h, n: (b, n, r_blocks + h)),
            pl.BlockSpec((1, chunk, heads * dk), lambda b, h, n: (b, n, h)),
            pl.BlockSpec((1, dv), lambda b, h, n: (0, 0)),
        ],
        out_specs=pl.BlockSpec((1, chunk, heads * dv), lambda b, h, n: (b, n, h)),
        out_shape=jax.ShapeDtypeStruct((bsz, s, GLA_HEADS * dv), BF),
        scratch_shapes=[
            pltpu.VMEM((heads, dv, dk), F32),
            pltpu.VMEM((heads, chunk, expanded), BF),
            pltpu.VMEM((heads, chunk, expanded), BF),
        ],
        compiler_params=_params(("parallel", "parallel", "arbitrary"), 32),
        name="gla_core",
    )(proj, proj, proj, proj, la, gw)


def _proj_res_kernel(x_ref, w_ref, res_ref, o_ref):
    o_ref[...] = res_ref[...] + _dot(x_ref[...], w_ref[...])


def _proj_res(x, w, res, *, tm):
    t, kdim = x.shape
    n = w.shape[1]
    return pl.pallas_call(
        _proj_res_kernel,
        grid=(t // tm,),
        in_specs=[
            pl.BlockSpec((tm, kdim), lambda i: (i, 0)),
            pl.BlockSpec((kdim, n), lambda i: (0, 0)),
            pl.BlockSpec((tm, n), lambda i: (i, 0)),
        ],
        out_specs=pl.BlockSpec((tm, n), lambda i: (i, 0)),
        out_shape=jax.ShapeDtypeStruct((t, n), F32),
        compiler_params=_params(("parallel",), 48),
        name="proj_res",
    )(x, w, res)


def _ffn_kernel(h_ref, nw_ref, wg_ref, wu_ref, wd_ref, fw_ref, o_ref, xn_ref, *, final_norm):
    f = pl.program_id(1)

    @pl.when(f == 0)
    def _():
        h = h_ref[...]
        xn_ref[...] = _rms(h, nw_ref[...]).astype(BF)
        o_ref[...] = h

    xn = xn_ref[...]
    act = (_silu(_dot(xn, wg_ref[...])) * _dot(xn, wu_ref[...])).astype(BF)
    o_ref[...] += _dot(act, wd_ref[...])

    if final_norm:
        @pl.when(f == pl.num_programs(1) - 1)
        def _():
            o_ref[...] = _rms(o_ref[...], fw_ref[...])


def _ffn(h, nw, w_gate_up, w_down, layer, fw, *, tm, tf, final_norm):
    t, d = h.shape
    d_ff = w_down.shape[1]
    nf = d_ff // tf
    return pl.pallas_call(
        functools.partial(_ffn_kernel, final_norm=final_norm),
        grid=(t // tm, nf),
        in_specs=[
            pl.BlockSpec((tm, d), lambda i, f: (i, 0)),
            pl.BlockSpec((1, d), lambda i, f: (0, 0)),
            pl.BlockSpec((None, d, tf), lambda i, f: (layer, 0, f)),
            pl.BlockSpec((None, d, tf), lambda i, f: (layer, 0, nf + f)),
            pl.BlockSpec((None, tf, d), lambda i, f: (layer, f, 0)),
            pl.BlockSpec((1, d), lambda i, f: (0, 0)),
        ],
        out_specs=pl.BlockSpec((tm, d), lambda i, f: (i, 0)),
        out_shape=jax.ShapeDtypeStruct((t, d), F32),
        scratch_shapes=[pltpu.VMEM((tm, d), BF)],
        compiler_params=_params(("parallel", "arbitrary"), 60),
        name="ffn",
    )(h, nw, w_gate_up, w_gate_up, w_down, fw)


_EXP2_UNDERFLOW = -127.0
LOG2_E = 1.4426950408889634


def _sb_kernel(q_ref, k_ref, v_ref, o_ref, *, tq, group):
    s = q_ref.shape[1]
    tw = 2 * tq
    def later(n):
        r = lax.broadcasted_iota(jnp.int32, (n, n), 0)
        c = lax.broadcasted_iota(jnp.int32, (n, n), 1)
        return jnp.where(r > c, 1.0, 0.0).astype(BF)

    def causal(n_keys, offset):
        qr = lax.broadcasted_iota(jnp.int32, (tq, n_keys), 0)
        kc = lax.broadcasted_iota(jnp.int32, (tq, n_keys), 1)
        return kc < qr + offset

    later_w, later_b = later(tw), later(tq)
    diag_mask = causal(tq, 0)
    win_mask = causal(tw, tq)

    def scores(q, start, width, mask):
        kb = k_ref[0, pl.ds(start, width), :]
        z = lax.dot_general(q, kb, _NT, preferred_element_type=F32)
        nz = -z
        lf = jnp.minimum(nz, 0.0) - jnp.log2(1.0 + jnp.exp2(jnp.minimum(z, nz)))
        if mask is not None:
            lf = jnp.where(mask, lf, 0.0)
        return z, lf

    def suffix(lf, later, carry):
        after = _dot(lf.astype(BF), later)
        if carry is not None:
            after = after + carry
        return after, after[:, :1] + lf[:, :1]

    def weighted(z, lf, after, start, width, mask):
        a = jnp.exp2(z + lf + after)
        if mask is not None:
            a = jnp.where(mask, a, 0.0)
        return _dot(a.astype(BF), v_ref[0, pl.ds(start, width), :])

    def window(tile):
        if isinstance(tile, int):
            if tile == 0:
                return 0, tq, later_b, diag_mask
            return (tile - 1) * tq, tw, later_w, win_mask
        return pl.multiple_of((tile - 1) * tq, tq), tw, later_w, win_mask

    def row0(tile):
        return tile * tq if isinstance(tile, int) else pl.multiple_of(tile * tq, tq)

    def extend(tile, q, carry, top, acc):
        def cond(st):
            return jnp.logical_and(st[0] >= 0, st[1] > _EXP2_UNDERFLOW)

        def body(st):
            j, _, carry, acc = st
            start = pl.multiple_of(j * tq, tq)
            z, lf = scores(q, start, tq, None)
            after, carry = suffix(lf, later_b, carry)
            acc = acc + weighted(z, lf, after, start, tq, None)
            return j - 1, jnp.max(carry), carry, acc

        init = (jnp.asarray(tile - 2, jnp.int32), top, carry, acc)
        return lax.while_loop(cond, body, init)[3]

    def run_group(base):
        tiles = [base + g for g in range(group)]
        wins = [window(t) for t in tiles]
        qs = [q_ref[0, pl.ds(row0(t), tq), :] for t in tiles]
        zl = [scores(q, w[0], w[1], w[3]) for q, w in zip(qs, wins)]
        ac = [suffix(lf, w[2], None) for (_, lf), w in zip(zl, wins)]
        accs = [weighted(z, lf, after, w[0], w[1], w[3]) for (z, lf), (after, _), w in zip(zl, ac, wins)]
        tops = [jnp.max(carry) for _, carry in ac]
        for t, q, (_, carry), top, acc in zip(tiles, qs, ac, tops, accs):
            acc = extend(t, q, carry, top, acc)
            o_ref[0, pl.ds(row0(t), tq), :] = acc.astype(o_ref.dtype)

    run_group(0)

    def loop_body(it, carry):
        run_group(it * group)
        return carry

    lax.fori_loop(1, s // (tq * group), loop_body, 0)


def _sb_attention(qkv, *, tq, group):
    bsz, s, w3 = qkv.shape
    w = w3 // 3
    hd = w // SB_HEADS
    return pl.pallas_call(
        functools.partial(_sb_kernel, tq=tq, group=group),
        grid=(bsz, SB_HEADS),
        in_specs=[
            pl.BlockSpec((1, s, hd), lambda b, h: (b, 0, h)),
            pl.BlockSpec((1, s, hd), lambda b, h: (b, 0, SB_HEADS + h)),
            pl.BlockSpec((1, s, hd), lambda b, h: (b, 0, 2 * SB_HEADS + h)),
        ],
        out_specs=pl.BlockSpec((1, s, hd), lambda b, h: (b, 0, h)),
        out_shape=jax.ShapeDtypeStruct((bsz, s, w), BF),
        compiler_params=_params(("parallel", "parallel"), 32),
        name="sb_attn",
    )(qkv, qkv, qkv)


def kernel(x, attn_norm_w, ffn_norm_w, gla_w_in, gla_w_gate_up, gla_b_gate, gla_gnorm_w, gla_w_out,
           kv_norm_w, sb_w_kv, sb_w_q, sb_w_out, ffn_w_gate_up, ffn_w_down, final_norm_w):
    bsz, s, d = x.shape
    t = bsz * s
    depth = attn_norm_w.shape[0]
    n_gla = gla_w_in.shape[0]
    rank = gla_w_gate_up.shape[1]
    main_w = gla_w_in.shape[2] - rank
    hd = d // SB_HEADS
    row = lambda v: v.reshape(1, -1)
    assert n_gla >= 1 and depth - n_gla == 1

    w_in_all = gla_w_in.astype(BF)
    w_gate_up_all = ffn_w_gate_up.astype(BF)
    w_down_all = ffn_w_down.astype(BF)

    h = x.reshape(t, d)
    for layer in range(depth):
        if layer < n_gla:
            wgl = jnp.pad(w_in_all[layer, :, main_w:], ((0, 0), (0, LANES - rank)))
            wup = jnp.pad(gla_w_gate_up[layer], ((0, LANES - rank), (0, 0)))
            proj, la = _gla_inproj(h, row(attn_norm_w[layer]), w_in_all, layer, main_w, wgl, wup,
                                   row(gla_b_gate[layer]), tm=1024, tn=768)
            o = _gla_core(proj.reshape(bsz, s, main_w), la.reshape(bsz, s, -1),
                          row(gla_gnorm_w[layer]), chunk=128, sub=16, heads=GLA_HEADS)
            h = _proj_res(o.reshape(t, -1), gla_w_out[layer].astype(BF), h, tm=512)
        else:
            j = layer - n_gla
            w_qkv = jnp.concatenate([sb_w_q[j], sb_w_kv], axis=1).astype(BF)
            qkv = _norm_matmul2(h, row(attn_norm_w[layer]), row(kv_norm_w), w_qkv, sb_w_q.shape[2],
                                tm=1024, tn=1024, scale_a=hd ** -0.5 * LOG2_E)
            o = _sb_attention(qkv.reshape(bsz, s, -1), tq=128, group=8)
            h = _proj_res(o.reshape(t, -1), sb_w_out[j].astype(BF), h, tm=512)
        h = _ffn(h, row(ffn_norm_w[layer]), w_gate_up_all, w_down_all, layer, row(final_norm_w),
                 tm=1024, tf=512, final_norm=layer == depth - 1)
    return h.reshape(bsz, s, d)
```

```python
import functools

import jax
import jax.numpy as jnp
from jax import lax
from jax.experimental import pallas as pl
from jax.experimental.pallas import tpu as pltpu

EPS = 1e-6
GLA_HEADS = 4
GLA_GATE_TAU = 16.0
SB_HEADS = 16
BF = jnp.bfloat16
F32 = jnp.float32
LANES = 128
MIB = 2 ** 20

_NT = (((1,), (1,)), ((), ()))
_TN = (((0,), (0,)), ((), ()))


def _params(sem, vmem_mib):
    return pltpu.CompilerParams(dimension_semantics=sem, vmem_limit_bytes=vmem_mib * MIB)


def _dot(a, b):
    return jnp.dot(a, b, preferred_element_type=F32)


def _rms(x, w):
    return x * lax.rsqrt(jnp.mean(x * x, axis=-1, keepdims=True) + EPS) * w


def _split2(x):
    hi = x.astype(BF)
    lo = (x - hi.astype(F32)).astype(BF)
    return hi, lo


def _log_sigmoid(x):
    return jnp.minimum(x, 0.0) - jnp.log(1.0 + jnp.exp(-jnp.abs(x)))


def _silu(x):
    return x * (1.0 / (1.0 + jnp.exp(-x)))


BF16_SUBLANES = 16


def _cast_specs(jobs, grid):
    steps = 1
    for g in grid:
        steps *= g

    def linear(idx):
        s = 0
        for g, i in zip(grid, idx):
            s = s * g + i
        return s

    operands, in_specs, out_specs, out_shapes, parts = [], [], [], [], []
    for job in jobs:
        n_rows = job[0][0].shape[-2]
        n_blk = max(n for n in range(1, steps + 1)
                    if n_rows % n == 0 and (n_rows // n) % BF16_SUBLANES == 0)
        rows = n_rows // n_blk
        blk = lambda *idx, n_blk=n_blk: linear(idx) * n_blk // steps
        for arr, lead in job:
            assert arr.shape[-2] == n_rows
            operands.append(arr)
            if lead is None:
                in_specs.append(pl.BlockSpec((rows, arr.shape[-1]), lambda *idx, blk=blk: (blk(*idx), 0)))
            else:
                in_specs.append(pl.BlockSpec((None, rows, arr.shape[-1]),
                                             lambda *idx, blk=blk, lead=lead: (lead, blk(*idx), 0)))
        width = sum(arr.shape[-1] for arr, _ in job)
        out_specs.append(pl.BlockSpec((rows, width), lambda *idx, blk=blk: (blk(*idx), 0)))
        out_shapes.append(jax.ShapeDtypeStruct((n_rows, width), BF))
        parts.append(len(job))
    return operands, in_specs, out_specs, out_shapes, parts


def _with_casts(body, n_in, n_out, parts):
    n_src = sum(parts)

    def kernel(*refs):
        ins, rest = refs[:n_in], refs[n_in:]
        srcs, rest = rest[:n_src], rest[n_src:]
        outs, rest = rest[:n_out], rest[n_out:]
        dsts, scratch = rest[:len(parts)], rest[len(parts):]
        body(*ins, *outs, *scratch)
        srcs = list(srcs)
        for dst, n_parts in zip(dsts, parts):
            col = 0
            for _ in range(n_parts):
                src = srcs.pop(0)
                dst[:, col:col + src.shape[-1]] = src[...].astype(BF)
                col += src.shape[-1]

    return kernel


def _norm_matmul2_kernel(x_ref, nwa_ref, nwb_ref, w_ref, o_ref, xn_ref, *, na, scale_a):
    j = pl.program_id(1)

    @pl.when(j == 0)
    def _():
        x = x_ref[...]
        xhat = x * lax.rsqrt(jnp.mean(x * x, axis=-1, keepdims=True) + EPS)
        xn_ref[0] = (xhat * nwa_ref[...]).astype(BF)
        xn_ref[1] = (xhat * nwb_ref[...]).astype(BF)

    @pl.when(j < na)
    def _():
        o_ref[...] = (_dot(xn_ref[0], w_ref[...]) * scale_a).astype(o_ref.dtype)

    @pl.when(j >= na)
    def _():
        o_ref[...] = _dot(xn_ref[1], w_ref[...]).astype(o_ref.dtype)


def _norm_matmul2(x, nwa, nwb, w, n_a, *, tm, tn, scale_a):
    t, d = x.shape
    n = w.shape[1]
    return pl.pallas_call(
        functools.partial(_norm_matmul2_kernel, na=n_a // tn, scale_a=scale_a),
        grid=(t // tm, n // tn),
        in_specs=[
            pl.BlockSpec((tm, d), lambda i, j: (i, 0)),
            pl.BlockSpec((1, d), lambda i, j: (0, 0)),
            pl.BlockSpec((1, d), lambda i, j: (0, 0)),
            pl.BlockSpec((d, tn), lambda i, j: (0, j)),
        ],
        out_specs=pl.BlockSpec((tm, tn), lambda i, j: (i, j)),
        out_shape=jax.ShapeDtypeStruct((t, n), BF),
        scratch_shapes=[pltpu.VMEM((2, tm, d), BF)],
        compiler_params=_params(("parallel", "arbitrary"), 56),
        name="norm_matmul2",
    )(x, nwa, nwb, w)


def _gla_inproj_kernel(x_ref, nw_ref, w_ref, wgl_ref, wup_ref, bg_ref, o_ref, la_ref, xn_ref, gl_ref):
    @pl.when(pl.program_id(1) == 0)
    def _():
        xn = _rms(x_ref[...], nw_ref[...]).astype(BF)
        xn_ref[...] = xn
        gl_ref[0], gl_ref[1] = _split2(_dot(xn, wgl_ref[...]))

    up_hi, up_lo = _split2(wup_ref[...])
    g = _dot(gl_ref[0], up_hi) + _dot(gl_ref[1], up_hi) + _dot(gl_ref[0], up_lo) + bg_ref[...]
    la_ref[...] = _log_sigmoid(g) * (1.0 / GLA_GATE_TAU)
    o_ref[...] = _dot(xn_ref[...], w_ref[...]).astype(o_ref.dtype)


def _gla_inproj(x, nw, w_all, layer, n, wgl, wup, bg, *, tm, tn):
    t, d = x.shape
    steps = n // tn
    slab = wup.shape[1] // steps
    assert slab * steps == wup.shape[1] and slab % LANES == 0
    return pl.pallas_call(
        _gla_inproj_kernel,
        grid=(t // tm, steps),
        in_specs=[
            pl.BlockSpec((tm, d), lambda i, j: (i, 0)),
            pl.BlockSpec((1, d), lambda i, j: (0, 0)),
            pl.BlockSpec((None, d, tn), lambda i, j: (layer, 0, j)),
            pl.BlockSpec(wgl.shape, lambda i, j: (0, 0)),
            pl.BlockSpec((wup.shape[0], slab), lambda i, j: (0, j)),
            pl.BlockSpec((1, slab), lambda i, j: (0, j)),
        ],
        out_specs=[
            pl.BlockSpec((tm, tn), lambda i, j: (i, j)),
            pl.BlockSpec((tm, slab), lambda i, j: (i, j)),
        ],
        out_shape=[
            jax.ShapeDtypeStruct((t, n), BF),
            jax.ShapeDtypeStruct((t, wup.shape[1]), F32),
        ],
        scratch_shapes=[pltpu.VMEM((tm, d), BF), pltpu.VMEM((2, tm, wgl.shape[1]), BF)],
        compiler_params=_params(("parallel", "arbitrary"), 56),
        name="gla_inproj",
    )(x, nw, w_all, wgl, wup, bg)


def _gla_kernel(q_ref, k_ref, v_ref, r_ref, g_ref, gw_ref, o_ref, st_ref, qh_ref, kh_ref,
                *, chunk, sub, heads, scale):
    n_sub = chunk // sub
    dk = q_ref.shape[-1] // heads
    dv = v_ref.shape[-1] // heads
    hs = range(heads)

    @pl.when(pl.program_id(2) == 0)
    def _():
        st_ref[...] = jnp.zeros_like(st_ref)
        qh_ref[...] = jnp.zeros_like(qh_ref)
        kh_ref[...] = jnp.zeros_like(kh_ref)

    def head_cols(ref, h, width):
        return ref[0, :, h * width:(h + 1) * width]

    q = [head_cols(q_ref, h, dk).astype(F32) * scale for h in hs]
    k = [head_cols(k_ref, h, dk).astype(F32) for h in hs]
    g = [head_cols(g_ref, h, dk) for h in hs]

    row = lax.broadcasted_iota(jnp.int32, (chunk, chunk), 0)
    col = lax.broadcasted_iota(jnp.int32, (chunk, chunk), 1)
    tri = jnp.where(col <= row, 1.0, 0.0).astype(BF)
    diff = jnp.where(row // sub == col // sub, row - col, -1)

    b = []
    for h in hs:
        g_hi, g_lo = _split2(g[h])
        b.append(_dot(tri, g_hi) + _dot(tri, g_lo))

    st = [st_ref[h] for h in hs]
    o = [lax.dot_general((q[h] * jnp.exp(b[h])).astype(BF), st[h].astype(BF), _NT,
                         preferred_element_type=F32) for h in hs]

    scores = []
    for h in hs:
        k_run = None
        for i in range(1, n_sub):
            beta = b[h][i * sub - 1:i * sub, :]
            rows = slice(i * sub, (i + 1) * sub)
            prev = slice((i - 1) * sub, i * sub)
            cols = slice((i - 1) * dk, i * dk)
            qh_ref[h, rows, cols] = (q[h][rows] * jnp.exp(b[h][rows] - beta)).astype(BF)
            k_new = k[h][prev] * jnp.exp(beta - b[h][prev])
            if k_run is None:
                k_run = k_new
            else:
                k_run = jnp.concatenate([k_run * jnp.exp(beta - beta_prev), k_new], axis=0)
            kh_ref[h, 0:i * sub, cols] = k_run.astype(BF)
            beta_prev = beta
        scores.append(lax.dot_general(qh_ref[h], kh_ref[h], _NT, preferred_element_type=F32))

    for h in hs:
        decay = jnp.exp(g[h])
        w = k[h]
        for d in range(sub):
            if d > 0:
                w = decay * pltpu.roll(w, 1, 0)
            scores[h] = jnp.where(diff == d, jnp.sum(q[h] * w, axis=-1, keepdims=True), scores[h])

    v = [head_cols(v_ref, h, dv) for h in hs]
    o = [o[h] + _dot(scores[h].astype(BF), v[h]) for h in hs]

    for h in hs:
        b_last = b[h][chunk - 1:chunk, :]
        k_dec = (k[h] * jnp.exp(b_last - b[h])).astype(BF)
        st_ref[h] = st[h] * jnp.exp(b_last) + lax.dot_general(v[h], k_dec, _TN, preferred_element_type=F32)

    for h in hs:
        r = head_cols(r_ref, h, dv).astype(F32)
        o_ref[0, :, h * dv:(h + 1) * dv] = (_rms(o[h], gw_ref[...]) * _silu(r)).astype(o_ref.dtype)


def _gla_core(proj, la, gw, cast_jobs, *, chunk, sub, heads):
    bsz, s, _ = proj.shape
    dk = la.shape[-1] // GLA_HEADS
    dv = gw.shape[-1]
    groups = GLA_HEADS // heads
    k_blocks = groups
    v_blocks = 2 * GLA_HEADS * dk // (heads * dv)
    r_blocks = v_blocks + groups
    expanded = (chunk // sub - 1) * dk
    grid = (bsz, groups, s // chunk)
    c_ops, c_in, c_out, c_shapes, c_parts = _cast_specs(cast_jobs, grid)
    body = functools.partial(_gla_kernel, chunk=chunk, sub=sub, heads=heads, scale=dk ** -0.5)
    outs = pl.pallas_call(
        _with_casts(body, 6, 1, c_parts),
        grid=grid,
        in_specs=[
            pl.BlockSpec((1, chunk, heads * dk), lambda b, h, n: (b, n, h)),
            pl.BlockSpec((1, chunk, heads * dk), lambda b, h, n: (b, n, k_blocks + h)),
            pl.BlockSpec((1, chunk, heads * dv), lambda b, h, n: (b, n, v_blocks + h)),
            pl.BlockSpec((1, chunk, heads * dv), lambda b, h, n: (b, n, r_blocks + h)),
            pl.BlockSpec((1, chunk, heads * dk), lambda b, h, n: (b, n, h)),
            pl.BlockSpec((1, dv), lambda b, h, n: (0, 0)),
        ] + c_in,
        out_specs=[pl.BlockSpec((1, chunk, heads * dv), lambda b, h, n: (b, n, h))] + c_out,
        out_shape=[jax.ShapeDtypeStruct((bsz, s, GLA_HEADS * dv), BF)] + c_shapes,
        scratch_shapes=[
            pltpu.VMEM((heads, dv, dk), F32),
            pltpu.VMEM((heads, chunk, expanded), BF),
            pltpu.VMEM((heads, chunk, expanded), BF),
        ],
        compiler_params=_params(("arbitrary", "arbitrary", "arbitrary"), 48),
        name="gla_core",
    )(proj, proj, proj, proj, la, gw, *c_ops)
    return outs[0], outs[1:]


def _proj_res_kernel(x_ref, w_ref, res_ref, o_ref):
    o_ref[...] = res_ref[...] + _dot(x_ref[...], w_ref[...])


def _proj_res(x, w, res, *, tm):
    t, kdim = x.shape
    n = w.shape[1]
    return pl.pallas_call(
        _proj_res_kernel,
        grid=(t // tm,),
        in_specs=[
            pl.BlockSpec((tm, kdim), lambda i: (i, 0)),
            pl.BlockSpec((kdim, n), lambda i: (0, 0)),
            pl.BlockSpec((tm, n), lambda i: (i, 0)),
        ],
        out_specs=pl.BlockSpec((tm, n), lambda i: (i, 0)),
        out_shape=jax.ShapeDtypeStruct((t, n), F32),
        compiler_params=_params(("parallel",), 48),
        name="proj_res",
    )(x, w, res)


def _ffn_kernel(h_ref, nw_ref, wg_ref, wu_ref, wd_ref, fw_ref, o_ref, xn_ref, *, final_norm):
    f = pl.program_id(1)

    @pl.when(f == 0)
    def _():
        h = h_ref[...]
        xn_ref[...] = _rms(h, nw_ref[...]).astype(BF)
        o_ref[...] = h

    xn = xn_ref[...]
    act = (_silu(_dot(xn, wg_ref[...])) * _dot(xn, wu_ref[...])).astype(BF)
    o_ref[...] += _dot(act, wd_ref[...])

    if final_norm:
        @pl.when(f == pl.num_programs(1) - 1)
        def _():
            o_ref[...] = _rms(o_ref[...], fw_ref[...])


def _ffn(h, nw, w_gate_up, w_down, fw, *, tm, tf, final_norm):
    t, d = h.shape
    d_ff = w_down.shape[0]
    nf = d_ff // tf
    return pl.pallas_call(
        functools.partial(_ffn_kernel, final_norm=final_norm),
        grid=(t // tm, nf),
        in_specs=[
            pl.BlockSpec((tm, d), lambda i, f: (i, 0)),
            pl.BlockSpec((1, d), lambda i, f: (0, 0)),
            pl.BlockSpec((d, tf), lambda i, f: (0, f)),
            pl.BlockSpec((d, tf), lambda i, f: (0, nf + f)),
            pl.BlockSpec((tf, d), lambda i, f: (f, 0)),
            pl.BlockSpec((1, d), lambda i, f: (0, 0)),
        ],
        out_specs=pl.BlockSpec((tm, d), lambda i, f: (i, 0)),
        out_shape=jax.ShapeDtypeStruct((t, d), F32),
        scratch_shapes=[pltpu.VMEM((tm, d), BF)],
        compiler_params=_params(("parallel", "arbitrary"), 60),
        name="ffn",
    )(h, nw, w_gate_up, w_gate_up, w_down, fw)


_EXP2_UNDERFLOW = -127.0
LOG2_E = 1.4426950408889634


def _sb_kernel(q_ref, k_ref, v_ref, o_ref, *, tq, group):
    s = q_ref.shape[1]
    tw = 2 * tq
    def later(n):
        r = lax.broadcasted_iota(jnp.int32, (n, n), 0)
        c = lax.broadcasted_iota(jnp.int32, (n, n), 1)
        return jnp.where(r > c, 1.0, 0.0).astype(BF)

    def causal(n_keys, offset):
        qr = lax.broadcasted_iota(jnp.int32, (tq, n_keys), 0)
        kc = lax.broadcasted_iota(jnp.int32, (tq, n_keys), 1)
        return kc < qr + offset

    later_w, later_b = later(tw), later(tq)
    diag_mask = causal(tq, 0)
    win_mask = causal(tw, tq)

    def scores(q, start, width, mask):
        kb = k_ref[0, pl.ds(start, width), :]
        z = lax.dot_general(q, kb, _NT, preferred_element_type=F32)
        nz = -z
        lf = jnp.minimum(nz, 0.0) - jnp.log2(1.0 + jnp.exp2(jnp.minimum(z, nz)))
        if mask is not None:
            lf = jnp.where(mask, lf, 0.0)
        return z, lf

    def suffix(lf, later, carry):
        after = _dot(lf.astype(BF), later)
        if carry is not None:
            after = after + carry
        return after, after[:, :1] + lf[:, :1]

    def weighted(z, lf, after, start, width, mask):
        a = jnp.exp2(z + lf + after)
        if mask is not None:
            a = jnp.where(mask, a, 0.0)
        return _dot(a.astype(BF), v_ref[0, pl.ds(start, width), :])

    def window(tile):
        if isinstance(tile, int):
            if tile == 0:
                return 0, tq, later_b, diag_mask
            return (tile - 1) * tq, tw, later_w, win_mask
        return pl.multiple_of((tile - 1) * tq, tq), tw, later_w, win_mask

    def row0(tile):
        return tile * tq if isinstance(tile, int) else pl.multiple_of(tile * tq, tq)

    def extend(tile, q, carry, top, acc):
        def cond(st):
            return jnp.logical_and(st[0] >= 0, st[1] > _EXP2_UNDERFLOW)

        def body(st):
            j, _, carry, acc = st
            start = pl.multiple_of(j * tq, tq)
            z, lf = scores(q, start, tq, None)
            after, carry = suffix(lf, later_b, carry)
            acc = acc + weighted(z, lf, after, start, tq, None)
            return j - 1, jnp.max(carry), carry, acc

        init = (jnp.asarray(tile - 2, jnp.int32), top, carry, acc)
        return lax.while_loop(cond, body, init)[3]

    def run_group(base):
        tiles = [base + g for g in range(group)]
        wins = [window(t) for t in tiles]
        qs = [q_ref[0, pl.ds(row0(t), tq), :] for t in tiles]
        zl = [scores(q, w[0], w[1], w[3]) for q, w in zip(qs, wins)]
        ac = [suffix(lf, w[2], None) for (_, lf), w in zip(zl, wins)]
        accs = [weighted(z, lf, after, w[0], w[1], w[3]) for (z, lf), (after, _), w in zip(zl, ac, wins)]
        tops = [jnp.max(carry) for _, carry in ac]
        for t, q, (_, carry), top, acc in zip(tiles, qs, ac, tops, accs):
            acc = extend(t, q, carry, top, acc)
            o_ref[0, pl.ds(row0(t), tq), :] = acc.astype(o_ref.dtype)

    run_group(0)

    def loop_body(it, carry):
        run_group(it * group)
        return carry

    lax.fori_loop(1, s // (tq * group), loop_body, 0)


def _sb_attention(qkv, cast_jobs, *, tq, group):
    bsz, s, w3 = qkv.shape
    w = w3 // 3
    hd = w // SB_HEADS
    grid = (bsz, SB_HEADS)
    c_ops, c_in, c_out, c_shapes, c_parts = _cast_specs(cast_jobs, grid)
    outs = pl.pallas_call(
        _with_casts(functools.partial(_sb_kernel, tq=tq, group=group), 3, 1, c_parts),
        grid=grid,
        in_specs=[
            pl.BlockSpec((1, s, hd), lambda b, h: (b, 0, h)),
            pl.BlockSpec((1, s, hd), lambda b, h: (b, 0, SB_HEADS + h)),
            pl.BlockSpec((1, s, hd), lambda b, h: (b, 0, 2 * SB_HEADS + h)),
        ] + c_in,
        out_specs=[pl.BlockSpec((1, s, hd), lambda b, h: (b, 0, h))] + c_out,
        out_shape=[jax.ShapeDtypeStruct((bsz, s, w), BF)] + c_shapes,
        compiler_params=_params(("arbitrary", "arbitrary"), 48),
        name="sb_attn",
    )(qkv, qkv, qkv, *c_ops)
    return outs[0], outs[1:]


def kernel(x, attn_norm_w, ffn_norm_w, gla_w_in, gla_w_gate_up, gla_b_gate, gla_gnorm_w, gla_w_out,
           kv_norm_w, sb_w_kv, sb_w_q, sb_w_out, ffn_w_gate_up, ffn_w_down, final_norm_w):
    bsz, s, d = x.shape
    t = bsz * s
    depth = attn_norm_w.shape[0]
    n_gla = gla_w_in.shape[0]
    rank = gla_w_gate_up.shape[1]
    main_w = gla_w_in.shape[2] - rank
    hd = d // SB_HEADS
    row = lambda v: v.reshape(1, -1)
    assert n_gla >= 1 and depth - n_gla == 1

    w_in_all = gla_w_in.astype(BF)

    h = x.reshape(t, d)
    for layer in range(depth):
        casts = [[(ffn_w_gate_up, layer)], [(ffn_w_down, layer)]]
        if layer < n_gla:
            wgl = jnp.pad(w_in_all[layer, :, main_w:], ((0, 0), (0, LANES - rank)))
            wup = jnp.pad(gla_w_gate_up[layer], ((0, LANES - rank), (0, 0)))
            proj, la = _gla_inproj(h, row(attn_norm_w[layer]), w_in_all, layer, main_w, wgl, wup,
                                   row(gla_b_gate[layer]), tm=1024, tn=768)
            casts.append([(gla_w_out, layer)])
            if layer == n_gla - 1:
                casts.append([(sb_w_q, 0), (sb_w_kv, None)])
            o, cast = _gla_core(proj.reshape(bsz, s, main_w), la.reshape(bsz, s, -1),
                                row(gla_gnorm_w[layer]), casts, chunk=128, sub=16, heads=GLA_HEADS)
            if layer == n_gla - 1:
                w_qkv = cast[3]
        else:
            j = layer - n_gla
            qkv = _norm_matmul2(h, row(attn_norm_w[layer]), row(kv_norm_w), w_qkv, sb_w_q.shape[2],
                                tm=1024, tn=1024, scale_a=hd ** -0.5 * LOG2_E)
            casts.append([(sb_w_out, j)])
            o, cast = _sb_attention(qkv.reshape(bsz, s, -1), casts, tq=128, group=8)
        w_gate_up, w_down, w_out = cast[:3]
        h = _proj_res(o.reshape(t, -1), w_out, h, tm=512)
        h = _ffn(h, row(ffn_norm_w[layer]), w_gate_up, w_down, row(final_norm_w),
                 tm=1024, tf=512, final_norm=layer == depth - 1)
    return h.reshape(bsz, s, d)
```

```python
import functools

import jax
import jax.numpy as jnp
from jax import lax
from jax.experimental import pallas as pl
from jax.experimental.pallas import tpu as pltpu

EPS = 1e-6
GLA_HEADS = 4
GLA_GATE_TAU = 16.0
SB_HEADS = 16
BF = jnp.bfloat16
F32 = jnp.float32
LANES = 128
MIB = 2 ** 20

_NT = (((1,), (1,)), ((), ()))
_TN = (((0,), (0,)), ((), ()))


def _params(sem, vmem_mib):
    return pltpu.CompilerParams(dimension_semantics=sem, vmem_limit_bytes=vmem_mib * MIB)


def _dot(a, b):
    return jnp.dot(a, b, preferred_element_type=F32)


def _rms(x, w):
    return x * lax.rsqrt(jnp.mean(x * x, axis=-1, keepdims=True) + EPS) * w


def _split2(x):
    hi = x.astype(BF)
    lo = (x - hi.astype(F32)).astype(BF)
    return hi, lo


def _log_sigmoid(x):
    return jnp.minimum(x, 0.0) - jnp.log(1.0 + jnp.exp(-jnp.abs(x)))


def _silu(x):
    return x * (1.0 / (1.0 + jnp.exp(-x)))


BF16_SUBLANES = 16


def _cast_specs(jobs, grid):
    steps = 1
    for g in grid:
        steps *= g

    def linear(idx):
        s = 0
        for g, i in zip(grid, idx):
            s = s * g + i
        return s

    operands, in_specs, out_specs, out_shapes, parts = [], [], [], [], []
    for job in jobs:
        n_rows = job[0][0].shape[-2]
        n_blk = max(n for n in range(1, steps + 1)
                    if n_rows % n == 0 and (n_rows // n) % BF16_SUBLANES == 0)
        rows = n_rows // n_blk
        blk = lambda *idx, n_blk=n_blk: linear(idx) * n_blk // steps
        for arr, lead in job:
            assert arr.shape[-2] == n_rows
            operands.append(arr)
            if lead is None:
                in_specs.append(pl.BlockSpec((rows, arr.shape[-1]), lambda *idx, blk=blk: (blk(*idx), 0)))
            else:
                in_specs.append(pl.BlockSpec((None, rows, arr.shape[-1]),
                                             lambda *idx, blk=blk, lead=lead: (lead, blk(*idx), 0)))
        width = sum(arr.shape[-1] for arr, _ in job)
        out_specs.append(pl.BlockSpec((rows, width), lambda *idx, blk=blk: (blk(*idx), 0)))
        out_shapes.append(jax.ShapeDtypeStruct((n_rows, width), BF))
        parts.append(len(job))
    return operands, in_specs, out_specs, out_shapes, parts


def _with_casts(body, n_in, n_out, parts):
    n_src = sum(parts)

    def kernel(*refs):
        ins, rest = refs[:n_in], refs[n_in:]
        srcs, rest = rest[:n_src], rest[n_src:]
        outs, rest = rest[:n_out], rest[n_out:]
        dsts, scratch = rest[:len(parts)], rest[len(parts):]
        body(*ins, *outs, *scratch)
        srcs = list(srcs)
        for dst, n_parts in zip(dsts, parts):
            col = 0
            for _ in range(n_parts):
                src = srcs.pop(0)
                dst[:, col:col + src.shape[-1]] = src[...].astype(BF)
                col += src.shape[-1]

    return kernel


def _norm_matmul2_kernel(x_ref, nwa_ref, nwb_ref, w_ref, o_ref, xn_ref, *, na, scale_a):
    j = pl.program_id(1)

    @pl.when(j == 0)
    def _():
        x = x_ref[...]
        xhat = x * lax.rsqrt(jnp.mean(x * x, axis=-1, keepdims=True) + EPS)
        xn_ref[0] = (xhat * nwa_ref[...]).astype(BF)
        xn_ref[1] = (xhat * nwb_ref[...]).astype(BF)

    @pl.when(j < na)
    def _():
        o_ref[...] = (_dot(xn_ref[0], w_ref[...]) * scale_a).astype(o_ref.dtype)

    @pl.when(j >= na)
    def _():
        o_ref[...] = _dot(xn_ref[1], w_ref[...]).astype(o_ref.dtype)


def _norm_matmul2(x, nwa, nwb, w, n_a, *, tm, tn, scale_a):
    t, d = x.shape
    n = w.shape[1]
    return pl.pallas_call(
        functools.partial(_norm_matmul2_kernel, na=n_a // tn, scale_a=scale_a),
        grid=(t // tm, n // tn),
        in_specs=[
            pl.BlockSpec((tm, d), lambda i, j: (i, 0)),
            pl.BlockSpec((1, d), lambda i, j: (0, 0)),
            pl.BlockSpec((1, d), lambda i, j: (0, 0)),
            pl.BlockSpec((d, tn), lambda i, j: (0, j)),
        ],
        out_specs=pl.BlockSpec((tm, tn), lambda i, j: (i, j)),
        out_shape=jax.ShapeDtypeStruct((t, n), BF),
        scratch_shapes=[pltpu.VMEM((2, tm, d), BF)],
        compiler_params=_params(("parallel", "arbitrary"), 56),
        name="norm_matmul2",
    )(x, nwa, nwb, w)


def _gla_inproj_kernel(x_ref, nw_ref, w_ref, wgl_ref, wup_ref, bg_ref, o_ref, la_ref, xn_ref, gl_ref):
    def dot_t(a, w_t):
        return lax.dot_general(a, w_t, _NT, preferred_element_type=F32)

    @pl.when(pl.program_id(1) == 0)
    def _():
        xn = _rms(x_ref[...], nw_ref[...]).astype(BF)
        xn_ref[...] = xn
        gl_ref[0], gl_ref[1] = _split2(dot_t(xn, wgl_ref[...]))

    up_hi, up_lo = _split2(wup_ref[...])
    g = _dot(gl_ref[0], up_hi) + _dot(gl_ref[1], up_hi) + _dot(gl_ref[0], up_lo) + bg_ref[...]
    la_ref[...] = _log_sigmoid(g) * (1.0 / GLA_GATE_TAU)
    o_ref[...] = dot_t(xn_ref[...], w_ref[...].astype(BF)).astype(o_ref.dtype)


def _gla_inproj(x, nw, w_t_all, layer, n, wgl_t, wup, bg, *, tm, tn):
    t, d = x.shape
    steps = n // tn
    slab = wup.shape[1] // steps
    assert slab * steps == wup.shape[1] and slab % LANES == 0
    return pl.pallas_call(
        _gla_inproj_kernel,
        grid=(t // tm, steps),
        in_specs=[
            pl.BlockSpec((tm, d), lambda i, j: (i, 0)),
            pl.BlockSpec((1, d), lambda i, j: (0, 0)),
            pl.BlockSpec((None, tn, d), lambda i, j: (layer, j, 0)),
            pl.BlockSpec(wgl_t.shape, lambda i, j: (0, 0)),
            pl.BlockSpec((wup.shape[0], slab), lambda i, j: (0, j)),
            pl.BlockSpec((1, slab), lambda i, j: (0, j)),
        ],
        out_specs=[
            pl.BlockSpec((tm, tn), lambda i, j: (i, j)),
            pl.BlockSpec((tm, slab), lambda i, j: (i, j)),
        ],
        out_shape=[
            jax.ShapeDtypeStruct((t, n), BF),
            jax.ShapeDtypeStruct((t, wup.shape[1]), F32),
        ],
        scratch_shapes=[pltpu.VMEM((tm, d), BF), pltpu.VMEM((2, tm, wgl_t.shape[0]), BF)],
        compiler_params=_params(("parallel", "arbitrary"), 56),
        name="gla_inproj",
    )(x, nw, w_t_all, wgl_t, wup, bg)


def _gla_kernel(q_ref, k_ref, v_ref, r_ref, g_ref, gw_ref, o_ref, st_ref, qh_ref, kh_ref,
                *, chunk, sub, heads, scale):
    n_sub = chunk // sub
    dk = q_ref.shape[-1] // heads
    dv = v_ref.shape[-1] // heads
    hs = range(heads)

    @pl.when(pl.program_id(2) == 0)
    def _():
        st_ref[...] = jnp.zeros_like(st_ref)
        qh_ref[...] = jnp.zeros_like(qh_ref)
        kh_ref[...] = jnp.zeros_like(kh_ref)

    def head_cols(ref, h, width):
        return ref[0, :, h * width:(h + 1) * width]

    q = [head_cols(q_ref, h, dk).astype(F32) * scale for h in hs]
    k = [head_cols(k_ref, h, dk).astype(F32) for h in hs]
    g = [head_cols(g_ref, h, dk) for h in hs]

    row = lax.broadcasted_iota(jnp.int32, (chunk, chunk), 0)
    col = lax.broadcasted_iota(jnp.int32, (chunk, chunk), 1)
    tri = jnp.where(col <= row, 1.0, 0.0).astype(BF)
    diff = jnp.where(row // sub == col // sub, row - col, -1)

    b = []
    for h in hs:
        g_hi, g_lo = _split2(g[h])
        b.append(_dot(tri, g_hi) + _dot(tri, g_lo))

    st = [st_ref[h] for h in hs]
    o = [lax.dot_general((q[h] * jnp.exp(b[h])).astype(BF), st[h].astype(BF), _NT,
                         preferred_element_type=F32) for h in hs]

    def store_rows(ref, h, start, cols, x):
        lo = start // BF16_SUBLANES * BF16_SUBLANES
        hi = -(-(start + x.shape[0]) // BF16_SUBLANES) * BF16_SUBLANES
        pieces = [jnp.zeros((start - lo, x.shape[1]), F32)] if start > lo else []
        pieces.append(x)
        if hi > start + x.shape[0]:
            pieces.append(jnp.zeros((hi - start - x.shape[0], x.shape[1]), F32))
        x = jnp.concatenate(pieces, axis=0) if len(pieces) > 1 else x
        ref[h, lo:hi, cols] = x.astype(BF)

    scores = []
    for h in hs:
        k_run = None
        for i in range(1, n_sub):
            beta = b[h][i * sub - 1:i * sub, :]
            rows = slice(i * sub, (i + 1) * sub)
            prev = slice((i - 1) * sub, i * sub)
            cols = slice((i - 1) * dk, i * dk)
            store_rows(qh_ref, h, i * sub, cols, q[h][rows] * jnp.exp(b[h][rows] - beta))
            k_new = k[h][prev] * jnp.exp(beta - b[h][prev])
            if k_run is None:
                k_run = k_new
            else:
                k_run = jnp.concatenate([k_run * jnp.exp(beta - beta_prev), k_new], axis=0)
            store_rows(kh_ref, h, 0, cols, k_run)
            beta_prev = beta
        scores.append(lax.dot_general(qh_ref[h], kh_ref[h], _NT, preferred_element_type=F32))

    for h in hs:
        decay = jnp.exp(g[h])
        w = k[h]
        for d in range(sub):
            if d > 0:
                w = decay * pltpu.roll(w, 1, 0)
            scores[h] = jnp.where(diff == d, jnp.sum(q[h] * w, axis=-1, keepdims=True), scores[h])

    v = [head_cols(v_ref, h, dv) for h in hs]
    o = [o[h] + _dot(scores[h].astype(BF), v[h]) for h in hs]

    for h in hs:
        b_last = b[h][chunk - 1:chunk, :]
        k_dec = (k[h] * jnp.exp(b_last - b[h])).astype(BF)
        st_ref[h] = st[h] * jnp.exp(b_last) + lax.dot_general(v[h], k_dec, _TN, preferred_element_type=F32)

    for h in hs:
        r = head_cols(r_ref, h, dv).astype(F32)
        o_ref[0, :, h * dv:(h + 1) * dv] = (_rms(o[h], gw_ref[...]) * _silu(r)).astype(o_ref.dtype)


def _gla_core(proj, la, gw, cast_jobs, *, chunk, sub, heads):
    bsz, s, _ = proj.shape
    dk = la.shape[-1] // GLA_HEADS
    dv = gw.shape[-1]
    groups = GLA_HEADS // heads
    k_blocks = groups
    v_blocks = 2 * GLA_HEADS * dk // (heads * dv)
    r_blocks = v_blocks + groups
    expanded = (chunk // sub - 1) * dk
    grid = (bsz, groups, s // chunk)
    c_ops, c_in, c_out, c_shapes, c_parts = _cast_specs(cast_jobs, grid)
    body = functools.partial(_gla_kernel, chunk=chunk, sub=sub, heads=heads, scale=dk ** -0.5)
    outs = pl.pallas_call(
        _with_casts(body, 6, 1, c_parts),
        grid=grid,
        in_specs=[
            pl.BlockSpec((1, chunk, heads * dk), lambda b, h, n: (b, n, h)),
            pl.BlockSpec((1, chunk, heads * dk), lambda b, h, n: (b, n, k_blocks + h)),
            pl.BlockSpec((1, chunk, heads * dv), lambda b, h, n: (b, n, v_blocks + h)),
            pl.BlockSpec((1, chunk, heads * dv), lambda b, h, n: (b, n, r_blocks + h)),
            pl.BlockSpec((1, chunk, heads * dk), lambda b, h, n: (b, n, h)),
            pl.BlockSpec((1, dv), lambda b, h, n: (0, 0)),
        ] + c_in,
        out_specs=[pl.BlockSpec((1, chunk, heads * dv), lambda b, h, n: (b, n, h))] + c_out,
        out_shape=[jax.ShapeDtypeStruct((bsz, s, GLA_HEADS * dv), BF)] + c_shapes,
        scratch_shapes=[
            pltpu.VMEM((heads, dv, dk), F32),
            pltpu.VMEM((heads, chunk, expanded), BF),
            pltpu.VMEM((heads, chunk, expanded), BF),
        ],
        compiler_params=_params(("arbitrary", "arbitrary", "arbitrary"), 48),
        name="gla_core",
    )(proj, proj, proj, proj, la, gw, *c_ops)
    return outs[0], outs[1:]


def _proj_res_kernel(x_ref, w_ref, res_ref, o_ref):
    o_ref[...] = res_ref[...] + _dot(x_ref[...], w_ref[...])


def _proj_res(x, w, res, *, tm):
    t, kdim = x.shape
    n = w.shape[1]
    return pl.pallas_call(
        _proj_res_kernel,
        grid=(t // tm,),
        in_specs=[
            pl.BlockSpec((tm, kdim), lambda i: (i, 0)),
            pl.BlockSpec((kdim, n), lambda i: (0, 0)),
            pl.BlockSpec((tm, n), lambda i: (i, 0)),
        ],
        out_specs=pl.BlockSpec((tm, n), lambda i: (i, 0)),
        out_shape=jax.ShapeDtypeStruct((t, n), F32),
        compiler_params=_params(("parallel",), 48),
        name="proj_res",
    )(x, w, res)


def _ffn_kernel(h_ref, nw_ref, wg_ref, wu_ref, wd_ref, fw_ref, o_ref, xn_ref, *, final_norm):
    f = pl.program_id(1)

    @pl.when(f == 0)
    def _():
        h = h_ref[...]
        xn_ref[...] = _rms(h, nw_ref[...]).astype(BF)
        o_ref[...] = h

    xn = xn_ref[...]
    act = (_silu(_dot(xn, wg_ref[...])) * _dot(xn, wu_ref[...])).astype(BF)
    o_ref[...] += _dot(act, wd_ref[...])

    if final_norm:
        @pl.when(f == pl.num_programs(1) - 1)
        def _():
            o_ref[...] = _rms(o_ref[...], fw_ref[...])


def _ffn(h, nw, w_gate_up, w_down, fw, *, tm, tf, final_norm):
    t, d = h.shape
    d_ff = w_down.shape[0]
    nf = d_ff // tf
    return pl.pallas_call(
        functools.partial(_ffn_kernel, final_norm=final_norm),
        grid=(t // tm, nf),
        in_specs=[
            pl.BlockSpec((tm, d), lambda i, f: (i, 0)),
            pl.BlockSpec((1, d), lambda i, f: (0, 0)),
            pl.BlockSpec((d, tf), lambda i, f: (0, f)),
            pl.BlockSpec((d, tf), lambda i, f: (0, nf + f)),
            pl.BlockSpec((tf, d), lambda i, f: (f, 0)),
            pl.BlockSpec((1, d), lambda i, f: (0, 0)),
        ],
        out_specs=pl.BlockSpec((tm, d), lambda i, f: (i, 0)),
        out_shape=jax.ShapeDtypeStruct((t, d), F32),
        scratch_shapes=[pltpu.VMEM((tm, d), BF)],
        compiler_params=_params(("parallel", "arbitrary"), 60),
        name="ffn",
    )(h, nw, w_gate_up, w_gate_up, w_down, fw)


_EXP2_UNDERFLOW = -127.0
LOG2_E = 1.4426950408889634


def _sb_kernel(q_ref, k_ref, v_ref, o_ref, *, tq, group):
    s = q_ref.shape[1]
    tw = 2 * tq
    def later(n):
        r = lax.broadcasted_iota(jnp.int32, (n, n), 0)
        c = lax.broadcasted_iota(jnp.int32, (n, n), 1)
        return jnp.where(r > c, 1.0, 0.0).astype(BF)

    def causal(n_keys, offset):
        qr = lax.broadcasted_iota(jnp.int32, (tq, n_keys), 0)
        kc = lax.broadcasted_iota(jnp.int32, (tq, n_keys), 1)
        return kc < qr + offset

    later_w, later_b = later(tw), later(tq)
    diag_mask = causal(tq, 0)
    win_mask = causal(tw, tq)

    def scores(q, start, width, mask):
        kb = k_ref[0, pl.ds(start, width), :]
        z = lax.dot_general(q, kb, _NT, preferred_element_type=F32)
        nz = -z
        lf = jnp.minimum(nz, 0.0) - jnp.log2(1.0 + jnp.exp2(jnp.minimum(z, nz)))
        if mask is not None:
            lf = jnp.where(mask, lf, 0.0)
        return z, lf

    def suffix(lf, later, carry):
        after = _dot(lf.astype(BF), later)
        if carry is not None:
            after = after + carry
        return after, after[:, :1] + lf[:, :1]

    def weighted(z, lf, after, start, width, mask):
        a = jnp.exp2(z + lf + after)
        if mask is not None:
            a = jnp.where(mask, a, 0.0)
        return _dot(a.astype(BF), v_ref[0, pl.ds(start, width), :])

    def window(tile):
        if isinstance(tile, int):
            if tile == 0:
                return 0, tq, later_b, diag_mask
            return (tile - 1) * tq, tw, later_w, win_mask
        return pl.multiple_of((tile - 1) * tq, tq), tw, later_w, win_mask

    def row0(tile):
        return tile * tq if isinstance(tile, int) else pl.multiple_of(tile * tq, tq)

    def extend(tile, q, carry, top, acc):
        def cond(st):
            return jnp.logical_and(st[0] >= 0, st[1] > _EXP2_UNDERFLOW)

        def body(st):
            j, _, carry, acc = st
            start = pl.multiple_of(j * tq, tq)
            z, lf = scores(q, start, tq, None)
            after, carry = suffix(lf, later_b, carry)
            acc = acc + weighted(z, lf, after, start, tq, None)
            return j - 1, jnp.max(carry), carry, acc

        init = (jnp.asarray(tile - 2, jnp.int32), top, carry, acc)
        return lax.while_loop(cond, body, init)[3]

    def run_group(base):
        tiles = [base + g for g in range(group)]
        wins = [window(t) for t in tiles]
        qs = [q_ref[0, pl.ds(row0(t), tq), :] for t in tiles]
        zl = [scores(q, w[0], w[1], w[3]) for q, w in zip(qs, wins)]
        ac = [suffix(lf, w[2], None) for (_, lf), w in zip(zl, wins)]
        accs = [weighted(z, lf, after, w[0], w[1], w[3]) for (z, lf), (after, _), w in zip(zl, ac, wins)]
        tops = [jnp.max(carry) for _, carry in ac]
        for t, q, (_, carry), top, acc in zip(tiles, qs, ac, tops, accs):
            acc = extend(t, q, carry, top, acc)
            o_ref[0, pl.ds(row0(t), tq), :] = acc.astype(o_ref.dtype)

    run_group(0)

    def loop_body(it, carry):
        run_group(it * group)
        return carry

    lax.fori_loop(1, s // (tq * group), loop_body, 0)


def _sb_attention(qkv, cast_jobs, *, tq, group):
    bsz, s, w3 = qkv.shape
    w = w3 // 3
    hd = w // SB_HEADS
    grid = (bsz, SB_HEADS)
    c_ops, c_in, c_out, c_shapes, c_parts = _cast_specs(cast_jobs, grid)
    outs = pl.pallas_call(
        _with_casts(functools.partial(_sb_kernel, tq=tq, group=group), 3, 1, c_parts),
        grid=grid,
        in_specs=[
            pl.BlockSpec((1, s, hd), lambda b, h: (b, 0, h)),
            pl.BlockSpec((1, s, hd), lambda b, h: (b, 0, SB_HEADS + h)),
            pl.BlockSpec((1, s, hd), lambda b, h: (b, 0, 2 * SB_HEADS + h)),
        ] + c_in,
        out_specs=[pl.BlockSpec((1, s, hd), lambda b, h: (b, 0, h))] + c_out,
        out_shape=[jax.ShapeDtypeStruct((bsz, s, w), BF)] + c_shapes,
        compiler_params=_params(("arbitrary", "arbitrary"), 48),
        name="sb_attn",
    )(qkv, qkv, qkv, *c_ops)
    return outs[0], outs[1:]


def kernel(x, attn_norm_w, ffn_norm_w, gla_w_in, gla_w_gate_up, gla_b_gate, gla_gnorm_w, gla_w_out,
           kv_norm_w, sb_w_kv, sb_w_q, sb_w_out, ffn_w_gate_up, ffn_w_down, final_norm_w):
    bsz, s, d = x.shape
    t = bsz * s
    depth = attn_norm_w.shape[0]
    n_gla = gla_w_in.shape[0]
    rank = gla_w_gate_up.shape[1]
    main_w = gla_w_in.shape[2] - rank
    hd = d // SB_HEADS
    row = lambda v: v.reshape(1, -1)
    assert n_gla >= 1 and depth - n_gla == 1

    w_in_t = jnp.swapaxes(gla_w_in, 1, 2)

    h = x.reshape(t, d)
    for layer in range(depth):
        casts = [[(ffn_w_gate_up, layer)], [(ffn_w_down, layer)]]
        if layer < n_gla:
            wgl_t = jnp.pad(w_in_t[layer, main_w:], ((0, LANES - rank), (0, 0))).astype(BF)
            wup = jnp.pad(gla_w_gate_up[layer], ((0, LANES - rank), (0, 0)))
            proj, la = _gla_inproj(h, row(attn_norm_w[layer]), w_in_t, layer, main_w, wgl_t, wup,
                                   row(gla_b_gate[layer]), tm=1024, tn=768)
            casts.append([(gla_w_out, layer)])
            if layer == n_gla - 1:
                casts.append([(sb_w_q, 0), (sb_w_kv, None)])
            o, cast = _gla_core(proj.reshape(bsz, s, main_w), la.reshape(bsz, s, -1),
                                row(gla_gnorm_w[layer]), casts, chunk=128, sub=8, heads=GLA_HEADS)
            if layer == n_gla - 1:
                w_qkv = cast[3]
        else:
            j = layer - n_gla
            qkv = _norm_matmul2(h, row(attn_norm_w[layer]), row(kv_norm_w), w_qkv, sb_w_q.shape[2],
                                tm=1024, tn=1024, scale_a=hd ** -0.5 * LOG2_E)
            casts.append([(sb_w_out, j)])
            o, cast = _sb_attention(qkv.reshape(bsz, s, -1), casts, tq=128, group=8)
        w_gate_up, w_down, w_out = cast[:3]
        h = _proj_res(o.reshape(t, -1), w_out, h, tm=512)
        h = _ffn(h, row(ffn_norm_w[layer]), w_gate_up, w_down, row(final_norm_w),
                 tm=1024, tf=512, final_norm=layer == depth - 1)
    return h.reshape(bsz, s, d)
```

```python
import functools

import jax
import jax.numpy as jnp
from jax import lax
from jax.experimental import pallas as pl
from jax.experimental.pallas import tpu as pltpu

EPS = 1e-6
GLA_HEADS = 4
GLA_GATE_TAU = 16.0
SB_HEADS = 16
BF = jnp.bfloat16
F32 = jnp.float32
LANES = 128
BF16_SUBLANES = 16
MIB = 2 ** 20
V7X_VMEM_BYTES = 64 * MIB
VMEM_HEADROOM = 4 * MIB
VMEM_SPILL_ALLOWANCE = 2 * MIB

TILES = dict(
    inproj=dict(tm=1024, tn=768),
    gla=dict(chunk=128, sub=8, heads=GLA_HEADS),
    proj=dict(tm=512),
    ffn=dict(tm=1024, tf=512),
    qkv=dict(tm=1024, tn=1024),
    sb=dict(tq=128, group=8),
)

_NT = (((1,), (1,)), ((), ()))
_TN = (((0,), (0,)), ((), ()))


def _nbytes(shape, dtype):
    n = jnp.dtype(dtype).itemsize
    for dim in shape:
        n *= dim or 1
    return n


class _Windows:
    def __init__(self):
        self.bytes = 0

    def spec(self, block, dtype, index_map):
        self.bytes += _nbytes(block, dtype)
        return pl.BlockSpec(block, index_map)

    def params(self, sem, scratch=(), temps=0):
        need = 2 * self.bytes + sum(_nbytes(shape, dtype) for shape, dtype in scratch) + temps
        need = -(-(need + VMEM_SPILL_ALLOWANCE) // MIB) * MIB
        assert need <= V7X_VMEM_BYTES - VMEM_HEADROOM, need
        return pltpu.CompilerParams(dimension_semantics=sem, vmem_limit_bytes=need)


def _vmem(scratch):
    return [pltpu.VMEM(shape, dtype) for shape, dtype in scratch]


def _dot(a, b):
    return jnp.dot(a, b, preferred_element_type=F32)


def _rms(x, w):
    return x * lax.rsqrt(jnp.mean(x * x, axis=-1, keepdims=True) + EPS) * w


def _split2(x):
    hi = x.astype(BF)
    lo = (x - hi.astype(F32)).astype(BF)
    return hi, lo


def _log_sigmoid(x):
    return jnp.minimum(x, 0.0) - jnp.log(1.0 + jnp.exp(-jnp.abs(x)))


def _silu(x):
    return x * (1.0 / (1.0 + jnp.exp(-x)))


def _cast_specs(win, jobs, grid):
    steps = 1
    for g in grid:
        steps *= g

    def linear(idx):
        s = 0
        for g, i in zip(grid, idx):
            s = s * g + i
        return s

    operands, in_specs, out_specs, out_shapes, parts = [], [], [], [], []
    for job in jobs:
        n_rows = job[0][0].shape[-2]
        n_blk = max(n for n in range(1, steps + 1)
                    if n_rows % n == 0 and (n_rows // n) % BF16_SUBLANES == 0)
        rows = n_rows // n_blk
        blk = lambda *idx, n_blk=n_blk: linear(idx) * n_blk // steps
        for arr, lead in job:
            assert arr.shape[-2] == n_rows
            operands.append(arr)
            if lead is None:
                in_specs.append(win.spec((rows, arr.shape[-1]), F32, lambda *idx, blk=blk: (blk(*idx), 0)))
            else:
                in_specs.append(win.spec((None, rows, arr.shape[-1]), F32,
                                         lambda *idx, blk=blk, lead=lead: (lead, blk(*idx), 0)))
        width = sum(arr.shape[-1] for arr, _ in job)
        out_specs.append(win.spec((rows, width), BF, lambda *idx, blk=blk: (blk(*idx), 0)))
        out_shapes.append(jax.ShapeDtypeStruct((n_rows, width), BF))
        parts.append(len(job))
    return operands, in_specs, out_specs, out_shapes, parts


def _with_casts(body, n_in, n_out, parts):
    n_src = sum(parts)

    def kernel(*refs):
        ins, rest = refs[:n_in], refs[n_in:]
        srcs, rest = rest[:n_src], rest[n_src:]
        outs, rest = rest[:n_out], rest[n_out:]
        dsts, scratch = rest[:len(parts)], rest[len(parts):]
        body(*ins, *outs, *scratch)
        srcs = list(srcs)
        for dst, n_parts in zip(dsts, parts):
            col = 0
            for _ in range(n_parts):
                src = srcs.pop(0)
                dst[:, col:col + src.shape[-1]] = src[...].astype(BF)
                col += src.shape[-1]

    return kernel


def _norm_matmul2_kernel(x_ref, nwa_ref, nwb_ref, w_ref, o_ref, xn_ref, *, na, scale_a):
    j = pl.program_id(1)

    @pl.when(j == 0)
    def _():
        x = x_ref[...]
        xhat = x * lax.rsqrt(jnp.mean(x * x, axis=-1, keepdims=True) + EPS)
        xn_ref[0] = (xhat * nwa_ref[...]).astype(BF)
        xn_ref[1] = (xhat * nwb_ref[...]).astype(BF)

    @pl.when(j < na)
    def _():
        o_ref[...] = (_dot(xn_ref[0], w_ref[...]) * scale_a).astype(o_ref.dtype)

    @pl.when(j >= na)
    def _():
        o_ref[...] = _dot(xn_ref[1], w_ref[...]).astype(o_ref.dtype)


def _norm_matmul2(x, nwa, nwb, w, n_a, *, tm, tn, scale_a):
    t, d = x.shape
    n = w.shape[1]
    win = _Windows()
    scratch = [((2, tm, d), BF)]
    temps = 2 * _nbytes((tm, d), F32) + _nbytes((tm, tn), F32)
    return pl.pallas_call(
        functools.partial(_norm_matmul2_kernel, na=n_a // tn, scale_a=scale_a),
        grid=(t // tm, n // tn),
        in_specs=[
            win.spec((tm, d), F32, lambda i, j: (i, 0)),
            win.spec((1, d), F32, lambda i, j: (0, 0)),
            win.spec((1, d), F32, lambda i, j: (0, 0)),
            win.spec((d, tn), BF, lambda i, j: (0, j)),
        ],
        out_specs=win.spec((tm, tn), BF, lambda i, j: (i, j)),
        out_shape=jax.ShapeDtypeStruct((t, n), BF),
        scratch_shapes=_vmem(scratch),
        compiler_params=win.params(("parallel", "arbitrary"), scratch, temps),
        name="norm_matmul2",
    )(x, nwa, nwb, w)


def _gla_inproj_kernel(x_ref, nw_ref, w_ref, wgl_ref, wup_ref, bg_ref, o_ref, la_ref, xn_ref, gl_ref, *, rank):
    def dot_t(a, w_t):
        return lax.dot_general(a, w_t, _NT, preferred_element_type=F32)

    @pl.when(pl.program_id(1) == 0)
    def _():
        xn = _rms(x_ref[...], nw_ref[...]).astype(BF)
        xn_ref[...] = xn
        gl_hi, gl_lo = _split2(dot_t(xn, wgl_ref[...]))
        group = lax.broadcasted_iota(jnp.int32, gl_hi.shape, 1) // rank
        gl_ref[...] = jnp.where(group == 1, gl_lo, gl_hi)

    up_hi, up_lo = _split2(wup_ref[...])
    group = lax.broadcasted_iota(jnp.int32, up_hi.shape, 0) // rank
    g = _dot(gl_ref[...], jnp.where(group == 2, up_lo, up_hi)) + bg_ref[...]
    la_ref[...] = _log_sigmoid(g) * (1.0 / GLA_GATE_TAU)
    o_ref[...] = dot_t(xn_ref[...], w_ref[...].astype(BF)).astype(o_ref.dtype)


def _gla_inproj(x, nw, w_t_all, layer, n, rank, w_gate_up, bg, *, tm, tn):
    t, d = x.shape
    steps = n // tn
    qk_w = w_gate_up.shape[1]
    slab = qk_w // steps
    assert slab * steps == qk_w and slab % LANES == 0 and 3 * rank <= LANES
    three = lambda a: jnp.pad(jnp.tile(a, (3, 1)), ((0, LANES - 3 * rank), (0, 0)))
    wgl_t = three(w_t_all[layer, n:n + rank]).astype(BF)
    wup = three(w_gate_up)
    win = _Windows()
    scratch = [((tm, d), BF), ((tm, LANES), BF)]
    temps = _nbytes((tm, tn), F32) + _nbytes((tn, d), BF)
    return pl.pallas_call(
        functools.partial(_gla_inproj_kernel, rank=rank),
        grid=(t // tm, steps),
        in_specs=[
            win.spec((tm, d), F32, lambda i, j: (i, 0)),
            win.spec((1, d), F32, lambda i, j: (0, 0)),
            win.spec((None, tn, d), F32, lambda i, j: (layer, j, 0)),
            win.spec((LANES, d), BF, lambda i, j: (0, 0)),
            win.spec((LANES, slab), F32, lambda i, j: (0, j)),
            win.spec((1, slab), F32, lambda i, j: (0, j)),
        ],
        out_specs=[
            win.spec((tm, tn), BF, lambda i, j: (i, j)),
            win.spec((tm, slab), F32, lambda i, j: (i, j)),
        ],
        out_shape=[
            jax.ShapeDtypeStruct((t, n), BF),
            jax.ShapeDtypeStruct((t, qk_w), F32),
        ],
        scratch_shapes=_vmem(scratch),
        compiler_params=win.params(("parallel", "arbitrary"), scratch, temps),
        name="gla_inproj",
    )(x, nw, w_t_all, wgl_t, wup, bg)


def _gla_kernel(q_ref, k_ref, v_ref, r_ref, g_ref, gw_ref, o_ref, st_ref, qh_ref, kh_ref,
                *, chunk, sub, heads, scale):
    n_sub = chunk // sub
    dk = q_ref.shape[-1] // heads
    dv = v_ref.shape[-1] // heads
    hs = range(heads)

    @pl.when(pl.program_id(2) == 0)
    def _():
        st_ref[...] = jnp.zeros_like(st_ref)
        qh_ref[...] = jnp.zeros_like(qh_ref)
        kh_ref[...] = jnp.zeros_like(kh_ref)

    def head_cols(ref, h, width):
        return ref[0, :, h * width:(h + 1) * width]

    q = [head_cols(q_ref, h, dk).astype(F32) * scale for h in hs]
    k = [head_cols(k_ref, h, dk).astype(F32) for h in hs]
    g = [head_cols(g_ref, h, dk) for h in hs]

    row = lax.broadcasted_iota(jnp.int32, (chunk, chunk), 0)
    col = lax.broadcasted_iota(jnp.int32, (chunk, chunk), 1)
    tri = jnp.where(col <= row, 1.0, 0.0).astype(BF)
    diff = jnp.where(row // sub == col // sub, row - col, -1)

    b = []
    for h in hs:
        g_hi, g_lo = _split2(g[h])
        b.append(_dot(tri, g_hi) + _dot(tri, g_lo))

    st = [st_ref[h] for h in hs]
    o = [lax.dot_general((q[h] * jnp.exp(b[h])).astype(BF), st[h].astype(BF), _NT,
                         preferred_element_type=F32) for h in hs]

    def store_rows(ref, h, start, cols, x):
        lo = start // BF16_SUBLANES * BF16_SUBLANES
        hi = -(-(start + x.shape[0]) // BF16_SUBLANES) * BF16_SUBLANES
        pieces = [jnp.zeros((start - lo, x.shape[1]), F32)] if start > lo else []
        pieces.append(x)
        if hi > start + x.shape[0]:
            pieces.append(jnp.zeros((hi - start - x.shape[0], x.shape[1]), F32))
        x = jnp.concatenate(pieces, axis=0) if len(pieces) > 1 else x
        ref[h, lo:hi, cols] = x.astype(BF)

    scores = []
    for h in hs:
        k_run = None
        for i in range(1, n_sub):
            beta = b[h][i * sub - 1:i * sub, :]
            rows = slice(i * sub, (i + 1) * sub)
            prev = slice((i - 1) * sub, i * sub)
            cols = slice((i - 1) * dk, i * dk)
            store_rows(qh_ref, h, i * sub, cols, q[h][rows] * jnp.exp(b[h][rows] - beta))
            k_new = k[h][prev] * jnp.exp(beta - b[h][prev])
            if k_run is None:
                k_run = k_new
            else:
                k_run = jnp.concatenate([k_run * jnp.exp(beta - beta_prev), k_new], axis=0)
            store_rows(kh_ref, h, 0, cols, k_run)
            beta_prev = beta
        scores.append(lax.dot_general(qh_ref[h], kh_ref[h], _NT, preferred_element_type=F32))

    for h in hs:
        decay = jnp.exp(g[h])
        w = k[h]
        for d in range(sub):
            if d > 0:
                w = decay * pltpu.roll(w, 1, 0)
            scores[h] = jnp.where(diff == d, jnp.sum(q[h] * w, axis=-1, keepdims=True), scores[h])

    v = [head_cols(v_ref, h, dv) for h in hs]
    o = [o[h] + _dot(scores[h].astype(BF), v[h]) for h in hs]

    for h in hs:
        b_last = b[h][chunk - 1:chunk, :]
        k_dec = (k[h] * jnp.exp(b_last - b[h])).astype(BF)
        st_ref[h] = st[h] * jnp.exp(b_last) + lax.dot_general(v[h], k_dec, _TN, preferred_element_type=F32)

    for h in hs:
        r = head_cols(r_ref, h, dv).astype(F32)
        o_ref[0, :, h * dv:(h + 1) * dv] = (_rms(o[h], gw_ref[...]) * _silu(r)).astype(o_ref.dtype)


def _gla_core(proj, la, gw, cast_jobs, *, chunk, sub, heads):
    bsz, s, _ = proj.shape
    dk = la.shape[-1] // GLA_HEADS
    dv = gw.shape[-1]
    groups = GLA_HEADS // heads
    k_blocks = groups
    v_blocks = 2 * GLA_HEADS * dk // (heads * dv)
    r_blocks = v_blocks + groups
    expanded = (chunk // sub - 1) * dk
    grid = (bsz, groups, s // chunk)
    win = _Windows()
    c_ops, c_in, c_out, c_shapes, c_parts = _cast_specs(win, cast_jobs, grid)
    scratch = [((heads, dv, dk), F32), ((heads, chunk, expanded), BF), ((heads, chunk, expanded), BF)]
    temps = heads * (6 * _nbytes((chunk, dk), F32) + 2 * _nbytes((chunk, dv), F32))
    body = functools.partial(_gla_kernel, chunk=chunk, sub=sub, heads=heads, scale=dk ** -0.5)
    outs = pl.pallas_call(
        _with_casts(body, 6, 1, c_parts),
        grid=grid,
        in_specs=[
            win.spec((1, chunk, heads * dk), BF, lambda b, h, n: (b, n, h)),
            win.spec((1, chunk, heads * dk), BF, lambda b, h, n: (b, n, k_blocks + h)),
            win.spec((1, chunk, heads * dv), BF, lambda b, h, n: (b, n, v_blocks + h)),
            win.spec((1, chunk, heads * dv), BF, lambda b, h, n: (b, n, r_blocks + h)),
            win.spec((1, chunk, heads * dk), F32, lambda b, h, n: (b, n, h)),
            win.spec((1, dv), F32, lambda b, h, n: (0, 0)),
        ] + c_in,
        out_specs=[win.spec((1, chunk, heads * dv), BF, lambda b, h, n: (b, n, h))] + c_out,
        out_shape=[jax.ShapeDtypeStruct((bsz, s, GLA_HEADS * dv), BF)] + c_shapes,
        scratch_shapes=_vmem(scratch),
        compiler_params=win.params(("arbitrary", "arbitrary", "arbitrary"), scratch, temps),
        name="gla_core",
    )(proj, proj, proj, proj, la, gw, *c_ops)
    return outs[0], outs[1:]


def _proj_res_kernel(x_ref, w_ref, res_ref, o_ref):
    o_ref[...] = res_ref[...] + _dot(x_ref[...], w_ref[...])


def _proj_res(x, w, res, *, tm):
    t, kdim = x.shape
    n = w.shape[1]
    win = _Windows()
    return pl.pallas_call(
        _proj_res_kernel,
        grid=(t // tm,),
        in_specs=[
            win.spec((tm, kdim), BF, lambda i: (i, 0)),
            win.spec((kdim, n), BF, lambda i: (0, 0)),
            win.spec((tm, n), F32, lambda i: (i, 0)),
        ],
        out_specs=win.spec((tm, n), F32, lambda i: (i, 0)),
        out_shape=jax.ShapeDtypeStruct((t, n), F32),
        compiler_params=win.params(("parallel",), temps=_nbytes((tm, n), F32)),
        name="proj_res",
    )(x, w, res)


def _ffn_kernel(h_ref, nw_ref, wg_ref, wu_ref, wd_ref, fw_ref, o_ref, xn_ref, *, final_norm):
    f = pl.program_id(1)

    @pl.when(f == 0)
    def _():
        h = h_ref[...]
        xn_ref[...] = _rms(h, nw_ref[...]).astype(BF)
        o_ref[...] = h

    xn = xn_ref[...]
    act = (_silu(_dot(xn, wg_ref[...])) * _dot(xn, wu_ref[...])).astype(BF)
    o_ref[...] += _dot(act, wd_ref[...])

    if final_norm:
        @pl.when(f == pl.num_programs(1) - 1)
        def _():
            o_ref[...] = _rms(o_ref[...], fw_ref[...])


def _ffn(h, nw, w_gate_up, w_down, fw, *, tm, tf, final_norm):
    t, d = h.shape
    d_ff = w_down.shape[0]
    nf = d_ff // tf
    win = _Windows()
    scratch = [((tm, d), BF)]
    temps = 2 * _nbytes((tm, tf), F32) + _nbytes((tm, tf), BF)
    return pl.pallas_call(
        functools.partial(_ffn_kernel, final_norm=final_norm),
        grid=(t // tm, nf),
        in_specs=[
            win.spec((tm, d), F32, lambda i, f: (i, 0)),
            win.spec((1, d), F32, lambda i, f: (0, 0)),
            win.spec((d, tf), BF, lambda i, f: (0, f)),
            win.spec((d, tf), BF, lambda i, f: (0, nf + f)),
            win.spec((tf, d), BF, lambda i, f: (f, 0)),
            win.spec((1, d), F32, lambda i, f: (0, 0)),
        ],
        out_specs=win.spec((tm, d), F32, lambda i, f: (i, 0)),
        out_shape=jax.ShapeDtypeStruct((t, d), F32),
        scratch_shapes=_vmem(scratch),
        compiler_params=win.params(("parallel", "arbitrary"), scratch, temps),
        name="ffn",
    )(h, nw, w_gate_up, w_gate_up, w_down, fw)


_EXP2_UNDERFLOW = -127.0
LOG2_E = 1.4426950408889634


def _sb_kernel(q_ref, k_ref, v_ref, o_ref, *, tq, group):
    s = q_ref.shape[1]
    tw = 2 * tq
    def later(n):
        r = lax.broadcasted_iota(jnp.int32, (n, n), 0)
        c = lax.broadcasted_iota(jnp.int32, (n, n), 1)
        return jnp.where(r > c, 1.0, 0.0).astype(BF)

    def causal(n_keys, offset):
        qr = lax.broadcasted_iota(jnp.int32, (tq, n_keys), 0)
        kc = lax.broadcasted_iota(jnp.int32, (tq, n_keys), 1)
        return kc < qr + offset

    later_w, later_b = later(tw), later(tq)
    diag_mask = causal(tq, 0)
    win_mask = causal(tw, tq)

    def scores(q, start, width, mask):
        kb = k_ref[0, pl.ds(start, width), :]
        z = lax.dot_general(q, kb, _NT, preferred_element_type=F32)
        nz = -z
        lf = jnp.minimum(nz, 0.0) - jnp.log2(1.0 + jnp.exp2(jnp.minimum(z, nz)))
        if mask is not None:
            lf = jnp.where(mask, lf, 0.0)
        return z, lf

    def suffix(lf, later, carry):
        after = _dot(lf.astype(BF), later)
        if carry is not None:
            after = after + carry
        return after, after[:, :1] + lf[:, :1]

    def weighted(z, lf, after, start, width, mask):
        a = jnp.exp2(z + lf + after)
        if mask is not None:
            a = jnp.where(mask, a, 0.0)
        return _dot(a.astype(BF), v_ref[0, pl.ds(start, width), :])

    def window(tile):
        if isinstance(tile, int):
            if tile == 0:
                return 0, tq, later_b, diag_mask
            return (tile - 1) * tq, tw, later_w, win_mask
        return pl.multiple_of((tile - 1) * tq, tq), tw, later_w, win_mask

    def row0(tile):
        return tile * tq if isinstance(tile, int) else pl.multiple_of(tile * tq, tq)

    def extend(tile, q, carry, top, acc):
        def cond(st):
            return jnp.logical_and(st[0] >= 0, st[1] > _EXP2_UNDERFLOW)

        def body(st):
            j, _, carry, acc = st
            start = pl.multiple_of(j * tq, tq)
            z, lf = scores(q, start, tq, None)
            after, carry = suffix(lf, later_b, carry)
            acc = acc + weighted(z, lf, after, start, tq, None)
            return j - 1, jnp.max(carry), carry, acc

        init = (jnp.asarray(tile - 2, jnp.int32), top, carry, acc)
        return lax.while_loop(cond, body, init)[3]

    def run_group(base):
        tiles = [base + g for g in range(group)]
        wins = [window(t) for t in tiles]
        qs = [q_ref[0, pl.ds(row0(t), tq), :] for t in tiles]
        zl = [scores(q, w[0], w[1], w[3]) for q, w in zip(qs, wins)]
        ac = [suffix(lf, w[2], None) for (_, lf), w in zip(zl, wins)]
        accs = [weighted(z, lf, after, w[0], w[1], w[3]) for (z, lf), (after, _), w in zip(zl, ac, wins)]
        tops = [jnp.max(carry) for _, carry in ac]
        for t, q, (_, carry), top, acc in zip(tiles, qs, ac, tops, accs):
            acc = extend(t, q, carry, top, acc)
            o_ref[0, pl.ds(row0(t), tq), :] = acc.astype(o_ref.dtype)

    run_group(0)

    def loop_body(it, carry):
        run_group(it * group)
        return carry

    lax.fori_loop(1, s // (tq * group), loop_body, 0)


def _sb_attention(qkv, cast_jobs, *, tq, group):
    bsz, s, w3 = qkv.shape
    w = w3 // 3
    hd = w // SB_HEADS
    grid = (bsz, SB_HEADS)
    win = _Windows()
    c_ops, c_in, c_out, c_shapes, c_parts = _cast_specs(win, cast_jobs, grid)
    temps = group * 8 * _nbytes((tq, 2 * tq), F32)
    outs = pl.pallas_call(
        _with_casts(functools.partial(_sb_kernel, tq=tq, group=group), 3, 1, c_parts),
        grid=grid,
        in_specs=[
            win.spec((1, s, hd), BF, lambda b, h: (b, 0, h)),
            win.spec((1, s, hd), BF, lambda b, h: (b, 0, SB_HEADS + h)),
            win.spec((1, s, hd), BF, lambda b, h: (b, 0, 2 * SB_HEADS + h)),
        ] + c_in,
        out_specs=[win.spec((1, s, hd), BF, lambda b, h: (b, 0, h))] + c_out,
        out_shape=[jax.ShapeDtypeStruct((bsz, s, w), BF)] + c_shapes,
        compiler_params=win.params(("arbitrary", "arbitrary"), temps=temps),
        name="sb_attn",
    )(qkv, qkv, qkv, *c_ops)
    return outs[0], outs[1:]


def kernel(x, attn_norm_w, ffn_norm_w, gla_w_in, gla_w_gate_up, gla_b_gate, gla_gnorm_w, gla_w_out,
           kv_norm_w, sb_w_kv, sb_w_q, sb_w_out, ffn_w_gate_up, ffn_w_down, final_norm_w):
    bsz, s, d = x.shape
    t = bsz * s
    depth = attn_norm_w.shape[0]
    n_gla = gla_w_in.shape[0]
    rank = gla_w_gate_up.shape[1]
    main_w = gla_w_in.shape[2] - rank
    hd = d // SB_HEADS
    row = lambda v: v.reshape(1, -1)
    assert n_gla >= 1 and depth - n_gla == 1

    w_in_t = jnp.swapaxes(gla_w_in, 1, 2)

    h = x.reshape(t, d)
    for layer in range(depth):
        casts = [[(ffn_w_gate_up, layer)], [(ffn_w_down, layer)]]
        if layer < n_gla:
            proj, la = _gla_inproj(h, row(attn_norm_w[layer]), w_in_t, layer, main_w, rank,
                                   gla_w_gate_up[layer], row(gla_b_gate[layer]), **TILES["inproj"])
            casts.append([(gla_w_out, layer)])
            if layer == n_gla - 1:
                casts.append([(sb_w_q, 0), (sb_w_kv, None)])
            o, cast = _gla_core(proj.reshape(bsz, s, main_w), la.reshape(bsz, s, -1),
                                row(gla_gnorm_w[layer]), casts, **TILES["gla"])
            if layer == n_gla - 1:
                w_qkv = cast[3]
        else:
            j = layer - n_gla
            qkv = _norm_matmul2(h, row(attn_norm_w[layer]), row(kv_norm_w), w_qkv, sb_w_q.shape[2],
                                scale_a=hd ** -0.5 * LOG2_E, **TILES["qkv"])
            casts.append([(sb_w_out, j)])
            o, cast = _sb_attention(qkv.reshape(bsz, s, -1), casts, **TILES["sb"])
        w_gate_up, w_down, w_out = cast[:3]
        h = _proj_res(o.reshape(t, -1), w_out, h, **TILES["proj"])
        h = _ffn(h, row(ffn_norm_w[layer]), w_gate_up, w_down, row(final_norm_w),
                 final_norm=layer == depth - 1, **TILES["ffn"])
    return h.reshape(bsz, s, d)
```

```python
import functools

import jax
import jax.numpy as jnp
from jax import lax
from jax.experimental import pallas as pl
from jax.experimental.pallas import tpu as pltpu

EPS = 1e-6
GLA_HEADS = 4
GLA_GATE_TAU = 16.0
SB_HEADS = 16
BF = jnp.bfloat16
F32 = jnp.float32
LANES = 128
BF16_SUBLANES = 16
MIB = 2 ** 20
V7X_VMEM_BYTES = 64 * MIB
VMEM_HEADROOM = 4 * MIB
VMEM_SPILL_ALLOWANCE = 2 * MIB

TILES = dict(
    inproj=dict(tm=1024, tn=768),
    gla=dict(chunk=128, sub=8, heads=GLA_HEADS),
    proj=dict(tm=512),
    ffn=dict(tm=1024, tf=512),
    qkv=dict(tm=1024, tn=1024),
    sb=dict(tq=128, group=8),
)

_NT = (((1,), (1,)), ((), ()))
_TN = (((0,), (0,)), ((), ()))


def _nbytes(shape, dtype):
    n = jnp.dtype(dtype).itemsize
    for dim in shape:
        n *= dim or 1
    return n


class _Windows:
    def __init__(self):
        self.bytes = 0

    def spec(self, block, dtype, index_map):
        self.bytes += _nbytes(block, dtype)
        return pl.BlockSpec(block, index_map)

    def params(self, sem, scratch=(), temps=0):
        need = 2 * self.bytes + sum(_nbytes(shape, dtype) for shape, dtype in scratch) + temps
        need = -(-(need + VMEM_SPILL_ALLOWANCE) // MIB) * MIB
        assert need <= V7X_VMEM_BYTES - VMEM_HEADROOM, need
        return pltpu.CompilerParams(dimension_semantics=sem, vmem_limit_bytes=need)


def _vmem(scratch):
    return [pltpu.VMEM(shape, dtype) for shape, dtype in scratch]


def _dot(a, b):
    return jnp.dot(a, b, preferred_element_type=F32)


def _rms(x, w):
    return x * lax.rsqrt(jnp.mean(x * x, axis=-1, keepdims=True) + EPS) * w


def _split2(x):
    hi = x.astype(BF)
    lo = (x - hi.astype(F32)).astype(BF)
    return hi, lo


def _log_sigmoid(x):
    return jnp.minimum(x, 0.0) - jnp.log(1.0 + jnp.exp(-jnp.abs(x)))


def _silu(x):
    return x * (1.0 / (1.0 + jnp.exp(-x)))


def _cast_specs(win, jobs, grid):
    steps = 1
    for g in grid:
        steps *= g

    def linear(idx):
        s = 0
        for g, i in zip(grid, idx):
            s = s * g + i
        return s

    operands, in_specs, out_specs, out_shapes, parts = [], [], [], [], []
    for job in jobs:
        n_rows = job[0][0].shape[-2]
        n_blk = max(n for n in range(1, steps + 1)
                    if n_rows % n == 0 and (n_rows // n) % BF16_SUBLANES == 0)
        rows = n_rows // n_blk
        blk = lambda *idx, n_blk=n_blk: linear(idx) * n_blk // steps
        for arr, lead in job:
            assert arr.shape[-2] == n_rows
            operands.append(arr)
            if lead is None:
                in_specs.append(win.spec((rows, arr.shape[-1]), F32, lambda *idx, blk=blk: (blk(*idx), 0)))
            else:
                in_specs.append(win.spec((None, rows, arr.shape[-1]), F32,
                                         lambda *idx, blk=blk, lead=lead: (lead, blk(*idx), 0)))
        width = sum(arr.shape[-1] for arr, _ in job)
        out_specs.append(win.spec((rows, width), BF, lambda *idx, blk=blk: (blk(*idx), 0)))
        out_shapes.append(jax.ShapeDtypeStruct((n_rows, width), BF))
        parts.append(len(job))
    return operands, in_specs, out_specs, out_shapes, parts


def _with_casts(body, n_in, n_out, parts):
    n_src = sum(parts)

    def kernel(*refs):
        ins, rest = refs[:n_in], refs[n_in:]
        srcs, rest = rest[:n_src], rest[n_src:]
        outs, rest = rest[:n_out], rest[n_out:]
        dsts, scratch = rest[:len(parts)], rest[len(parts):]
        body(*ins, *outs, *scratch)
        srcs = list(srcs)
        for dst, n_parts in zip(dsts, parts):
            col = 0
            for _ in range(n_parts):
                src = srcs.pop(0)
                dst[:, col:col + src.shape[-1]] = src[...].astype(BF)
                col += src.shape[-1]

    return kernel


def _norm_matmul2_kernel(x_ref, nwa_ref, nwb_ref, w_ref, o_ref, xn_ref, *, na, scale_a):
    j = pl.program_id(1)

    @pl.when(j == 0)
    def _():
        x = x_ref[...]
        xhat = x * lax.rsqrt(jnp.mean(x * x, axis=-1, keepdims=True) + EPS)
        xa = (xhat * nwa_ref[...]).astype(BF)
        xn_ref[0] = xa
        xn_ref[1] = (xhat * nwb_ref[...]).astype(BF)
        o_ref[...] = (_dot(xa, w_ref[...]) * scale_a).astype(o_ref.dtype)

    @pl.when((j > 0) & (j < na))
    def _():
        o_ref[...] = (_dot(xn_ref[0], w_ref[...]) * scale_a).astype(o_ref.dtype)

    @pl.when(j >= na)
    def _():
        o_ref[...] = _dot(xn_ref[1], w_ref[...]).astype(o_ref.dtype)


def _norm_matmul2(x, nwa, nwb, w, n_a, *, tm, tn, scale_a):
    t, d = x.shape
    n = w.shape[1]
    win = _Windows()
    scratch = [((2, tm, d), BF)]
    temps = 2 * _nbytes((tm, d), F32) + _nbytes((tm, tn), F32)
    return pl.pallas_call(
        functools.partial(_norm_matmul2_kernel, na=n_a // tn, scale_a=scale_a),
        grid=(t // tm, n // tn),
        in_specs=[
            win.spec((tm, d), F32, lambda i, j: (i, 0)),
            win.spec((1, d), F32, lambda i, j: (0, 0)),
            win.spec((1, d), F32, lambda i, j: (0, 0)),
            win.spec((d, tn), BF, lambda i, j: (0, j)),
        ],
        out_specs=win.spec((tm, tn), BF, lambda i, j: (i, j)),
        out_shape=jax.ShapeDtypeStruct((t, n), BF),
        scratch_shapes=_vmem(scratch),
        compiler_params=win.params(("parallel", "arbitrary"), scratch, temps),
        name="norm_matmul2",
    )(x, nwa, nwb, w)


def _gla_inproj_kernel(x_ref, nw_ref, w_ref, wgl_ref, wup_ref, bg_ref, o_ref, la_ref, xn_ref, gl_ref, *, rank):
    def dot_t(a, w_t):
        return lax.dot_general(a, w_t, _NT, preferred_element_type=F32)

    def gate(gl):
        up_hi, up_lo = _split2(wup_ref[...])
        group = lax.broadcasted_iota(jnp.int32, up_hi.shape, 0) // rank
        g = _dot(gl, jnp.where(group == 2, up_lo, up_hi)) + bg_ref[...]
        la_ref[...] = _log_sigmoid(g) * (1.0 / GLA_GATE_TAU)

    @pl.when(pl.program_id(1) == 0)
    def _():
        xn = _rms(x_ref[...], nw_ref[...]).astype(BF)
        xn_ref[...] = xn
        o_ref[...] = dot_t(xn, w_ref[...].astype(BF)).astype(o_ref.dtype)
        gl_hi, gl_lo = _split2(dot_t(xn, wgl_ref[...]))
        group = lax.broadcasted_iota(jnp.int32, gl_hi.shape, 1) // rank
        gl = jnp.where(group == 1, gl_lo, gl_hi)
        gl_ref[...] = gl
        gate(gl)

    @pl.when(pl.program_id(1) > 0)
    def _():
        gate(gl_ref[...])
        o_ref[...] = dot_t(xn_ref[...], w_ref[...].astype(BF)).astype(o_ref.dtype)


def _gla_inproj(x, nw, w_t_all, layer, n, rank, w_gate_up, bg, *, tm, tn):
    t, d = x.shape
    steps = n // tn
    qk_w = w_gate_up.shape[1]
    slab = qk_w // steps
    assert slab * steps == qk_w and slab % LANES == 0 and 3 * rank <= LANES
    three = lambda a: jnp.pad(jnp.tile(a, (3, 1)), ((0, LANES - 3 * rank), (0, 0)))
    wgl_t = three(w_t_all[layer, n:n + rank]).astype(BF)
    wup = three(w_gate_up)
    win = _Windows()
    scratch = [((tm, d), BF), ((tm, LANES), BF)]
    temps = _nbytes((tm, tn), F32) + _nbytes((tn, d), BF)
    return pl.pallas_call(
        functools.partial(_gla_inproj_kernel, rank=rank),
        grid=(t // tm, steps),
        in_specs=[
            win.spec((tm, d), F32, lambda i, j: (i, 0)),
            win.spec((1, d), F32, lambda i, j: (0, 0)),
            win.spec((None, tn, d), F32, lambda i, j: (layer, j, 0)),
            win.spec((LANES, d), BF, lambda i, j: (0, 0)),
            win.spec((LANES, slab), F32, lambda i, j: (0, j)),
            win.spec((1, slab), F32, lambda i, j: (0, j)),
        ],
        out_specs=[
            win.spec((tm, tn), BF, lambda i, j: (i, j)),
            win.spec((tm, slab), F32, lambda i, j: (i, j)),
        ],
        out_shape=[
            jax.ShapeDtypeStruct((t, n), BF),
            jax.ShapeDtypeStruct((t, qk_w), F32),
        ],
        scratch_shapes=_vmem(scratch),
        compiler_params=win.params(("parallel", "arbitrary"), scratch, temps),
        name="gla_inproj",
    )(x, nw, w_t_all, wgl_t, wup, bg)


def _gla_kernel(q_ref, k_ref, v_ref, r_ref, g_ref, gw_ref, o_ref, st_ref, qh_ref, kh_ref,
                *, chunk, sub, heads, scale):
    n_sub = chunk // sub
    dk = q_ref.shape[-1] // heads
    dv = v_ref.shape[-1] // heads
    hs = range(heads)

    @pl.when(pl.program_id(2) == 0)
    def _():
        st_ref[...] = jnp.zeros_like(st_ref)
        qh_ref[...] = jnp.zeros_like(qh_ref)
        kh_ref[...] = jnp.zeros_like(kh_ref)

    def head_cols(ref, h, width):
        return ref[0, :, h * width:(h + 1) * width]

    q = [head_cols(q_ref, h, dk).astype(F32) * scale for h in hs]
    k = [head_cols(k_ref, h, dk).astype(F32) for h in hs]
    g = [head_cols(g_ref, h, dk) for h in hs]

    row = lax.broadcasted_iota(jnp.int32, (chunk, chunk), 0)
    col = lax.broadcasted_iota(jnp.int32, (chunk, chunk), 1)
    tri = jnp.where(col <= row, 1.0, 0.0).astype(BF)
    diff = jnp.where(row // sub == col // sub, row - col, -1)

    b = []
    for h in hs:
        g_hi, g_lo = _split2(g[h])
        b.append(_dot(tri, g_hi) + _dot(tri, g_lo))

    st = [st_ref[h] for h in hs]
    o = [lax.dot_general((q[h] * jnp.exp(b[h])).astype(BF), st[h].astype(BF), _NT,
                         preferred_element_type=F32) for h in hs]

    def store_rows(ref, h, start, cols, x):
        lo = start // BF16_SUBLANES * BF16_SUBLANES
        hi = -(-(start + x.shape[0]) // BF16_SUBLANES) * BF16_SUBLANES
        pieces = [jnp.zeros((start - lo, x.shape[1]), F32)] if start > lo else []
        pieces.append(x)
        if hi > start + x.shape[0]:
            pieces.append(jnp.zeros((hi - start - x.shape[0], x.shape[1]), F32))
        x = jnp.concatenate(pieces, axis=0) if len(pieces) > 1 else x
        ref[h, lo:hi, cols] = x.astype(BF)

    scores = []
    for h in hs:
        k_run = None
        for i in range(1, n_sub):
            beta = b[h][i * sub - 1:i * sub, :]
            rows = slice(i * sub, (i + 1) * sub)
            prev = slice((i - 1) * sub, i * sub)
            cols = slice((i - 1) * dk, i * dk)
            store_rows(qh_ref, h, i * sub, cols, q[h][rows] * jnp.exp(b[h][rows] - beta))
            k_new = k[h][prev] * jnp.exp(beta - b[h][prev])
            if k_run is None:
                k_run = k_new
            else:
                k_run = jnp.concatenate([k_run * jnp.exp(beta - beta_prev), k_new], axis=0)
            store_rows(kh_ref, h, 0, cols, k_run)
            beta_prev = beta
        scores.append(lax.dot_general(qh_ref[h], kh_ref[h], _NT, preferred_element_type=F32))

    for h in hs:
        decay = jnp.exp(g[h])
        w = k[h]
        for d in range(sub):
            if d > 0:
                w = decay * pltpu.roll(w, 1, 0)
            scores[h] = jnp.where(diff == d, jnp.sum(q[h] * w, axis=-1, keepdims=True), scores[h])

    v = [head_cols(v_ref, h, dv) for h in hs]
    o = [o[h] + _dot(scores[h].astype(BF), v[h]) for h in hs]

    for h in hs:
        b_last = b[h][chunk - 1:chunk, :]
        k_dec = (k[h] * jnp.exp(b_last - b[h])).astype(BF)
        st_ref[h] = st[h] * jnp.exp(b_last) + lax.dot_general(v[h], k_dec, _TN, preferred_element_type=F32)

    for h in hs:
        r = head_cols(r_ref, h, dv).astype(F32)
        o_ref[0, :, h * dv:(h + 1) * dv] = (_rms(o[h], gw_ref[...]) * _silu(r)).astype(o_ref.dtype)


def _gla_core(proj, la, gw, cast_jobs, *, chunk, sub, heads):
    bsz, s, _ = proj.shape
    dk = la.shape[-1] // GLA_HEADS
    dv = gw.shape[-1]
    groups = GLA_HEADS // heads
    k_blocks = groups
    v_blocks = 2 * GLA_HEADS * dk // (heads * dv)
    r_blocks = v_blocks + groups
    expanded = (chunk // sub - 1) * dk
    grid = (bsz, groups, s // chunk)
    win = _Windows()
    c_ops, c_in, c_out, c_shapes, c_parts = _cast_specs(win, cast_jobs, grid)
    scratch = [((heads, dv, dk), F32), ((heads, chunk, expanded), BF), ((heads, chunk, expanded), BF)]
    temps = heads * (6 * _nbytes((chunk, dk), F32) + 2 * _nbytes((chunk, dv), F32))
    body = functools.partial(_gla_kernel, chunk=chunk, sub=sub, heads=heads, scale=dk ** -0.5)
    outs = pl.pallas_call(
        _with_casts(body, 6, 1, c_parts),
        grid=grid,
        in_specs=[
            win.spec((1, chunk, heads * dk), BF, lambda b, h, n: (b, n, h)),
            win.spec((1, chunk, heads * dk), BF, lambda b, h, n: (b, n, k_blocks + h)),
            win.spec((1, chunk, heads * dv), BF, lambda b, h, n: (b, n, v_blocks + h)),
            win.spec((1, chunk, heads * dv), BF, lambda b, h, n: (b, n, r_blocks + h)),
            win.spec((1, chunk, heads * dk), F32, lambda b, h, n: (b, n, h)),
            win.spec((1, dv), F32, lambda b, h, n: (0, 0)),
        ] + c_in,
        out_specs=[win.spec((1, chunk, heads * dv), BF, lambda b, h, n: (b, n, h))] + c_out,
        out_shape=[jax.ShapeDtypeStruct((bsz, s, GLA_HEADS * dv), BF)] + c_shapes,
        scratch_shapes=_vmem(scratch),
        compiler_params=win.params(("arbitrary", "arbitrary", "arbitrary"), scratch, temps),
        name="gla_core",
    )(proj, proj, proj, proj, la, gw, *c_ops)
    return outs[0], outs[1:]


def _proj_res_kernel(x_ref, w_ref, res_ref, o_ref):
    o_ref[...] = res_ref[...] + _dot(x_ref[...], w_ref[...])


def _proj_res(x, w, res, *, tm):
    t, kdim = x.shape
    n = w.shape[1]
    win = _Windows()
    return pl.pallas_call(
        _proj_res_kernel,
        grid=(t // tm,),
        in_specs=[
            win.spec((tm, kdim), BF, lambda i: (i, 0)),
            win.spec((kdim, n), BF, lambda i: (0, 0)),
            win.spec((tm, n), F32, lambda i: (i, 0)),
        ],
        out_specs=win.spec((tm, n), F32, lambda i: (i, 0)),
        out_shape=jax.ShapeDtypeStruct((t, n), F32),
        compiler_params=win.params(("parallel",), temps=_nbytes((tm, n), F32)),
        name="proj_res",
    )(x, w, res)


def _ffn_kernel(h_ref, nw_ref, wg_ref, wu_ref, wd_ref, fw_ref, o_ref, xn_ref, *, final_norm):
    f = pl.program_id(1)

    def step(acc_ref):
        xn = xn_ref[...]
        act = (_silu(_dot(xn, wg_ref[...])) * _dot(xn, wu_ref[...])).astype(BF)
        o_ref[...] = acc_ref[...] + _dot(act, wd_ref[...])

    @pl.when(f == 0)
    def _():
        xn_ref[...] = _rms(h_ref[...], nw_ref[...]).astype(BF)
        step(h_ref)

    @pl.when(f > 0)
    def _():
        step(o_ref)

    if final_norm:
        @pl.when(f == pl.num_programs(1) - 1)
        def _():
            o_ref[...] = _rms(o_ref[...], fw_ref[...])


def _ffn(h, nw, w_gate_up, w_down, fw, *, tm, tf, final_norm):
    t, d = h.shape
    d_ff = w_down.shape[0]
    nf = d_ff // tf
    win = _Windows()
    scratch = [((tm, d), BF)]
    temps = 2 * _nbytes((tm, tf), F32) + _nbytes((tm, tf), BF)
    return pl.pallas_call(
        functools.partial(_ffn_kernel, final_norm=final_norm),
        grid=(t // tm, nf),
        in_specs=[
            win.spec((tm, d), F32, lambda i, f: (i, 0)),
            win.spec((1, d), F32, lambda i, f: (0, 0)),
            win.spec((d, tf), BF, lambda i, f: (0, f)),
            win.spec((d, tf), BF, lambda i, f: (0, nf + f)),
            win.spec((tf, d), BF, lambda i, f: (f, 0)),
            win.spec((1, d), F32, lambda i, f: (0, 0)),
        ],
        out_specs=win.spec((tm, d), F32, lambda i, f: (i, 0)),
        out_shape=jax.ShapeDtypeStruct((t, d), F32),
        scratch_shapes=_vmem(scratch),
        compiler_params=win.params(("parallel", "arbitrary"), scratch, temps),
        name="ffn",
    )(h, nw, w_gate_up, w_gate_up, w_down, fw)


_EXP2_UNDERFLOW = -127.0
LOG2_E = 1.4426950408889634


def _sb_kernel(q_ref, k_ref, v_ref, o_ref, *, tq, group):
    s = q_ref.shape[1]
    tw = 2 * tq
    def later(n):
        r = lax.broadcasted_iota(jnp.int32, (n, n), 0)
        c = lax.broadcasted_iota(jnp.int32, (n, n), 1)
        return jnp.where(r > c, 1.0, 0.0).astype(BF)

    def causal(n_keys, offset):
        qr = lax.broadcasted_iota(jnp.int32, (tq, n_keys), 0)
        kc = lax.broadcasted_iota(jnp.int32, (tq, n_keys), 1)
        return kc < qr + offset

    later_w, later_b = later(tw), later(tq)
    diag_mask = causal(tq, 0)
    win_mask = causal(tw, tq)

    def scores(q, start, width, mask):
        kb = k_ref[0, pl.ds(start, width), :]
        z = lax.dot_general(q, kb, _NT, preferred_element_type=F32)
        nz = -z
        lf = jnp.minimum(nz, 0.0) - jnp.log2(1.0 + jnp.exp2(jnp.minimum(z, nz)))
        if mask is not None:
            lf = jnp.where(mask, lf, 0.0)
        return z, lf

    def suffix(lf, later, carry):
        after = _dot(lf.astype(BF), later)
        if carry is not None:
            after = after + carry
        return after, after[:, :1] + lf[:, :1]

    def weighted(z, lf, after, start, width, mask):
        a = jnp.exp2(z + lf + after)
        if mask is not None:
            a = jnp.where(mask, a, 0.0)
        return _dot(a.astype(BF), v_ref[0, pl.ds(start, width), :])

    def window(tile):
        if isinstance(tile, int):
            if tile == 0:
                return 0, tq, later_b, diag_mask
            return (tile - 1) * tq, tw, later_w, win_mask
        return pl.multiple_of((tile - 1) * tq, tq), tw, later_w, win_mask

    def row0(tile):
        return tile * tq if isinstance(tile, int) else pl.multiple_of(tile * tq, tq)

    def extend(tile, q, carry, top, acc):
        def cond(st):
            return jnp.logical_and(st[0] >= 0, st[1] > _EXP2_UNDERFLOW)

        def body(st):
            j, _, carry, acc = st
            start = pl.multiple_of(j * tq, tq)
            z, lf = scores(q, start, tq, None)
            after, carry = suffix(lf, later_b, carry)
            acc = acc + weighted(z, lf, after, start, tq, None)
            return j - 1, jnp.max(carry), carry, acc

        init = (jnp.asarray(tile - 2, jnp.int32), top, carry, acc)
        return lax.while_loop(cond, body, init)[3]

    def run_group(base):
        tiles = [base + g for g in range(group)]
        wins = [window(t) for t in tiles]
        qs = [q_ref[0, pl.ds(row0(t), tq), :] for t in tiles]
        zl = [scores(q, w[0], w[1], w[3]) for q, w in zip(qs, wins)]
        ac = [suffix(lf, w[2], None) for (_, lf), w in zip(zl, wins)]
        accs = [weighted(z, lf, after, w[0], w[1], w[3]) for (z, lf), (after, _), w in zip(zl, ac, wins)]
        tops = [jnp.max(carry) for _, carry in ac]
        for t, q, (_, carry), top, acc in zip(tiles, qs, ac, tops, accs):
            acc = extend(t, q, carry, top, acc)
            o_ref[0, pl.ds(row0(t), tq), :] = acc.astype(o_ref.dtype)

    run_group(0)

    def loop_body(it, carry):
        run_group(it * group)
        return carry

    lax.fori_loop(1, s // (tq * group), loop_body, 0)


def _sb_attention(qkv, cast_jobs, *, tq, group):
    bsz, s, w3 = qkv.shape
    w = w3 // 3
    hd = w // SB_HEADS
    grid = (bsz, SB_HEADS)
    win = _Windows()
    c_ops, c_in, c_out, c_shapes, c_parts = _cast_specs(win, cast_jobs, grid)
    temps = group * 8 * _nbytes((tq, 2 * tq), F32)
    outs = pl.pallas_call(
        _with_casts(functools.partial(_sb_kernel, tq=tq, group=group), 3, 1, c_parts),
        grid=grid,
        in_specs=[
            win.spec((1, s, hd), BF, lambda b, h: (b, 0, h)),
            win.spec((1, s, hd), BF, lambda b, h: (b, 0, SB_HEADS + h)),
            win.spec((1, s, hd), BF, lambda b, h: (b, 0, 2 * SB_HEADS + h)),
        ] + c_in,
        out_specs=[win.spec((1, s, hd), BF, lambda b, h: (b, 0, h))] + c_out,
        out_shape=[jax.ShapeDtypeStruct((bsz, s, w), BF)] + c_shapes,
        compiler_params=win.params(("arbitrary", "arbitrary"), temps=temps),
        name="sb_attn",
    )(qkv, qkv, qkv, *c_ops)
    return outs[0], outs[1:]


def kernel(x, attn_norm_w, ffn_norm_w, gla_w_in, gla_w_gate_up, gla_b_gate, gla_gnorm_w, gla_w_out,
           kv_norm_w, sb_w_kv, sb_w_q, sb_w_out, ffn_w_gate_up, ffn_w_down, final_norm_w):
    bsz, s, d = x.shape
    t = bsz * s
    depth = attn_norm_w.shape[0]
    n_gla = gla_w_in.shape[0]
    rank = gla_w_gate_up.shape[1]
    main_w = gla_w_in.shape[2] - rank
    hd = d // SB_HEADS
    row = lambda v: v.reshape(1, -1)
    assert n_gla >= 1 and depth - n_gla == 1

    w_in_t = jnp.swapaxes(gla_w_in, 1, 2)

    h = x.reshape(t, d)
    for layer in range(depth):
        casts = [[(ffn_w_gate_up, layer)], [(ffn_w_down, layer)]]
        if layer < n_gla:
            proj, la = _gla_inproj(h, row(attn_norm_w[layer]), w_in_t, layer, main_w, rank,
                                   gla_w_gate_up[layer], row(gla_b_gate[layer]), **TILES["inproj"])
            casts.append([(gla_w_out, layer)])
            if layer == n_gla - 1:
                casts.append([(sb_w_q, 0), (sb_w_kv, None)])
            o, cast = _gla_core(proj.reshape(bsz, s, main_w), la.reshape(bsz, s, -1),
                                row(gla_gnorm_w[layer]), casts, **TILES["gla"])
            if layer == n_gla - 1:
                w_qkv = cast[3]
        else:
            j = layer - n_gla
            qkv = _norm_matmul2(h, row(attn_norm_w[layer]), row(kv_norm_w), w_qkv, sb_w_q.shape[2],
                                scale_a=hd ** -0.5 * LOG2_E, **TILES["qkv"])
            casts.append([(sb_w_out, j)])
            o, cast = _sb_attention(qkv.reshape(bsz, s, -1), casts, **TILES["sb"])
        w_gate_up, w_down, w_out = cast[:3]
        h = _proj_res(o.reshape(t, -1), w_out, h, **TILES["proj"])
        h = _ffn(h, row(ffn_norm_w[layer]), w_gate_up, w_down, row(final_norm_w),
                 final_norm=layer == depth - 1, **TILES["ffn"])
    return h.reshape(bsz, s, d)
```

```python
import functools

import jax
import jax.numpy as jnp
from jax import lax
from jax.experimental import pallas as pl
from jax.experimental.pallas import tpu as pltpu

EPS = 1e-6
GLA_HEADS = 4
GLA_GATE_TAU = 16.0
SB_HEADS = 16
BF = jnp.bfloat16
F32 = jnp.float32
LANES = 128
BF16_SUBLANES = 16
MIB = 2 ** 20
V7X_VMEM_BYTES = 64 * MIB
VMEM_HEADROOM = 4 * MIB
VMEM_SPILL_ALLOWANCE = 2 * MIB

TILES = dict(
    inproj=dict(tm=1024, tn=768),
    gla=dict(chunk=128, sub=8, heads=GLA_HEADS, per_step=2),
    proj=dict(tm=512),
    ffn=dict(tm=1024, tf=512),
    qkv=dict(tm=1024, tn=1024),
    sb=dict(tq=128, group=8),
)

_NT = (((1,), (1,)), ((), ()))
_TN = (((0,), (0,)), ((), ()))


def _nbytes(shape, dtype):
    n = jnp.dtype(dtype).itemsize
    for dim in shape:
        n *= dim or 1
    return n


class _Windows:
    def __init__(self):
        self.bytes = 0

    def spec(self, block, dtype, index_map):
        self.bytes += _nbytes(block, dtype)
        return pl.BlockSpec(block, index_map)

    def params(self, sem, scratch=(), temps=0):
        need = 2 * self.bytes + sum(_nbytes(shape, dtype) for shape, dtype in scratch) + temps
        need = -(-(need + VMEM_SPILL_ALLOWANCE) // MIB) * MIB
        assert need <= V7X_VMEM_BYTES - VMEM_HEADROOM, need
        return pltpu.CompilerParams(dimension_semantics=sem, vmem_limit_bytes=need)


def _vmem(scratch):
    return [pltpu.VMEM(shape, dtype) for shape, dtype in scratch]


def _dot(a, b):
    return jnp.dot(a, b, preferred_element_type=F32)


def _rms(x, w):
    return x * lax.rsqrt(jnp.mean(x * x, axis=-1, keepdims=True) + EPS) * w


def _split2(x):
    hi = x.astype(BF)
    lo = (x - hi.astype(F32)).astype(BF)
    return hi, lo


def _log_sigmoid(x):
    return jnp.minimum(x, 0.0) - jnp.log(1.0 + jnp.exp(-jnp.abs(x)))


def _silu(x):
    return x * (1.0 / (1.0 + jnp.exp(-x)))


def _cast_specs(win, jobs, grid):
    steps = 1
    for g in grid:
        steps *= g

    def linear(idx):
        s = 0
        for g, i in zip(grid, idx):
            s = s * g + i
        return s

    operands, in_specs, out_specs, out_shapes, parts = [], [], [], [], []
    for job in jobs:
        n_rows = job[0][0].shape[-2]
        n_blk = max(n for n in range(1, steps + 1)
                    if n_rows % n == 0 and (n_rows // n) % BF16_SUBLANES == 0)
        rows = n_rows // n_blk
        blk = lambda *idx, n_blk=n_blk: linear(idx) * n_blk // steps
        for arr, lead in job:
            assert arr.shape[-2] == n_rows
            operands.append(arr)
            if lead is None:
                in_specs.append(win.spec((rows, arr.shape[-1]), F32, lambda *idx, blk=blk: (blk(*idx), 0)))
            else:
                in_specs.append(win.spec((None, rows, arr.shape[-1]), F32,
                                         lambda *idx, blk=blk, lead=lead: (lead, blk(*idx), 0)))
        width = sum(arr.shape[-1] for arr, _ in job)
        out_specs.append(win.spec((rows, width), BF, lambda *idx, blk=blk: (blk(*idx), 0)))
        out_shapes.append(jax.ShapeDtypeStruct((n_rows, width), BF))
        parts.append(len(job))
    return operands, in_specs, out_specs, out_shapes, parts


def _with_casts(body, n_in, n_out, parts):
    n_src = sum(parts)

    def kernel(*refs):
        ins, rest = refs[:n_in], refs[n_in:]
        srcs, rest = rest[:n_src], rest[n_src:]
        outs, rest = rest[:n_out], rest[n_out:]
        dsts, scratch = rest[:len(parts)], rest[len(parts):]
        body(*ins, *outs, *scratch)
        srcs = list(srcs)
        for dst, n_parts in zip(dsts, parts):
            col = 0
            for _ in range(n_parts):
                src = srcs.pop(0)
                dst[:, col:col + src.shape[-1]] = src[...].astype(BF)
                col += src.shape[-1]

    return kernel


def _norm_matmul2_kernel(x_ref, nwa_ref, nwb_ref, w_ref, o_ref, xn_ref, *, na, scale_a):
    j = pl.program_id(1)

    @pl.when(j == 0)
    def _():
        x = x_ref[...]
        xhat = x * lax.rsqrt(jnp.mean(x * x, axis=-1, keepdims=True) + EPS)
        xa = (xhat * nwa_ref[...]).astype(BF)
        xn_ref[0] = xa
        xn_ref[1] = (xhat * nwb_ref[...]).astype(BF)
        o_ref[...] = (_dot(xa, w_ref[...]) * scale_a).astype(o_ref.dtype)

    @pl.when((j > 0) & (j < na))
    def _():
        o_ref[...] = (_dot(xn_ref[0], w_ref[...]) * scale_a).astype(o_ref.dtype)

    @pl.when(j >= na)
    def _():
        o_ref[...] = _dot(xn_ref[1], w_ref[...]).astype(o_ref.dtype)


def _norm_matmul2(x, nwa, nwb, w, n_a, *, tm, tn, scale_a):
    t, d = x.shape
    n = w.shape[1]
    win = _Windows()
    scratch = [((2, tm, d), BF)]
    temps = 2 * _nbytes((tm, d), F32) + _nbytes((tm, tn), F32)
    return pl.pallas_call(
        functools.partial(_norm_matmul2_kernel, na=n_a // tn, scale_a=scale_a),
        grid=(t // tm, n // tn),
        in_specs=[
            win.spec((tm, d), F32, lambda i, j: (i, 0)),
            win.spec((1, d), F32, lambda i, j: (0, 0)),
            win.spec((1, d), F32, lambda i, j: (0, 0)),
            win.spec((d, tn), BF, lambda i, j: (0, j)),
        ],
        out_specs=win.spec((tm, tn), BF, lambda i, j: (i, j)),
        out_shape=jax.ShapeDtypeStruct((t, n), BF),
        scratch_shapes=_vmem(scratch),
        compiler_params=win.params(("parallel", "arbitrary"), scratch, temps),
        name="norm_matmul2",
    )(x, nwa, nwb, w)


def _gla_inproj_kernel(x_ref, nw_ref, w_ref, wgl_ref, wup_ref, bg_ref, o_ref, la_ref, xn_ref, gl_ref, *, rank):
    def dot_t(a, w_t):
        return lax.dot_general(a, w_t, _NT, preferred_element_type=F32)

    def gate(gl):
        up_hi, up_lo = _split2(wup_ref[...])
        group = lax.broadcasted_iota(jnp.int32, up_hi.shape, 0) // rank
        g = _dot(gl, jnp.where(group == 2, up_lo, up_hi)) + bg_ref[...]
        la_ref[...] = _log_sigmoid(g) * (1.0 / GLA_GATE_TAU)

    @pl.when(pl.program_id(1) == 0)
    def _():
        xn = _rms(x_ref[...], nw_ref[...]).astype(BF)
        xn_ref[...] = xn
        o_ref[...] = dot_t(xn, w_ref[...].astype(BF)).astype(o_ref.dtype)
        gl_hi, gl_lo = _split2(dot_t(xn, wgl_ref[...]))
        group = lax.broadcasted_iota(jnp.int32, gl_hi.shape, 1) // rank
        gl = jnp.where(group == 1, gl_lo, gl_hi)
        gl_ref[...] = gl
        gate(gl)

    @pl.when(pl.program_id(1) > 0)
    def _():
        gate(gl_ref[...])
        o_ref[...] = dot_t(xn_ref[...], w_ref[...].astype(BF)).astype(o_ref.dtype)


def _gla_inproj(x, nw, w_t_all, layer, n, rank, w_gate_up, bg, *, tm, tn):
    t, d = x.shape
    steps = n // tn
    qk_w = w_gate_up.shape[1]
    slab = qk_w // steps
    assert slab * steps == qk_w and slab % LANES == 0 and 3 * rank <= LANES
    three = lambda a: jnp.pad(jnp.tile(a, (3, 1)), ((0, LANES - 3 * rank), (0, 0)))
    wgl_t = three(w_t_all[layer, n:n + rank]).astype(BF)
    wup = three(w_gate_up)
    win = _Windows()
    scratch = [((tm, d), BF), ((tm, LANES), BF)]
    temps = _nbytes((tm, tn), F32) + _nbytes((tn, d), BF)
    return pl.pallas_call(
        functools.partial(_gla_inproj_kernel, rank=rank),
        grid=(t // tm, steps),
        in_specs=[
            win.spec((tm, d), F32, lambda i, j: (i, 0)),
            win.spec((1, d), F32, lambda i, j: (0, 0)),
            win.spec((None, tn, d), F32, lambda i, j: (layer, j, 0)),
            win.spec((LANES, d), BF, lambda i, j: (0, 0)),
            win.spec((LANES, slab), F32, lambda i, j: (0, j)),
            win.spec((1, slab), F32, lambda i, j: (0, j)),
        ],
        out_specs=[
            win.spec((tm, tn), BF, lambda i, j: (i, j)),
            win.spec((tm, slab), F32, lambda i, j: (i, j)),
        ],
        out_shape=[
            jax.ShapeDtypeStruct((t, n), BF),
            jax.ShapeDtypeStruct((t, qk_w), F32),
        ],
        scratch_shapes=_vmem(scratch),
        compiler_params=win.params(("parallel", "arbitrary"), scratch, temps),
        name="gla_inproj",
    )(x, nw, w_t_all, wgl_t, wup, bg)


def _gla_kernel(*refs, chunk, per_step, **kw):
    for c in range(per_step):
        _gla_chunk(*refs, chunk=chunk, r0=c * chunk, **kw)


def _gla_chunk(q_ref, k_ref, v_ref, r_ref, g_ref, gw_ref, o_ref, st_ref, qh_ref, kh_ref,
               *, chunk, r0, sub, heads, scale):
    n_sub = chunk // sub
    dk = q_ref.shape[-1] // heads
    dv = v_ref.shape[-1] // heads
    hs = range(heads)

    if r0 == 0:
        @pl.when(pl.program_id(2) == 0)
        def _():
            st_ref[...] = jnp.zeros_like(st_ref)
            qh_ref[...] = jnp.zeros_like(qh_ref)
            kh_ref[...] = jnp.zeros_like(kh_ref)

    def head_cols(ref, h, width):
        return ref[0, r0:r0 + chunk, h * width:(h + 1) * width]

    q = [head_cols(q_ref, h, dk).astype(F32) * scale for h in hs]
    k = [head_cols(k_ref, h, dk).astype(F32) for h in hs]
    g = [head_cols(g_ref, h, dk) for h in hs]

    row = lax.broadcasted_iota(jnp.int32, (chunk, chunk), 0)
    col = lax.broadcasted_iota(jnp.int32, (chunk, chunk), 1)
    tri = jnp.where(col <= row, 1.0, 0.0).astype(BF)
    diff = jnp.where(row // sub == col // sub, row - col, -1)

    b = []
    for h in hs:
        g_hi, g_lo = _split2(g[h])
        b.append(_dot(tri, g_hi) + _dot(tri, g_lo))

    st = [st_ref[h] for h in hs]
    o = [lax.dot_general((q[h] * jnp.exp(b[h])).astype(BF), st[h].astype(BF), _NT,
                         preferred_element_type=F32) for h in hs]

    def store_rows(ref, h, start, cols, x):
        lo = start // BF16_SUBLANES * BF16_SUBLANES
        hi = -(-(start + x.shape[0]) // BF16_SUBLANES) * BF16_SUBLANES
        pieces = [jnp.zeros((start - lo, x.shape[1]), F32)] if start > lo else []
        pieces.append(x)
        if hi > start + x.shape[0]:
            pieces.append(jnp.zeros((hi - start - x.shape[0], x.shape[1]), F32))
        x = jnp.concatenate(pieces, axis=0) if len(pieces) > 1 else x
        ref[h, lo:hi, cols] = x.astype(BF)

    scores = []
    for h in hs:
        k_run = None
        for i in range(1, n_sub):
            beta = b[h][i * sub - 1:i * sub, :]
            rows = slice(i * sub, (i + 1) * sub)
            prev = slice((i - 1) * sub, i * sub)
            cols = slice((i - 1) * dk, i * dk)
            store_rows(qh_ref, h, i * sub, cols, q[h][rows] * jnp.exp(b[h][rows] - beta))
            k_new = k[h][prev] * jnp.exp(beta - b[h][prev])
            if k_run is None:
                k_run = k_new
            else:
                k_run = jnp.concatenate([k_run * jnp.exp(beta - beta_prev), k_new], axis=0)
            store_rows(kh_ref, h, 0, cols, k_run)
            beta_prev = beta
        scores.append(lax.dot_general(qh_ref[h], kh_ref[h], _NT, preferred_element_type=F32))

    for h in hs:
        decay = jnp.exp(g[h])
        w = k[h]
        for d in range(sub):
            if d > 0:
                w = decay * pltpu.roll(w, 1, 0)
            scores[h] = jnp.where(diff == d, jnp.sum(q[h] * w, axis=-1, keepdims=True), scores[h])

    v = [head_cols(v_ref, h, dv) for h in hs]
    o = [o[h] + _dot(scores[h].astype(BF), v[h]) for h in hs]

    for h in hs:
        b_last = b[h][chunk - 1:chunk, :]
        k_dec = (k[h] * jnp.exp(b_last - b[h])).astype(BF)
        st_ref[h] = st[h] * jnp.exp(b_last) + lax.dot_general(v[h], k_dec, _TN, preferred_element_type=F32)

    for h in hs:
        r = head_cols(r_ref, h, dv).astype(F32)
        o_ref[0, r0:r0 + chunk, h * dv:(h + 1) * dv] = (_rms(o[h], gw_ref[...]) * _silu(r)).astype(o_ref.dtype)


def _gla_core(proj, la, gw, cast_jobs, *, chunk, sub, heads, per_step):
    bsz, s, _ = proj.shape
    dk = la.shape[-1] // GLA_HEADS
    dv = gw.shape[-1]
    groups = GLA_HEADS // heads
    k_blocks = groups
    v_blocks = 2 * GLA_HEADS * dk // (heads * dv)
    r_blocks = v_blocks + groups
    expanded = (chunk // sub - 1) * dk
    rows = per_step * chunk
    grid = (bsz, groups, s // rows)
    win = _Windows()
    c_ops, c_in, c_out, c_shapes, c_parts = _cast_specs(win, cast_jobs, grid)
    scratch = [((heads, dv, dk), F32), ((heads, chunk, expanded), BF), ((heads, chunk, expanded), BF)]
    temps = heads * (6 * _nbytes((chunk, dk), F32) + 2 * _nbytes((chunk, dv), F32))
    body = functools.partial(_gla_kernel, chunk=chunk, per_step=per_step, sub=sub, heads=heads,
                             scale=dk ** -0.5)
    outs = pl.pallas_call(
        _with_casts(body, 6, 1, c_parts),
        grid=grid,
        in_specs=[
            win.spec((1, rows, heads * dk), BF, lambda b, h, n: (b, n, h)),
            win.spec((1, rows, heads * dk), BF, lambda b, h, n: (b, n, k_blocks + h)),
            win.spec((1, rows, heads * dv), BF, lambda b, h, n: (b, n, v_blocks + h)),
            win.spec((1, rows, heads * dv), BF, lambda b, h, n: (b, n, r_blocks + h)),
            win.spec((1, rows, heads * dk), F32, lambda b, h, n: (b, n, h)),
            win.spec((1, dv), F32, lambda b, h, n: (0, 0)),
        ] + c_in,
        out_specs=[win.spec((1, rows, heads * dv), BF, lambda b, h, n: (b, n, h))] + c_out,
        out_shape=[jax.ShapeDtypeStruct((bsz, s, GLA_HEADS * dv), BF)] + c_shapes,
        scratch_shapes=_vmem(scratch),
        compiler_params=win.params(("arbitrary", "arbitrary", "arbitrary"), scratch, temps),
        name="gla_core",
    )(proj, proj, proj, proj, la, gw, *c_ops)
    return outs[0], outs[1:]


def _proj_res_kernel(x_ref, w_ref, res_ref, o_ref):
    o_ref[...] = res_ref[...] + _dot(x_ref[...], w_ref[...])


def _proj_res(x, w, res, *, tm):
    t, kdim = x.shape
    n = w.shape[1]
    win = _Windows()
    return pl.pallas_call(
        _proj_res_kernel,
        grid=(t // tm,),
        in_specs=[
            win.spec((tm, kdim), BF, lambda i: (i, 0)),
            win.spec((kdim, n), BF, lambda i: (0, 0)),
            win.spec((tm, n), F32, lambda i: (i, 0)),
        ],
        out_specs=win.spec((tm, n), F32, lambda i: (i, 0)),
        out_shape=jax.ShapeDtypeStruct((t, n), F32),
        compiler_params=win.params(("parallel",), temps=_nbytes((tm, n), F32)),
        name="proj_res",
    )(x, w, res)


def _ffn_kernel(h_ref, nw_ref, wg_ref, wu_ref, wd_ref, fw_ref, o_ref, xn_ref, *, final_norm):
    f = pl.program_id(1)

    def step(acc_ref):
        xn = xn_ref[...]
        act = (_silu(_dot(xn, wg_ref[...])) * _dot(xn, wu_ref[...])).astype(BF)
        o_ref[...] = acc_ref[...] + _dot(act, wd_ref[...])

    @pl.when(f == 0)
    def _():
        xn_ref[...] = _rms(h_ref[...], nw_ref[...]).astype(BF)
        step(h_ref)

    @pl.when(f > 0)
    def _():
        step(o_ref)

    if final_norm:
        @pl.when(f == pl.num_programs(1) - 1)
        def _():
            o_ref[...] = _rms(o_ref[...], fw_ref[...])


def _ffn(h, nw, w_gate_up, w_down, fw, *, tm, tf, final_norm):
    t, d = h.shape
    d_ff = w_down.shape[0]
    nf = d_ff // tf
    win = _Windows()
    scratch = [((tm, d), BF)]
    temps = 2 * _nbytes((tm, tf), F32) + _nbytes((tm, tf), BF)
    return pl.pallas_call(
        functools.partial(_ffn_kernel, final_norm=final_norm),
        grid=(t // tm, nf),
        in_specs=[
            win.spec((tm, d), F32, lambda i, f: (i, 0)),
            win.spec((1, d), F32, lambda i, f: (0, 0)),
            win.spec((d, tf), BF, lambda i, f: (0, f)),
            win.spec((d, tf), BF, lambda i, f: (0, nf + f)),
            win.spec((tf, d), BF, lambda i, f: (f, 0)),
            win.spec((1, d), F32, lambda i, f: (0, 0)),
        ],
        out_specs=win.spec((tm, d), F32, lambda i, f: (i, 0)),
        out_shape=jax.ShapeDtypeStruct((t, d), F32),
        scratch_shapes=_vmem(scratch),
        compiler_params=win.params(("parallel", "arbitrary"), scratch, temps),
        name="ffn",
    )(h, nw, w_gate_up, w_gate_up, w_down, fw)


_EXP2_UNDERFLOW = -127.0
LOG2_E = 1.4426950408889634


def _sb_kernel(q_ref, k_ref, v_ref, o_ref, *, tq, group):
    s = q_ref.shape[1]
    tw = 2 * tq
    def later(n):
        r = lax.broadcasted_iota(jnp.int32, (n, n), 0)
        c = lax.broadcasted_iota(jnp.int32, (n, n), 1)
        return jnp.where(r > c, 1.0, 0.0).astype(BF)

    def causal(n_keys, offset):
        qr = lax.broadcasted_iota(jnp.int32, (tq, n_keys), 0)
        kc = lax.broadcasted_iota(jnp.int32, (tq, n_keys), 1)
        return kc < qr + offset

    later_w, later_b = later(tw), later(tq)
    diag_mask = causal(tq, 0)
    win_mask = causal(tw, tq)

    def scores(q, start, width, mask):
        kb = k_ref[0, pl.ds(start, width), :]
        z = lax.dot_general(q, kb, _NT, preferred_element_type=F32)
        nz = -z
        lf = jnp.minimum(nz, 0.0) - jnp.log2(1.0 + jnp.exp2(jnp.minimum(z, nz)))
        if mask is not None:
            lf = jnp.where(mask, lf, 0.0)
        return z, lf

    def suffix(lf, later, carry):
        after = _dot(lf.astype(BF), later)
        if carry is not None:
            after = after + carry
        return after, after[:, :1] + lf[:, :1]

    def weighted(z, lf, after, start, width, mask):
        a = jnp.exp2(z + lf + after)
        if mask is not None:
            a = jnp.where(mask, a, 0.0)
        return _dot(a.astype(BF), v_ref[0, pl.ds(start, width), :])

    def window(tile):
        if isinstance(tile, int):
            if tile == 0:
                return 0, tq, later_b, diag_mask
            return (tile - 1) * tq, tw, later_w, win_mask
        return pl.multiple_of((tile - 1) * tq, tq), tw, later_w, win_mask

    def row0(tile):
        return tile * tq if isinstance(tile, int) else pl.multiple_of(tile * tq, tq)

    def extend(tile, q, carry, top, acc):
        def cond(st):
            return jnp.logical_and(st[0] >= 0, st[1] > _EXP2_UNDERFLOW)

        def body(st):
            j, _, carry, acc = st
            start = pl.multiple_of(j * tq, tq)
            z, lf = scores(q, start, tq, None)
            after, carry = suffix(lf, later_b, carry)
            acc = acc + weighted(z, lf, after, start, tq, None)
            return j - 1, jnp.max(carry), carry, acc

        init = (jnp.asarray(tile - 2, jnp.int32), top, carry, acc)
        return lax.while_loop(cond, body, init)[3]

    def run_group(base):
        tiles = [base + g for g in range(group)]
        wins = [window(t) for t in tiles]
        qs = [q_ref[0, pl.ds(row0(t), tq), :] for t in tiles]
        zl = [scores(q, w[0], w[1], w[3]) for q, w in zip(qs, wins)]
        ac = [suffix(lf, w[2], None) for (_, lf), w in zip(zl, wins)]
        accs = [weighted(z, lf, after, w[0], w[1], w[3]) for (z, lf), (after, _), w in zip(zl, ac, wins)]
        tops = [jnp.max(carry) for _, carry in ac]
        for t, q, (_, carry), top, acc in zip(tiles, qs, ac, tops, accs):
            acc = extend(t, q, carry, top, acc)
            o_ref[0, pl.ds(row0(t), tq), :] = acc.astype(o_ref.dtype)

    run_group(0)

    def loop_body(it, carry):
        run_group(it * group)
        return carry

    lax.fori_loop(1, s // (tq * group), loop_body, 0)


def _sb_attention(qkv, cast_jobs, *, tq, group):
    bsz, s, w3 = qkv.shape
    w = w3 // 3
    hd = w // SB_HEADS
    grid = (bsz, SB_HEADS)
    win = _Windows()
    c_ops, c_in, c_out, c_shapes, c_parts = _cast_specs(win, cast_jobs, grid)
    temps = group * 8 * _nbytes((tq, 2 * tq), F32)
    outs = pl.pallas_call(
        _with_casts(functools.partial(_sb_kernel, tq=tq, group=group), 3, 1, c_parts),
        grid=grid,
        in_specs=[
            win.spec((1, s, hd), BF, lambda b, h: (b, 0, h)),
            win.spec((1, s, hd), BF, lambda b, h: (b, 0, SB_HEADS + h)),
            win.spec((1, s, hd), BF, lambda b, h: (b, 0, 2 * SB_HEADS + h)),
        ] + c_in,
        out_specs=[win.spec((1, s, hd), BF, lambda b, h: (b, 0, h))] + c_out,
        out_shape=[jax.ShapeDtypeStruct((bsz, s, w), BF)] + c_shapes,
        compiler_params=win.params(("arbitrary", "arbitrary"), temps=temps),
        name="sb_attn",
    )(qkv, qkv, qkv, *c_ops)
    return outs[0], outs[1:]


def kernel(x, attn_norm_w, ffn_norm_w, gla_w_in, gla_w_gate_up, gla_b_gate, gla_gnorm_w, gla_w_out,
           kv_norm_w, sb_w_kv, sb_w_q, sb_w_out, ffn_w_gate_up, ffn_w_down, final_norm_w):
    bsz, s, d = x.shape
    t = bsz * s
    depth = attn_norm_w.shape[0]
    n_gla = gla_w_in.shape[0]
    rank = gla_w_gate_up.shape[1]
    main_w = gla_w_in.shape[2] - rank
    hd = d // SB_HEADS
    row = lambda v: v.reshape(1, -1)
    assert n_gla >= 1 and depth - n_gla == 1

    w_in_t = jnp.swapaxes(gla_w_in, 1, 2)

    h = x.reshape(t, d)
    for layer in range(depth):
        casts = [[(ffn_w_gate_up, layer)], [(ffn_w_down, layer)]]
        if layer < n_gla:
            proj, la = _gla_inproj(h, row(attn_norm_w[layer]), w_in_t, layer, main_w, rank,
                                   gla_w_gate_up[layer], row(gla_b_gate[layer]), **TILES["inproj"])
            casts.append([(gla_w_out, layer)])
            if layer == n_gla - 1:
                casts.append([(sb_w_q, 0), (sb_w_kv, None)])
            o, cast = _gla_core(proj.reshape(bsz, s, main_w), la.reshape(bsz, s, -1),
                                row(gla_gnorm_w[layer]), casts, **TILES["gla"])
            if layer == n_gla - 1:
                w_qkv = cast[3]
        else:
            j = layer - n_gla
            qkv = _norm_matmul2(h, row(attn_norm_w[layer]), row(kv_norm_w), w_qkv, sb_w_q.shape[2],
                                scale_a=hd ** -0.5 * LOG2_E, **TILES["qkv"])
            casts.append([(sb_w_out, j)])
            o, cast = _sb_attention(qkv.reshape(bsz, s, -1), casts, **TILES["sb"])
        w_gate_up, w_down, w_out = cast[:3]
        h = _proj_res(o.reshape(t, -1), w_out, h, **TILES["proj"])
        h = _ffn(h, row(ffn_norm_w[layer]), w_gate_up, w_down, row(final_norm_w),
                 final_norm=layer == depth - 1, **TILES["ffn"])
    return h.reshape(bsz, s, d)
```

```python
import functools

import jax
import jax.numpy as jnp
from jax import lax
from jax.experimental import pallas as pl
from jax.experimental.pallas import tpu as pltpu

EPS = 1e-6
GLA_HEADS = 4
GLA_GATE_TAU = 16.0
SB_HEADS = 16
BF = jnp.bfloat16
F32 = jnp.float32
LANES = 128
BF16_SUBLANES = 16
MIB = 2 ** 20
V7X_VMEM_BYTES = 64 * MIB
VMEM_HEADROOM = 4 * MIB
VMEM_SPILL_ALLOWANCE = 2 * MIB

TILES = dict(
    inproj=dict(tm=1024, tn=768),
    gla=dict(chunk=128, sub=8, heads=GLA_HEADS),
    proj=dict(tm=512),
    ffn=dict(tm=1024, tf=512),
    qkv=dict(tm=1024, tn=1024),
    sb=dict(tq=128, group=8),
)

_NT = (((1,), (1,)), ((), ()))
_TN = (((0,), (0,)), ((), ()))


def _nbytes(shape, dtype):
    n = jnp.dtype(dtype).itemsize
    for dim in shape:
        n *= dim or 1
    return n


class _Windows:
    def __init__(self):
        self.bytes = 0

    def spec(self, block, dtype, index_map):
        self.bytes += _nbytes(block, dtype)
        return pl.BlockSpec(block, index_map)

    def params(self, sem, scratch=(), temps=0):
        need = 2 * self.bytes + sum(_nbytes(shape, dtype) for shape, dtype in scratch) + temps
        need = -(-(need + VMEM_SPILL_ALLOWANCE) // MIB) * MIB
        assert need <= V7X_VMEM_BYTES - VMEM_HEADROOM, need
        return pltpu.CompilerParams(dimension_semantics=sem, vmem_limit_bytes=need)


def _vmem(scratch):
    return [pltpu.VMEM(shape, dtype) for shape, dtype in scratch]


def _dot(a, b):
    return jnp.dot(a, b, preferred_element_type=F32)


def _rms(x, w):
    return x * lax.rsqrt(jnp.mean(x * x, axis=-1, keepdims=True) + EPS) * w


def _split2(x):
    hi = x.astype(BF)
    lo = (x - hi.astype(F32)).astype(BF)
    return hi, lo


def _log_sigmoid(x):
    return jnp.minimum(x, 0.0) - jnp.log(1.0 + jnp.exp(-jnp.abs(x)))


def _silu(x):
    return x * (1.0 / (1.0 + jnp.exp(-x)))


def _cast_specs(win, jobs, grid):
    steps = 1
    for g in grid:
        steps *= g

    def linear(idx):
        s = 0
        for g, i in zip(grid, idx):
            s = s * g + i
        return s

    operands, in_specs, out_specs, out_shapes, parts = [], [], [], [], []
    for job in jobs:
        n_rows = job[0][0].shape[-2]
        n_blk = max(n for n in range(1, steps + 1)
                    if n_rows % n == 0 and (n_rows // n) % BF16_SUBLANES == 0)
        rows = n_rows // n_blk
        blk = lambda *idx, n_blk=n_blk: linear(idx) * n_blk // steps
        for arr, lead in job:
            assert arr.shape[-2] == n_rows
            operands.append(arr)
            if lead is None:
                in_specs.append(win.spec((rows, arr.shape[-1]), F32, lambda *idx, blk=blk: (blk(*idx), 0)))
            else:
                in_specs.append(win.spec((None, rows, arr.shape[-1]), F32,
                                         lambda *idx, blk=blk, lead=lead: (lead, blk(*idx), 0)))
        width = sum(arr.shape[-1] for arr, _ in job)
        out_specs.append(win.spec((rows, width), BF, lambda *idx, blk=blk: (blk(*idx), 0)))
        out_shapes.append(jax.ShapeDtypeStruct((n_rows, width), BF))
        parts.append(len(job))
    return operands, in_specs, out_specs, out_shapes, parts


def _with_casts(body, n_in, n_out, parts):
    n_src = sum(parts)

    def kernel(*refs):
        ins, rest = refs[:n_in], refs[n_in:]
        srcs, rest = rest[:n_src], rest[n_src:]
        outs, rest = rest[:n_out], rest[n_out:]
        dsts, scratch = rest[:len(parts)], rest[len(parts):]
        body(*ins, *outs, *scratch)
        srcs = list(srcs)
        for dst, n_parts in zip(dsts, parts):
            col = 0
            for _ in range(n_parts):
                src = srcs.pop(0)
                dst[:, col:col + src.shape[-1]] = src[...].astype(BF)
                col += src.shape[-1]

    return kernel


def _norm_matmul2_kernel(x_ref, nwa_ref, nwb_ref, w_ref, o_ref, xn_ref, *, na, scale_a):
    j = pl.program_id(1)

    @pl.when(j == 0)
    def _():
        x = x_ref[...]
        xhat = x * lax.rsqrt(jnp.mean(x * x, axis=-1, keepdims=True) + EPS)
        xa = (xhat * nwa_ref[...]).astype(BF)
        xn_ref[0] = xa
        xn_ref[1] = (xhat * nwb_ref[...]).astype(BF)
        o_ref[...] = (_dot(xa, w_ref[...]) * scale_a).astype(o_ref.dtype)

    @pl.when((j > 0) & (j < na))
    def _():
        o_ref[...] = (_dot(xn_ref[0], w_ref[...]) * scale_a).astype(o_ref.dtype)

    @pl.when(j >= na)
    def _():
        o_ref[...] = _dot(xn_ref[1], w_ref[...]).astype(o_ref.dtype)


def _norm_matmul2(x, nwa, nwb, w, n_a, *, tm, tn, scale_a):
    t, d = x.shape
    n = w.shape[1]
    win = _Windows()
    scratch = [((2, tm, d), BF)]
    temps = 2 * _nbytes((tm, d), F32) + _nbytes((tm, tn), F32)
    return pl.pallas_call(
        functools.partial(_norm_matmul2_kernel, na=n_a // tn, scale_a=scale_a),
        grid=(t // tm, n // tn),
        in_specs=[
            win.spec((tm, d), F32, lambda i, j: (i, 0)),
            win.spec((1, d), F32, lambda i, j: (0, 0)),
            win.spec((1, d), F32, lambda i, j: (0, 0)),
            win.spec((d, tn), BF, lambda i, j: (0, j)),
        ],
        out_specs=win.spec((tm, tn), BF, lambda i, j: (i, j)),
        out_shape=jax.ShapeDtypeStruct((t, n), BF),
        scratch_shapes=_vmem(scratch),
        compiler_params=win.params(("parallel", "arbitrary"), scratch, temps),
        name="norm_matmul2",
    )(x, nwa, nwb, w)


def _gla_inproj_kernel(x_ref, nw_ref, w_ref, wgl_ref, wup_ref, bg_ref, o_ref, la_ref, xn_ref, gl_ref, *, rank):
    def dot_t(a, w_t):
        return lax.dot_general(a, w_t, _NT, preferred_element_type=F32)

    def gate(gl):
        up_hi, up_lo = _split2(wup_ref[...])
        group = lax.broadcasted_iota(jnp.int32, up_hi.shape, 0) // rank
        g = _dot(gl, jnp.where(group == 2, up_lo, up_hi)) + bg_ref[...]
        la_ref[...] = _log_sigmoid(g) * (1.0 / GLA_GATE_TAU)

    @pl.when(pl.program_id(1) == 0)
    def _():
        xn = _rms(x_ref[...], nw_ref[...]).astype(BF)
        xn_ref[...] = xn
        o_ref[...] = dot_t(xn, w_ref[...].astype(BF)).astype(o_ref.dtype)
        gl_hi, gl_lo = _split2(dot_t(xn, wgl_ref[...]))
        group = lax.broadcasted_iota(jnp.int32, gl_hi.shape, 1) // rank
        gl = jnp.where(group == 1, gl_lo, gl_hi)
        gl_ref[...] = gl
        gate(gl)

    @pl.when(pl.program_id(1) > 0)
    def _():
        gate(gl_ref[...])
        o_ref[...] = dot_t(xn_ref[...], w_ref[...].astype(BF)).astype(o_ref.dtype)


def _gla_inproj(x, nw, w_t_all, layer, n, rank, w_gate_up, bg, *, tm, tn):
    t, d = x.shape
    steps = n // tn
    qk_w = w_gate_up.shape[1]
    slab = qk_w // steps
    assert slab * steps == qk_w and slab % LANES == 0 and 3 * rank <= LANES
    three = lambda a: jnp.pad(jnp.tile(a, (3, 1)), ((0, LANES - 3 * rank), (0, 0)))
    wgl_t = three(w_t_all[layer, n:n + rank]).astype(BF)
    wup = three(w_gate_up)
    win = _Windows()
    scratch = [((tm, d), BF), ((tm, LANES), BF)]
    temps = _nbytes((tm, tn), F32) + _nbytes((tn, d), BF)
    return pl.pallas_call(
        functools.partial(_gla_inproj_kernel, rank=rank),
        grid=(t // tm, steps),
        in_specs=[
            win.spec((tm, d), F32, lambda i, j: (i, 0)),
            win.spec((1, d), F32, lambda i, j: (0, 0)),
            win.spec((None, tn, d), F32, lambda i, j: (layer, j, 0)),
            win.spec((LANES, d), BF, lambda i, j: (0, 0)),
            win.spec((LANES, slab), F32, lambda i, j: (0, j)),
            win.spec((1, slab), F32, lambda i, j: (0, j)),
        ],
        out_specs=[
            win.spec((tm, tn), BF, lambda i, j: (i, j)),
            win.spec((tm, slab), F32, lambda i, j: (i, j)),
        ],
        out_shape=[
            jax.ShapeDtypeStruct((t, n), BF),
            jax.ShapeDtypeStruct((t, qk_w), F32),
        ],
        scratch_shapes=_vmem(scratch),
        compiler_params=win.params(("parallel", "arbitrary"), scratch, temps),
        name="gla_inproj",
    )(x, nw, w_t_all, wgl_t, wup, bg)


def _gla_kernel(q_ref, k_ref, v_ref, r_ref, g_ref, gw_ref, o_ref, st_ref, qh_ref, kh_ref,
                *, chunk, sub, heads, scale):
    n_sub = chunk // sub
    dk = q_ref.shape[-1] // heads
    dv = v_ref.shape[-1] // heads
    hs = range(heads)

    @pl.when(pl.program_id(2) == 0)
    def _():
        st_ref[...] = jnp.zeros_like(st_ref)
        qh_ref[...] = jnp.zeros_like(qh_ref)
        kh_ref[...] = jnp.zeros_like(kh_ref)

    def head_cols(ref, h, width):
        return ref[0, :, h * width:(h + 1) * width]

    q = [head_cols(q_ref, h, dk).astype(F32) * scale for h in hs]
    k = [head_cols(k_ref, h, dk).astype(F32) for h in hs]
    g = [head_cols(g_ref, h, dk) for h in hs]

    row = lax.broadcasted_iota(jnp.int32, (chunk, chunk), 0)
    col = lax.broadcasted_iota(jnp.int32, (chunk, chunk), 1)
    tri = jnp.where(col <= row, 1.0, 0.0).astype(BF)
    diff = jnp.where(row // sub == col // sub, row - col, -1)

    b = []
    for h in hs:
        g_hi, g_lo = _split2(g[h])
        b.append(_dot(tri, g_hi) + _dot(tri, g_lo))

    st = [st_ref[h] for h in hs]
    o = [lax.dot_general((q[h] * jnp.exp(b[h])).astype(BF), st[h].astype(BF), _NT,
                         preferred_element_type=F32) for h in hs]

    def store_rows(ref, h, start, cols, x):
        lo = start // BF16_SUBLANES * BF16_SUBLANES
        hi = -(-(start + x.shape[0]) // BF16_SUBLANES) * BF16_SUBLANES
        pieces = [jnp.zeros((start - lo, x.shape[1]), F32)] if start > lo else []
        pieces.append(x)
        if hi > start + x.shape[0]:
            pieces.append(jnp.zeros((hi - start - x.shape[0], x.shape[1]), F32))
        x = jnp.concatenate(pieces, axis=0) if len(pieces) > 1 else x
        ref[h, lo:hi, cols] = x.astype(BF)

    scores = []
    for h in hs:
        k_run = None
        for i in range(1, n_sub):
            beta = b[h][i * sub - 1:i * sub, :]
            rows = slice(i * sub, (i + 1) * sub)
            prev = slice((i - 1) * sub, i * sub)
            cols = slice((i - 1) * dk, i * dk)
            store_rows(qh_ref, h, i * sub, cols, q[h][rows] * jnp.exp(b[h][rows] - beta))
            k_new = k[h][prev] * jnp.exp(beta - b[h][prev])
            if k_run is None:
                k_run = k_new
            else:
                k_run = jnp.concatenate([k_run * jnp.exp(beta - beta_prev), k_new], axis=0)
            store_rows(kh_ref, h, 0, cols, k_run)
            beta_prev = beta
        scores.append(lax.dot_general(qh_ref[h], kh_ref[h], _NT, preferred_element_type=F32))

    for h in hs:
        decay = jnp.exp(g[h])
        w = k[h]
        for d in range(sub):
            if d > 0:
                w = decay * pltpu.roll(w, 1, 0)
            scores[h] = jnp.where(diff == d, jnp.sum(q[h] * w, axis=-1, keepdims=True), scores[h])

    v = [head_cols(v_ref, h, dv) for h in hs]
    o = [o[h] + _dot(scores[h].astype(BF), v[h]) for h in hs]

    for h in hs:
        b_last = b[h][chunk - 1:chunk, :]
        k_dec = (k[h] * jnp.exp(b_last - b[h])).astype(BF)
        st_ref[h] = st[h] * jnp.exp(b_last) + lax.dot_general(v[h], k_dec, _TN, preferred_element_type=F32)

    for h in hs:
        r = head_cols(r_ref, h, dv).astype(F32)
        o_ref[0, :, h * dv:(h + 1) * dv] = (_rms(o[h], gw_ref[...]) * _silu(r)).astype(o_ref.dtype)


def _gla_core(proj, la, gw, cast_jobs, *, chunk, sub, heads):
    bsz, s, _ = proj.shape
    dk = la.shape[-1] // GLA_HEADS
    dv = gw.shape[-1]
    groups = GLA_HEADS // heads
    k_blocks = groups
    v_blocks = 2 * GLA_HEADS * dk // (heads * dv)
    r_blocks = v_blocks + groups
    expanded = (chunk // sub - 1) * dk
    grid = (bsz, groups, s // chunk)
    win = _Windows()
    c_ops, c_in, c_out, c_shapes, c_parts = _cast_specs(win, cast_jobs, grid)
    scratch = [((heads, dv, dk), F32), ((heads, chunk, expanded), BF), ((heads, chunk, expanded), BF)]
    temps = heads * (6 * _nbytes((chunk, dk), F32) + 2 * _nbytes((chunk, dv), F32))
    body = functools.partial(_gla_kernel, chunk=chunk, sub=sub, heads=heads, scale=dk ** -0.5)
    outs = pl.pallas_call(
        _with_casts(body, 6, 1, c_parts),
        grid=grid,
        in_specs=[
            win.spec((1, chunk, heads * dk), BF, lambda b, h, n: (b, n, h)),
            win.spec((1, chunk, heads * dk), BF, lambda b, h, n: (b, n, k_blocks + h)),
            win.spec((1, chunk, heads * dv), BF, lambda b, h, n: (b, n, v_blocks + h)),
            win.spec((1, chunk, heads * dv), BF, lambda b, h, n: (b, n, r_blocks + h)),
            win.spec((1, chunk, heads * dk), F32, lambda b, h, n: (b, n, h)),
            win.spec((1, dv), F32, lambda b, h, n: (0, 0)),
        ] + c_in,
        out_specs=[win.spec((1, chunk, heads * dv), BF, lambda b, h, n: (b, n, h))] + c_out,
        out_shape=[jax.ShapeDtypeStruct((bsz, s, GLA_HEADS * dv), BF)] + c_shapes,
        scratch_shapes=_vmem(scratch),
        compiler_params=win.params(("arbitrary", "arbitrary", "arbitrary"), scratch, temps),
        name="gla_core",
    )(proj, proj, proj, proj, la, gw, *c_ops)
    return outs[0], outs[1:]


def _proj_res_kernel(x_ref, w_ref, res_ref, o_ref):
    o_ref[...] = res_ref[...] + _dot(x_ref[...], w_ref[...])


def _proj_res(x, w, res, *, tm):
    t, kdim = x.shape
    n = w.shape[1]
    win = _Windows()
    return pl.pallas_call(
        _proj_res_kernel,
        grid=(t // tm,),
        in_specs=[
            win.spec((tm, kdim), BF, lambda i: (i, 0)),
            win.spec((kdim, n), BF, lambda i: (0, 0)),
            win.spec((tm, n), F32, lambda i: (i, 0)),
        ],
        out_specs=win.spec((tm, n), F32, lambda i: (i, 0)),
        out_shape=jax.ShapeDtypeStruct((t, n), F32),
        compiler_params=win.params(("parallel",), temps=_nbytes((tm, n), F32)),
        name="proj_res",
    )(x, w, res)


def _ffn_kernel(h_ref, nw_ref, wg_ref, wu_ref, wd_ref, fw_ref, o_ref, xn_ref, *, final_norm):
    f = pl.program_id(1)

    def step(acc_ref):
        xn = xn_ref[...]
        act = (_silu(_dot(xn, wg_ref[...])) * _dot(xn, wu_ref[...])).astype(BF)
        o_ref[...] = acc_ref[...] + _dot(act, wd_ref[...])

    @pl.when(f == 0)
    def _():
        xn_ref[...] = _rms(h_ref[...], nw_ref[...]).astype(BF)
        step(h_ref)

    @pl.when(f > 0)
    def _():
        step(o_ref)

    if final_norm:
        @pl.when(f == pl.num_programs(1) - 1)
        def _():
            o_ref[...] = _rms(o_ref[...], fw_ref[...])


def _ffn(h, nw, w_gate_up, w_down, fw, *, tm, tf, final_norm):
    t, d = h.shape
    d_ff = w_down.shape[0]
    nf = d_ff // tf
    win = _Windows()
    scratch = [((tm, d), BF)]
    temps = 2 * _nbytes((tm, tf), F32) + _nbytes((tm, tf), BF)
    return pl.pallas_call(
        functools.partial(_ffn_kernel, final_norm=final_norm),
        grid=(t // tm, nf),
        in_specs=[
            win.spec((tm, d), F32, lambda i, f: (i, 0)),
            win.spec((1, d), F32, lambda i, f: (0, 0)),
            win.spec((d, tf), BF, lambda i, f: (0, f)),
            win.spec((d, tf), BF, lambda i, f: (0, nf + f)),
            win.spec((tf, d), BF, lambda i, f: (f, 0)),
            win.spec((1, d), F32, lambda i, f: (0, 0)),
        ],
        out_specs=win.spec((tm, d), F32, lambda i, f: (i, 0)),
        out_shape=jax.ShapeDtypeStruct((t, d), F32),
        scratch_shapes=_vmem(scratch),
        compiler_params=win.params(("parallel", "arbitrary"), scratch, temps),
        name="ffn",
    )(h, nw, w_gate_up, w_gate_up, w_down, fw)


_EXP2_UNDERFLOW = -127.0
LOG2_E = 1.4426950408889634


def _sb_kernel(q_ref, k_ref, v_ref, o_ref, *, tq, group):
    s = q_ref.shape[1]
    tw = 2 * tq
    def later(n):
        r = lax.broadcasted_iota(jnp.int32, (n, n), 0)
        c = lax.broadcasted_iota(jnp.int32, (n, n), 1)
        return jnp.where(r > c, 1.0, 0.0).astype(BF)

    def causal(n_keys, offset):
        qr = lax.broadcasted_iota(jnp.int32, (tq, n_keys), 0)
        kc = lax.broadcasted_iota(jnp.int32, (tq, n_keys), 1)
        return kc < qr + offset

    later_w, later_b = later(tw), later(tq)
    diag_mask = causal(tq, 0)
    win_mask = causal(tw, tq)

    def scores(q, start, width, mask):
        kb = k_ref[0, pl.ds(start, width), :]
        z = lax.dot_general(q, kb, _NT, preferred_element_type=F32)
        nz = -z
        lf = jnp.minimum(nz, 0.0) - jnp.log2(1.0 + jnp.exp2(jnp.minimum(z, nz)))
        if mask is not None:
            lf = jnp.where(mask, lf, 0.0)
        return z, lf

    def suffix(lf, later, carry):
        after = _dot(lf.astype(BF), later)
        if carry is not None:
            after = after + carry
        return after, after[:, :1] + lf[:, :1]

    def weighted(z, lf, after, start, width, mask):
        a = jnp.exp2(z + lf + after)
        if mask is not None:
            a = jnp.where(mask, a, 0.0)
        return _dot(a.astype(BF), v_ref[0, pl.ds(start, width), :])

    def window(tile):
        if isinstance(tile, int):
            if tile == 0:
                return 0, tq, later_b, diag_mask
            return (tile - 1) * tq, tw, later_w, win_mask
        return pl.multiple_of((tile - 1) * tq, tq), tw, later_w, win_mask

    def row0(tile):
        return tile * tq if isinstance(tile, int) else pl.multiple_of(tile * tq, tq)

    def extend(tile, q, carry, top, acc):
        def cond(st):
            return jnp.logical_and(st[0] >= 0, st[1] > _EXP2_UNDERFLOW)

        def body(st):
            j, _, carry, acc = st
            start = pl.multiple_of(j * tq, tq)
            z, lf = scores(q, start, tq, None)
            after, carry = suffix(lf, later_b, carry)
            acc = acc + weighted(z, lf, after, start, tq, None)
            return j - 1, jnp.max(carry), carry, acc

        init = (jnp.asarray(tile - 2, jnp.int32), top, carry, acc)
        return lax.while_loop(cond, body, init)[3]

    def first_passes(base):
        tiles = [base + g for g in range(group)]
        wins = [window(t) for t in tiles]
        qs = [q_ref[0, pl.ds(row0(t), tq), :] for t in tiles]
        zl = [scores(q, w[0], w[1], w[3]) for q, w in zip(qs, wins)]
        ac = [suffix(lf, w[2], None) for (_, lf), w in zip(zl, wins)]
        accs = [weighted(z, lf, after, w[0], w[1], w[3]) for (z, lf), (after, _), w in zip(zl, ac, wins)]
        tops = [jnp.max(carry) for _, carry in ac]
        return list(zip(tiles, qs, [carry for _, carry in ac], tops, accs))

    todo = []
    for base in range(0, s // tq, group):
        todo += first_passes(base)
    for t, q, carry, top, acc in todo:
        acc = extend(t, q, carry, top, acc)
        o_ref[0, pl.ds(row0(t), tq), :] = acc.astype(o_ref.dtype)


def _sb_attention(qkv, cast_jobs, *, tq, group):
    bsz, s, w3 = qkv.shape
    w = w3 // 3
    hd = w // SB_HEADS
    grid = (bsz, SB_HEADS)
    win = _Windows()
    c_ops, c_in, c_out, c_shapes, c_parts = _cast_specs(win, cast_jobs, grid)
    temps = group * 8 * _nbytes((tq, 2 * tq), F32)
    outs = pl.pallas_call(
        _with_casts(functools.partial(_sb_kernel, tq=tq, group=group), 3, 1, c_parts),
        grid=grid,
        in_specs=[
            win.spec((1, s, hd), BF, lambda b, h: (b, 0, h)),
            win.spec((1, s, hd), BF, lambda b, h: (b, 0, SB_HEADS + h)),
            win.spec((1, s, hd), BF, lambda b, h: (b, 0, 2 * SB_HEADS + h)),
        ] + c_in,
        out_specs=[win.spec((1, s, hd), BF, lambda b, h: (b, 0, h))] + c_out,
        out_shape=[jax.ShapeDtypeStruct((bsz, s, w), BF)] + c_shapes,
        compiler_params=win.params(("arbitrary", "arbitrary"), temps=temps),
        name="sb_attn",
    )(qkv, qkv, qkv, *c_ops)
    return outs[0], outs[1:]


def kernel(x, attn_norm_w, ffn_norm_w, gla_w_in, gla_w_gate_up, gla_b_gate, gla_gnorm_w, gla_w_out,
           kv_norm_w, sb_w_kv, sb_w_q, sb_w_out, ffn_w_gate_up, ffn_w_down, final_norm_w):
    bsz, s, d = x.shape
    t = bsz * s
    depth = attn_norm_w.shape[0]
    n_gla = gla_w_in.shape[0]
    rank = gla_w_gate_up.shape[1]
    main_w = gla_w_in.shape[2] - rank
    hd = d // SB_HEADS
    row = lambda v: v.reshape(1, -1)
    assert n_gla >= 1 and depth - n_gla == 1

    w_in_t = jnp.swapaxes(gla_w_in, 1, 2)

    h = x.reshape(t, d)
    for layer in range(depth):
        casts = [[(ffn_w_gate_up, layer)], [(ffn_w_down, layer)]]
        if layer < n_gla:
            proj, la = _gla_inproj(h, row(attn_norm_w[layer]), w_in_t, layer, main_w, rank,
                                   gla_w_gate_up[layer], row(gla_b_gate[layer]), **TILES["inproj"])
            casts.append([(gla_w_out, layer)])
            if layer == n_gla - 1:
                casts.append([(sb_w_q, 0), (sb_w_kv, None)])
            o, cast = _gla_core(proj.reshape(bsz, s, main_w), la.reshape(bsz, s, -1),
                                row(gla_gnorm_w[layer]), casts, **TILES["gla"])
            if layer == n_gla - 1:
                w_qkv = cast[3]
        else:
            j = layer - n_gla
            qkv = _norm_matmul2(h, row(attn_norm_w[layer]), row(kv_norm_w), w_qkv, sb_w_q.shape[2],
                                scale_a=hd ** -0.5 * LOG2_E, **TILES["qkv"])
            casts.append([(sb_w_out, j)])
            o, cast = _sb_attention(qkv.reshape(bsz, s, -1), casts, **TILES["sb"])
        w_gate_up, w_down, w_out = cast[:3]
        h = _proj_res(o.reshape(t, -1), w_out, h, **TILES["proj"])
        h = _ffn(h, row(ffn_norm_w[layer]), w_gate_up, w_down, row(final_norm_w),
                 final_norm=layer == depth - 1, **TILES["ffn"])
    return h.reshape(bsz, s, d)
```

```python
import functools

import jax
import jax.numpy as jnp
from jax import lax
from jax.experimental import pallas as pl
from jax.experimental.pallas import tpu as pltpu

EPS = 1e-6
GLA_HEADS = 4
GLA_GATE_TAU = 16.0
SB_HEADS = 16
BF = jnp.bfloat16
F32 = jnp.float32
LANES = 128
BF16_SUBLANES = 16
MIB = 2 ** 20
V7X_VMEM_BYTES = 64 * MIB
VMEM_HEADROOM = 4 * MIB
VMEM_SPILL_ALLOWANCE = 2 * MIB

TILES = dict(
    inproj=dict(tm=1024, tn=768),
    gla=dict(chunk=128, sub=8, heads=GLA_HEADS),
    proj=dict(tm=1024),
    ffn=dict(tm=1024, tf=512),
    qkv=dict(tm=1024, tn=1024),
    sb=dict(tq=128, group=8),
)

_NT = (((1,), (1,)), ((), ()))
_TN = (((0,), (0,)), ((), ()))


def _nbytes(shape, dtype):
    n = jnp.dtype(dtype).itemsize
    for dim in shape:
        n *= dim or 1
    return n


class _Windows:
    def __init__(self):
        self.bytes = 0

    def spec(self, block, dtype, index_map, resident=False):
        self.bytes += _nbytes(block, dtype) // (2 if resident else 1)
        if resident:
            return pl.BlockSpec(block, index_map, pipeline_mode=pl.Buffered(1))
        return pl.BlockSpec(block, index_map)

    def params(self, sem, scratch=(), temps=0):
        need = 2 * self.bytes + sum(_nbytes(shape, dtype) for shape, dtype in scratch) + temps
        need = -(-(need + VMEM_SPILL_ALLOWANCE) // MIB) * MIB
        assert need <= V7X_VMEM_BYTES - VMEM_HEADROOM, need
        return pltpu.CompilerParams(dimension_semantics=sem, vmem_limit_bytes=need)


def _vmem(scratch):
    return [pltpu.VMEM(shape, dtype) for shape, dtype in scratch]


def _dot(a, b):
    return jnp.dot(a, b, preferred_element_type=F32)


def _rms(x, w):
    return x * lax.rsqrt(jnp.mean(x * x, axis=-1, keepdims=True) + EPS) * w


def _split2(x):
    hi = x.astype(BF)
    lo = (x - hi.astype(F32)).astype(BF)
    return hi, lo


def _log_sigmoid(x):
    return jnp.minimum(x, 0.0) - jnp.log(1.0 + jnp.exp(-jnp.abs(x)))


def _silu(x):
    return x * (1.0 / (1.0 + jnp.exp(-x)))


def _cast_specs(win, jobs, grid):
    steps = 1
    for g in grid:
        steps *= g

    def linear(idx):
        s = 0
        for g, i in zip(grid, idx):
            s = s * g + i
        return s

    operands, in_specs, out_specs, out_shapes, parts = [], [], [], [], []
    for job in jobs:
        n_rows = job[0][0].shape[-2]
        n_blk = max(n for n in range(1, steps + 1)
                    if n_rows % n == 0 and (n_rows // n) % BF16_SUBLANES == 0)
        rows = n_rows // n_blk
        blk = lambda *idx, n_blk=n_blk: linear(idx) * n_blk // steps
        for arr, lead in job:
            assert arr.shape[-2] == n_rows
            operands.append(arr)
            if lead is None:
                in_specs.append(win.spec((rows, arr.shape[-1]), F32, lambda *idx, blk=blk: (blk(*idx), 0)))
            else:
                in_specs.append(win.spec((None, rows, arr.shape[-1]), F32,
                                         lambda *idx, blk=blk, lead=lead: (lead, blk(*idx), 0)))
        width = sum(arr.shape[-1] for arr, _ in job)
        out_specs.append(win.spec((rows, width), BF, lambda *idx, blk=blk: (blk(*idx), 0)))
        out_shapes.append(jax.ShapeDtypeStruct((n_rows, width), BF))
        parts.append(len(job))
    return operands, in_specs, out_specs, out_shapes, parts


def _with_casts(body, n_in, n_out, parts):
    n_src = sum(parts)

    def kernel(*refs):
        ins, rest = refs[:n_in], refs[n_in:]
        srcs, rest = rest[:n_src], rest[n_src:]
        outs, rest = rest[:n_out], rest[n_out:]
        dsts, scratch = rest[:len(parts)], rest[len(parts):]
        body(*ins, *outs, *scratch)
        srcs = list(srcs)
        for dst, n_parts in zip(dsts, parts):
            col = 0
            for _ in range(n_parts):
                src = srcs.pop(0)
                dst[:, col:col + src.shape[-1]] = src[...].astype(BF)
                col += src.shape[-1]

    return kernel


def _norm_matmul2_kernel(x_ref, nwa_ref, nwb_ref, w_ref, o_ref, xn_ref, *, na, scale_a):
    j = pl.program_id(1)

    @pl.when(j == 0)
    def _():
        x = x_ref[...]
        xhat = x * lax.rsqrt(jnp.mean(x * x, axis=-1, keepdims=True) + EPS)
        xa = (xhat * nwa_ref[...]).astype(BF)
        xn_ref[0] = xa
        xn_ref[1] = (xhat * nwb_ref[...]).astype(BF)
        o_ref[...] = (_dot(xa, w_ref[...]) * scale_a).astype(o_ref.dtype)

    @pl.when((j > 0) & (j < na))
    def _():
        o_ref[...] = (_dot(xn_ref[0], w_ref[...]) * scale_a).astype(o_ref.dtype)

    @pl.when(j >= na)
    def _():
        o_ref[...] = _dot(xn_ref[1], w_ref[...]).astype(o_ref.dtype)


def _norm_matmul2(x, nwa, nwb, w, n_a, *, tm, tn, scale_a):
    t, d = x.shape
    n = w.shape[1]
    win = _Windows()
    scratch = [((2, tm, d), BF)]
    temps = 2 * _nbytes((tm, d), F32) + _nbytes((tm, tn), F32)
    return pl.pallas_call(
        functools.partial(_norm_matmul2_kernel, na=n_a // tn, scale_a=scale_a),
        grid=(t // tm, n // tn),
        in_specs=[
            win.spec((tm, d), F32, lambda i, j: (i, 0)),
            win.spec((1, d), F32, lambda i, j: (0, 0)),
            win.spec((1, d), F32, lambda i, j: (0, 0)),
            win.spec((d, tn), BF, lambda i, j: (0, j)),
        ],
        out_specs=win.spec((tm, tn), BF, lambda i, j: (i, j)),
        out_shape=jax.ShapeDtypeStruct((t, n), BF),
        scratch_shapes=_vmem(scratch),
        compiler_params=win.params(("parallel", "arbitrary"), scratch, temps),
        name="norm_matmul2",
    )(x, nwa, nwb, w)


def _gla_inproj_kernel(x_ref, nw_ref, w_ref, wgl_ref, wup_ref, bg_ref, o_ref, la_ref, xn_ref, gl_ref, *, rank):
    def dot_t(a, w_t):
        return lax.dot_general(a, w_t, _NT, preferred_element_type=F32)

    def gate(gl):
        up_hi, up_lo = _split2(wup_ref[...])
        group = lax.broadcasted_iota(jnp.int32, up_hi.shape, 0) // rank
        g = _dot(gl, jnp.where(group == 2, up_lo, up_hi)) + bg_ref[...]
        la_ref[...] = _log_sigmoid(g) * (1.0 / GLA_GATE_TAU)

    @pl.when(pl.program_id(1) == 0)
    def _():
        xn = _rms(x_ref[...], nw_ref[...]).astype(BF)
        xn_ref[...] = xn
        o_ref[...] = dot_t(xn, w_ref[...].astype(BF)).astype(o_ref.dtype)
        gl_hi, gl_lo = _split2(dot_t(xn, wgl_ref[...]))
        group = lax.broadcasted_iota(jnp.int32, gl_hi.shape, 1) // rank
        gl = jnp.where(group == 1, gl_lo, gl_hi)
        gl_ref[...] = gl
        gate(gl)

    @pl.when(pl.program_id(1) > 0)
    def _():
        gate(gl_ref[...])
        o_ref[...] = dot_t(xn_ref[...], w_ref[...].astype(BF)).astype(o_ref.dtype)


def _gla_inproj(x, nw, w_t_all, layer, n, rank, w_gate_up, bg, *, tm, tn):
    t, d = x.shape
    steps = n // tn
    qk_w = w_gate_up.shape[1]
    slab = qk_w // steps
    assert slab * steps == qk_w and slab % LANES == 0 and 3 * rank <= LANES
    three = lambda a: jnp.pad(jnp.tile(a, (3, 1)), ((0, LANES - 3 * rank), (0, 0)))
    wgl_t = three(w_t_all[layer, n:n + rank]).astype(BF)
    wup = three(w_gate_up)
    win = _Windows()
    scratch = [((tm, d), BF), ((tm, LANES), BF)]
    temps = _nbytes((tm, tn), F32) + _nbytes((tn, d), BF)
    return pl.pallas_call(
        functools.partial(_gla_inproj_kernel, rank=rank),
        grid=(t // tm, steps),
        in_specs=[
            win.spec((tm, d), F32, lambda i, j: (i, 0)),
            win.spec((1, d), F32, lambda i, j: (0, 0)),
            win.spec((None, tn, d), F32, lambda i, j: (layer, j, 0)),
            win.spec((LANES, d), BF, lambda i, j: (0, 0)),
            win.spec((LANES, slab), F32, lambda i, j: (0, j)),
            win.spec((1, slab), F32, lambda i, j: (0, j)),
        ],
        out_specs=[
            win.spec((tm, tn), BF, lambda i, j: (i, j)),
            win.spec((tm, slab), F32, lambda i, j: (i, j)),
        ],
        out_shape=[
            jax.ShapeDtypeStruct((t, n), BF),
            jax.ShapeDtypeStruct((t, qk_w), F32),
        ],
        scratch_shapes=_vmem(scratch),
        compiler_params=win.params(("parallel", "arbitrary"), scratch, temps),
        name="gla_inproj",
    )(x, nw, w_t_all, wgl_t, wup, bg)


def _gla_kernel(q_ref, k_ref, v_ref, r_ref, g_ref, gw_ref, o_ref, st_ref, qh_ref, kh_ref,
                *, chunk, sub, heads, scale):
    n_sub = chunk // sub
    dk = q_ref.shape[-1] // heads
    dv = v_ref.shape[-1] // heads
    hs = range(heads)

    @pl.when(pl.program_id(2) == 0)
    def _():
        st_ref[...] = jnp.zeros_like(st_ref)
        qh_ref[...] = jnp.zeros_like(qh_ref)
        kh_ref[...] = jnp.zeros_like(kh_ref)

    def head_cols(ref, h, width):
        return ref[0, :, h * width:(h + 1) * width]

    q = [head_cols(q_ref, h, dk).astype(F32) * scale for h in hs]
    k = [head_cols(k_ref, h, dk).astype(F32) for h in hs]
    g = [head_cols(g_ref, h, dk) for h in hs]

    row = lax.broadcasted_iota(jnp.int32, (chunk, chunk), 0)
    col = lax.broadcasted_iota(jnp.int32, (chunk, chunk), 1)
    tri = jnp.where(col <= row, 1.0, 0.0).astype(BF)
    diff = jnp.where(row // sub == col // sub, row - col, -1)

    b = []
    for h in hs:
        g_hi, g_lo = _split2(g[h])
        b.append(_dot(tri, g_hi) + _dot(tri, g_lo))

    st = [st_ref[h] for h in hs]
    o = [lax.dot_general((q[h] * jnp.exp(b[h])).astype(BF), st[h].astype(BF), _NT,
                         preferred_element_type=F32) for h in hs]

    def store_rows(ref, h, start, cols, x):
        lo = start // BF16_SUBLANES * BF16_SUBLANES
        hi = -(-(start + x.shape[0]) // BF16_SUBLANES) * BF16_SUBLANES
        pieces = [jnp.zeros((start - lo, x.shape[1]), F32)] if start > lo else []
        pieces.append(x)
        if hi > start + x.shape[0]:
            pieces.append(jnp.zeros((hi - start - x.shape[0], x.shape[1]), F32))
        x = jnp.concatenate(pieces, axis=0) if len(pieces) > 1 else x
        ref[h, lo:hi, cols] = x.astype(BF)

    scores = []
    for h in hs:
        k_run = None
        for i in range(1, n_sub):
            beta = b[h][i * sub - 1:i * sub, :]
            rows = slice(i * sub, (i + 1) * sub)
            prev = slice((i - 1) * sub, i * sub)
            cols = slice((i - 1) * dk, i * dk)
            store_rows(qh_ref, h, i * sub, cols, q[h][rows] * jnp.exp(b[h][rows] - beta))
            k_new = k[h][prev] * jnp.exp(beta - b[h][prev])
            if k_run is None:
                k_run = k_new
            else:
                k_run = jnp.concatenate([k_run * jnp.exp(beta - beta_prev), k_new], axis=0)
            store_rows(kh_ref, h, 0, cols, k_run)
            beta_prev = beta
        scores.append(lax.dot_general(qh_ref[h], kh_ref[h], _NT, preferred_element_type=F32))

    for h in hs:
        decay = jnp.exp(g[h])
        w = k[h]
        for d in range(sub):
            if d > 0:
                w = decay * pltpu.roll(w, 1, 0)
            scores[h] = jnp.where(diff == d, jnp.sum(q[h] * w, axis=-1, keepdims=True), scores[h])

    v = [head_cols(v_ref, h, dv) for h in hs]
    o = [o[h] + _dot(scores[h].astype(BF), v[h]) for h in hs]

    for h in hs:
        b_last = b[h][chunk - 1:chunk, :]
        k_dec = (k[h] * jnp.exp(b_last - b[h])).astype(BF)
        st_ref[h] = st[h] * jnp.exp(b_last) + lax.dot_general(v[h], k_dec, _TN, preferred_element_type=F32)

    for h in hs:
        r = head_cols(r_ref, h, dv).astype(F32)
        o_ref[0, :, h * dv:(h + 1) * dv] = (_rms(o[h], gw_ref[...]) * _silu(r)).astype(o_ref.dtype)


def _gla_core(proj, la, gw, cast_jobs, *, chunk, sub, heads):
    bsz, s, _ = proj.shape
    dk = la.shape[-1] // GLA_HEADS
    dv = gw.shape[-1]
    groups = GLA_HEADS // heads
    k_blocks = groups
    v_blocks = 2 * GLA_HEADS * dk // (heads * dv)
    r_blocks = v_blocks + groups
    expanded = (chunk // sub - 1) * dk
    grid = (bsz, groups, s // chunk)
    win = _Windows()
    c_ops, c_in, c_out, c_shapes, c_parts = _cast_specs(win, cast_jobs, grid)
    scratch = [((heads, dv, dk), F32), ((heads, chunk, expanded), BF), ((heads, chunk, expanded), BF)]
    temps = heads * (6 * _nbytes((chunk, dk), F32) + 2 * _nbytes((chunk, dv), F32))
    body = functools.partial(_gla_kernel, chunk=chunk, sub=sub, heads=heads, scale=dk ** -0.5)
    outs = pl.pallas_call(
        _with_casts(body, 6, 1, c_parts),
        grid=grid,
        in_specs=[
            win.spec((1, chunk, heads * dk), BF, lambda b, h, n: (b, n, h)),
            win.spec((1, chunk, heads * dk), BF, lambda b, h, n: (b, n, k_blocks + h)),
            win.spec((1, chunk, heads * dv), BF, lambda b, h, n: (b, n, v_blocks + h)),
            win.spec((1, chunk, heads * dv), BF, lambda b, h, n: (b, n, r_blocks + h)),
            win.spec((1, chunk, heads * dk), F32, lambda b, h, n: (b, n, h)),
            win.spec((1, dv), F32, lambda b, h, n: (0, 0)),
        ] + c_in,
        out_specs=[win.spec((1, chunk, heads * dv), BF, lambda b, h, n: (b, n, h))] + c_out,
        out_shape=[jax.ShapeDtypeStruct((bsz, s, GLA_HEADS * dv), BF)] + c_shapes,
        scratch_shapes=_vmem(scratch),
        compiler_params=win.params(("arbitrary", "arbitrary", "arbitrary"), scratch, temps),
        name="gla_core",
    )(proj, proj, proj, proj, la, gw, *c_ops)
    return outs[0], outs[1:]


def _proj_res_kernel(x_ref, w_ref, res_ref, o_ref):
    o_ref[...] = res_ref[...] + _dot(x_ref[...], w_ref[...])


def _proj_res(x, w, res, *, tm):
    t, kdim = x.shape
    n = w.shape[1]
    win = _Windows()
    return pl.pallas_call(
        _proj_res_kernel,
        grid=(t // tm,),
        in_specs=[
            win.spec((tm, kdim), BF, lambda i: (i, 0)),
            win.spec((kdim, n), BF, lambda i: (0, 0), resident=True),
            win.spec((tm, n), F32, lambda i: (i, 0)),
        ],
        out_specs=win.spec((tm, n), F32, lambda i: (i, 0)),
        out_shape=jax.ShapeDtypeStruct((t, n), F32),
        compiler_params=win.params(("parallel",), temps=_nbytes((tm, n), F32)),
        name="proj_res",
    )(x, w, res)


def _ffn_kernel(h_ref, nw_ref, wg_ref, wu_ref, wd_ref, fw_ref, o_ref, xn_ref, *, final_norm):
    f = pl.program_id(1)

    def step(acc_ref):
        xn = xn_ref[...]
        act = (_silu(_dot(xn, wg_ref[...])) * _dot(xn, wu_ref[...])).astype(BF)
        o_ref[...] = acc_ref[...] + _dot(act, wd_ref[...])

    @pl.when(f == 0)
    def _():
        xn_ref[...] = _rms(h_ref[...], nw_ref[...]).astype(BF)
        step(h_ref)

    @pl.when(f > 0)
    def _():
        step(o_ref)

    if final_norm:
        @pl.when(f == pl.num_programs(1) - 1)
        def _():
            o_ref[...] = _rms(o_ref[...], fw_ref[...])


def _ffn(h, nw, w_gate_up, w_down, fw, *, tm, tf, final_norm):
    t, d = h.shape
    d_ff = w_down.shape[0]
    nf = d_ff // tf
    win = _Windows()
    scratch = [((tm, d), BF)]
    temps = 2 * _nbytes((tm, tf), F32) + _nbytes((tm, tf), BF)
    return pl.pallas_call(
        functools.partial(_ffn_kernel, final_norm=final_norm),
        grid=(t // tm, nf),
        in_specs=[
            win.spec((tm, d), F32, lambda i, f: (i, 0)),
            win.spec((1, d), F32, lambda i, f: (0, 0)),
            win.spec((d, tf), BF, lambda i, f: (0, f)),
            win.spec((d, tf), BF, lambda i, f: (0, nf + f)),
            win.spec((tf, d), BF, lambda i, f: (f, 0)),
            win.spec((1, d), F32, lambda i, f: (0, 0)),
        ],
        out_specs=win.spec((tm, d), F32, lambda i, f: (i, 0)),
        out_shape=jax.ShapeDtypeStruct((t, d), F32),
        scratch_shapes=_vmem(scratch),
        compiler_params=win.params(("parallel", "arbitrary"), scratch, temps),
        name="ffn",
    )(h, nw, w_gate_up, w_gate_up, w_down, fw)


_EXP2_UNDERFLOW = -127.0
LOG2_E = 1.4426950408889634


def _sb_kernel(q_ref, k_ref, v_ref, o_ref, *, tq, group):
    s = q_ref.shape[1]
    tw = 2 * tq
    def later(n):
        r = lax.broadcasted_iota(jnp.int32, (n, n), 0)
        c = lax.broadcasted_iota(jnp.int32, (n, n), 1)
        return jnp.where(r > c, 1.0, 0.0).astype(BF)

    def causal(n_keys, offset):
        qr = lax.broadcasted_iota(jnp.int32, (tq, n_keys), 0)
        kc = lax.broadcasted_iota(jnp.int32, (tq, n_keys), 1)
        return kc < qr + offset

    later_w, later_b = later(tw), later(tq)
    diag_mask = causal(tq, 0)
    win_mask = causal(tw, tq)

    def scores(q, start, width, mask):
        kb = k_ref[0, pl.ds(start, width), :]
        z = lax.dot_general(q, kb, _NT, preferred_element_type=F32)
        nz = -z
        lf = jnp.minimum(nz, 0.0) - jnp.log2(1.0 + jnp.exp2(jnp.minimum(z, nz)))
        if mask is not None:
            lf = jnp.where(mask, lf, 0.0)
        return z, lf

    def suffix(lf, later, carry):
        after = _dot(lf.astype(BF), later)
        if carry is not None:
            after = after + carry
        return after, after[:, :1] + lf[:, :1]

    def weighted(z, lf, after, start, width, mask):
        a = jnp.exp2(z + lf + after)
        if mask is not None:
            a = jnp.where(mask, a, 0.0)
        return _dot(a.astype(BF), v_ref[0, pl.ds(start, width), :])

    def window(tile):
        if tile == 0:
            return 0, tq, later_b, diag_mask
        return (tile - 1) * tq, tw, later_w, win_mask

    def extend(tile, q, carry, top, acc):
        def cond(st):
            return jnp.logical_and(st[0] >= 0, st[1] > _EXP2_UNDERFLOW)

        def body(st):
            j, _, carry, acc = st
            start = pl.multiple_of(j * tq, tq)
            z, lf = scores(q, start, tq, None)
            after, carry = suffix(lf, later_b, carry)
            acc = acc + weighted(z, lf, after, start, tq, None)
            return j - 1, jnp.max(carry), carry, acc

        init = (jnp.asarray(tile - 2, jnp.int32), top, carry, acc)
        return lax.while_loop(cond, body, init)[3]

    def first_passes(base):
        tiles = [base + g for g in range(group)]
        wins = [window(t) for t in tiles]
        qs = [q_ref[0, t * tq:(t + 1) * tq, :] for t in tiles]
        zl = [scores(q, w[0], w[1], w[3]) for q, w in zip(qs, wins)]
        ac = [suffix(lf, w[2], None) for (_, lf), w in zip(zl, wins)]
        accs = [weighted(z, lf, after, w[0], w[1], w[3]) for (z, lf), (after, _), w in zip(zl, ac, wins)]
        tops = [jnp.max(carry) for _, carry in ac]
        return list(zip(tiles, qs, [carry for _, carry in ac], tops, accs))

    todo = []
    for base in range(0, s // tq, group):
        todo += first_passes(base)
    for t, q, carry, top, acc in todo:
        acc = extend(t, q, carry, top, acc)
        o_ref[0, t * tq:(t + 1) * tq, :] = acc.astype(o_ref.dtype)


def _sb_attention(qkv, cast_jobs, *, tq, group):
    bsz, s, w3 = qkv.shape
    w = w3 // 3
    hd = w // SB_HEADS
    grid = (bsz, SB_HEADS)
    win = _Windows()
    c_ops, c_in, c_out, c_shapes, c_parts = _cast_specs(win, cast_jobs, grid)
    temps = group * 8 * _nbytes((tq, 2 * tq), F32)
    outs = pl.pallas_call(
        _with_casts(functools.partial(_sb_kernel, tq=tq, group=group), 3, 1, c_parts),
        grid=grid,
        in_specs=[
            win.spec((1, s, hd), BF, lambda b, h: (b, 0, h)),
            win.spec((1, s, hd), BF, lambda b, h: (b, 0, SB_HEADS + h)),
            win.spec((1, s, hd), BF, lambda b, h: (b, 0, 2 * SB_HEADS + h)),
        ] + c_in,
        out_specs=[win.spec((1, s, hd), BF, lambda b, h: (b, 0, h))] + c_out,
        out_shape=[jax.ShapeDtypeStruct((bsz, s, w), BF)] + c_shapes,
        compiler_params=win.params(("arbitrary", "arbitrary"), temps=temps),
        name="sb_attn",
    )(qkv, qkv, qkv, *c_ops)
    return outs[0], outs[1:]


def kernel(x, attn_norm_w, ffn_norm_w, gla_w_in, gla_w_gate_up, gla_b_gate, gla_gnorm_w, gla_w_out,
           kv_norm_w, sb_w_kv, sb_w_q, sb_w_out, ffn_w_gate_up, ffn_w_down, final_norm_w):
    bsz, s, d = x.shape
    t = bsz * s
    depth = attn_norm_w.shape[0]
    n_gla = gla_w_in.shape[0]
    rank = gla_w_gate_up.shape[1]
    main_w = gla_w_in.shape[2] - rank
    hd = d // SB_HEADS
    row = lambda v: v.reshape(1, -1)
    assert n_gla >= 1 and depth - n_gla == 1

    w_in_t = jnp.swapaxes(gla_w_in, 1, 2)

    h = x.reshape(t, d)
    for layer in range(depth):
        casts = [[(ffn_w_gate_up, layer)], [(ffn_w_down, layer)]]
        if layer < n_gla:
            proj, la = _gla_inproj(h, row(attn_norm_w[layer]), w_in_t, layer, main_w, rank,
                                   gla_w_gate_up[layer], row(gla_b_gate[layer]), **TILES["inproj"])
            casts.append([(gla_w_out, layer)])
            if layer == n_gla - 1:
                casts.append([(sb_w_q, 0), (sb_w_kv, None)])
            o, cast = _gla_core(proj.reshape(bsz, s, main_w), la.reshape(bsz, s, -1),
                                row(gla_gnorm_w[layer]), casts, **TILES["gla"])
            if layer == n_gla - 1:
                w_qkv = cast[3]
        else:
            j = layer - n_gla
            qkv = _norm_matmul2(h, row(attn_norm_w[layer]), row(kv_norm_w), w_qkv, sb_w_q.shape[2],
                                scale_a=hd ** -0.5 * LOG2_E, **TILES["qkv"])
            casts.append([(sb_w_out, j)])
            o, cast = _sb_attention(qkv.reshape(bsz, s, -1), casts, **TILES["sb"])
        w_gate_up, w_down, w_out = cast[:3]
        h = _proj_res(o.reshape(t, -1), w_out, h, **TILES["proj"])
        h = _ffn(h, row(ffn_norm_w[layer]), w_gate_up, w_down, row(final_norm_w),
                 final_norm=layer == depth - 1, **TILES["ffn"])
    return h.reshape(bsz, s, d)
```

```python
import functools

import jax
import jax.numpy as jnp
from jax import lax
from jax.experimental import pallas as pl
from jax.experimental.pallas import tpu as pltpu

EPS = 1e-6
GLA_HEADS = 4
GLA_GATE_TAU = 16.0
SB_HEADS = 16
BF = jnp.bfloat16
F32 = jnp.float32
LANES = 128
BF16_SUBLANES = 16
MIB = 2 ** 20
V7X_VMEM_BYTES = 64 * MIB
VMEM_HEADROOM = 4 * MIB
VMEM_SPILL_ALLOWANCE = 2 * MIB

TILES = dict(
    inproj=dict(tm=1024, tn=768),
    gla=dict(chunk=128, sub=8, heads=GLA_HEADS),
    proj=dict(tm=512),
    ffn=dict(tm=1024, tf=512),
    qkv=dict(tm=1024, tn=1024),
    sb=dict(tq=128, group=8),
)

_NT = (((1,), (1,)), ((), ()))
_TN = (((0,), (0,)), ((), ()))


def _nbytes(shape, dtype):
    n = jnp.dtype(dtype).itemsize
    for dim in shape:
        n *= dim or 1
    return n


class _Windows:
    def __init__(self):
        self.bytes = 0

    def spec(self, block, dtype, index_map):
        self.bytes += _nbytes(block, dtype)
        return pl.BlockSpec(block, index_map)

    def params(self, sem, scratch=(), temps=0):
        need = 2 * self.bytes + sum(_nbytes(shape, dtype) for shape, dtype in scratch) + temps
        need = -(-(need + VMEM_SPILL_ALLOWANCE) // MIB) * MIB
        assert need <= V7X_VMEM_BYTES - VMEM_HEADROOM, need
        return pltpu.CompilerParams(dimension_semantics=sem, vmem_limit_bytes=need)


def _vmem(scratch):
    return [pltpu.VMEM(shape, dtype) for shape, dtype in scratch]


def _dot(a, b):
    return jnp.dot(a, b, preferred_element_type=F32)


def _rms(x, w):
    return x * lax.rsqrt(jnp.mean(x * x, axis=-1, keepdims=True) + EPS) * w


def _split2(x):
    hi = x.astype(BF)
    lo = (x - hi.astype(F32)).astype(BF)
    return hi, lo


def _log_sigmoid(x):
    return jnp.minimum(x, 0.0) - jnp.log(1.0 + jnp.exp(-jnp.abs(x)))


def _silu(x):
    return x * (1.0 / (1.0 + jnp.exp(-x)))


def _cast_specs(win, jobs, grid):
    steps = 1
    for g in grid:
        steps *= g

    def linear(idx):
        s = 0
        for g, i in zip(grid, idx):
            s = s * g + i
        return s

    operands, in_specs, out_specs, out_shapes, parts = [], [], [], [], []
    for job in jobs:
        n_rows = job[0][0].shape[-2]
        n_blk = max(n for n in range(1, steps + 1)
                    if n_rows % n == 0 and (n_rows // n) % BF16_SUBLANES == 0)
        rows = n_rows // n_blk
        blk = lambda *idx, n_blk=n_blk: linear(idx) * n_blk // steps
        for arr, lead in job:
            assert arr.shape[-2] == n_rows
            operands.append(arr)
            if lead is None:
                in_specs.append(win.spec((rows, arr.shape[-1]), F32, lambda *idx, blk=blk: (blk(*idx), 0)))
            else:
                in_specs.append(win.spec((None, rows, arr.shape[-1]), F32,
                                         lambda *idx, blk=blk, lead=lead: (lead, blk(*idx), 0)))
        width = sum(arr.shape[-1] for arr, _ in job)
        out_specs.append(win.spec((rows, width), BF, lambda *idx, blk=blk: (blk(*idx), 0)))
        out_shapes.append(jax.ShapeDtypeStruct((n_rows, width), BF))
        parts.append(len(job))
    return operands, in_specs, out_specs, out_shapes, parts


def _with_casts(body, n_in, n_out, parts):
    n_src = sum(parts)

    def kernel(*refs):
        ins, rest = refs[:n_in], refs[n_in:]
        srcs, rest = rest[:n_src], rest[n_src:]
        outs, rest = rest[:n_out], rest[n_out:]
        dsts, scratch = rest[:len(parts)], rest[len(parts):]
        body(*ins, *outs, *scratch)
        srcs = list(srcs)
        for dst, n_parts in zip(dsts, parts):
            col = 0
            for _ in range(n_parts):
                src = srcs.pop(0)
                dst[:, col:col + src.shape[-1]] = src[...].astype(BF)
                col += src.shape[-1]

    return kernel


def _norm_matmul2_kernel(x_ref, nwa_ref, nwb_ref, w_ref, o_ref, xn_ref, *, na, scale_a):
    j = pl.program_id(1)

    @pl.when(j == 0)
    def _():
        x = x_ref[...]
        xhat = x * lax.rsqrt(jnp.mean(x * x, axis=-1, keepdims=True) + EPS)
        xa = (xhat * nwa_ref[...]).astype(BF)
        xn_ref[0] = xa
        xn_ref[1] = (xhat * nwb_ref[...]).astype(BF)
        o_ref[...] = (_dot(xa, w_ref[...]) * scale_a).astype(o_ref.dtype)

    @pl.when((j > 0) & (j < na))
    def _():
        o_ref[...] = (_dot(xn_ref[0], w_ref[...]) * scale_a).astype(o_ref.dtype)

    @pl.when(j >= na)
    def _():
        o_ref[...] = _dot(xn_ref[1], w_ref[...]).astype(o_ref.dtype)


def _norm_matmul2(x, nwa, nwb, w, n_a, *, tm, tn, scale_a):
    t, d = x.shape
    n = w.shape[1]
    win = _Windows()
    scratch = [((2, tm, d), BF)]
    temps = 2 * _nbytes((tm, d), F32) + _nbytes((tm, tn), F32)
    return pl.pallas_call(
        functools.partial(_norm_matmul2_kernel, na=n_a // tn, scale_a=scale_a),
        grid=(t // tm, n // tn),
        in_specs=[
            win.spec((tm, d), F32, lambda i, j: (i, 0)),
            win.spec((1, d), F32, lambda i, j: (0, 0)),
            win.spec((1, d), F32, lambda i, j: (0, 0)),
            win.spec((d, tn), BF, lambda i, j: (0, j)),
        ],
        out_specs=win.spec((tm, tn), BF, lambda i, j: (i, j)),
        out_shape=jax.ShapeDtypeStruct((t, n), BF),
        scratch_shapes=_vmem(scratch),
        compiler_params=win.params(("parallel", "arbitrary"), scratch, temps),
        name="norm_matmul2",
    )(x, nwa, nwb, w)


def _gla_inproj_kernel(x_ref, nw_ref, w_ref, wgl_ref, wup_ref, bg_ref, o_ref, la_ref, xn_ref, gl_ref, *, rank):
    def dot_t(a, w_t):
        return lax.dot_general(a, w_t, _NT, preferred_element_type=F32)

    def gate(gl):
        up_hi, up_lo = _split2(wup_ref[...])
        group = lax.broadcasted_iota(jnp.int32, up_hi.shape, 0) // rank
        g = _dot(gl, jnp.where(group == 2, up_lo, up_hi)) + bg_ref[...]
        la_ref[...] = _log_sigmoid(g) * (1.0 / GLA_GATE_TAU)

    @pl.when(pl.program_id(1) == 0)
    def _():
        xn = _rms(x_ref[...], nw_ref[...]).astype(BF)
        xn_ref[...] = xn
        o_ref[...] = dot_t(xn, w_ref[...].astype(BF)).astype(o_ref.dtype)
        gl_hi, gl_lo = _split2(dot_t(xn, wgl_ref[...]))
        group = lax.broadcasted_iota(jnp.int32, gl_hi.shape, 1) // rank
        gl = jnp.where(group == 1, gl_lo, gl_hi)
        gl_ref[...] = gl
        gate(gl)

    @pl.when(pl.program_id(1) > 0)
    def _():
        gate(gl_ref[...])
        o_ref[...] = dot_t(xn_ref[...], w_ref[...].astype(BF)).astype(o_ref.dtype)


def _gla_inproj(x, nw, w_t_all, layer, n, rank, w_gate_up, bg, *, tm, tn):
    t, d = x.shape
    steps = n // tn
    qk_w = w_gate_up.shape[1]
    slab = qk_w // steps
    assert slab * steps == qk_w and slab % LANES == 0 and 3 * rank <= LANES
    three = lambda a: jnp.pad(jnp.tile(a, (3, 1)), ((0, LANES - 3 * rank), (0, 0)))
    wgl_t = three(w_t_all[layer, n:n + rank]).astype(BF)
    wup = three(w_gate_up)
    win = _Windows()
    scratch = [((tm, d), BF), ((tm, LANES), BF)]
    temps = _nbytes((tm, tn), F32) + _nbytes((tn, d), BF)
    return pl.pallas_call(
        functools.partial(_gla_inproj_kernel, rank=rank),
        grid=(t // tm, steps),
        in_specs=[
            win.spec((tm, d), F32, lambda i, j: (i, 0)),
            win.spec((1, d), F32, lambda i, j: (0, 0)),
            win.spec((None, tn, d), F32, lambda i, j: (layer, j, 0)),
            win.spec((LANES, d), BF, lambda i, j: (0, 0)),
            win.spec((LANES, slab), F32, lambda i, j: (0, j)),
            win.spec((1, slab), F32, lambda i, j: (0, j)),
        ],
        out_specs=[
            win.spec((tm, tn), BF, lambda i, j: (i, j)),
            win.spec((tm, slab), F32, lambda i, j: (i, j)),
        ],
        out_shape=[
            jax.ShapeDtypeStruct((t, n), BF),
            jax.ShapeDtypeStruct((t, qk_w), F32),
        ],
        scratch_shapes=_vmem(scratch),
        compiler_params=win.params(("parallel", "arbitrary"), scratch, temps),
        name="gla_inproj",
    )(x, nw, w_t_all, wgl_t, wup, bg)


def _gla_kernel(q_ref, k_ref, v_ref, r_ref, g_ref, gw_ref, o_ref, st_ref, qh_ref, kh_ref,
                *, chunk, sub, heads, scale):
    n_sub = chunk // sub
    dk = q_ref.shape[-1] // heads
    dv = v_ref.shape[-1] // heads
    hs = range(heads)

    @pl.when(pl.program_id(2) == 0)
    def _():
        st_ref[...] = jnp.zeros_like(st_ref)
        qh_ref[...] = jnp.zeros_like(qh_ref)
        kh_ref[...] = jnp.zeros_like(kh_ref)

    def head_cols(ref, h, width):
        return ref[0, :, h * width:(h + 1) * width]

    q = [head_cols(q_ref, h, dk).astype(F32) * scale for h in hs]
    k = [head_cols(k_ref, h, dk).astype(F32) for h in hs]
    g = [head_cols(g_ref, h, dk) for h in hs]

    row = lax.broadcasted_iota(jnp.int32, (chunk, chunk), 0)
    col = lax.broadcasted_iota(jnp.int32, (chunk, chunk), 1)
    tri = jnp.where(col <= row, 1.0, 0.0).astype(BF)
    diff = jnp.where(row // sub == col // sub, row - col, -1)

    b = []
    for h in hs:
        g_hi, g_lo = _split2(g[h])
        b.append(_dot(tri, g_hi) + _dot(tri, g_lo))

    st = [st_ref[h] for h in hs]
    o = [lax.dot_general((q[h] * jnp.exp(b[h])).astype(BF), st[h].astype(BF), _NT,
                         preferred_element_type=F32) for h in hs]

    def store_rows(ref, h, start, cols, x):
        lo = start // BF16_SUBLANES * BF16_SUBLANES
        hi = -(-(start + x.shape[0]) // BF16_SUBLANES) * BF16_SUBLANES
        pieces = [jnp.zeros((start - lo, x.shape[1]), F32)] if start > lo else []
        pieces.append(x)
        if hi > start + x.shape[0]:
            pieces.append(jnp.zeros((hi - start - x.shape[0], x.shape[1]), F32))
        x = jnp.concatenate(pieces, axis=0) if len(pieces) > 1 else x
        ref[h, lo:hi, cols] = x.astype(BF)

    scores = []
    for h in hs:
        k_run = None
        for i in range(1, n_sub):
            beta = b[h][i * sub - 1:i * sub, :]
            rows = slice(i * sub, (i + 1) * sub)
            prev = slice((i - 1) * sub, i * sub)
            cols = slice((i - 1) * dk, i * dk)
            store_rows(qh_ref, h, i * sub, cols, q[h][rows] * jnp.exp(b[h][rows] - beta))
            k_new = k[h][prev] * jnp.exp(beta - b[h][prev])
            if k_run is None:
                k_run = k_new
            else:
                k_run = jnp.concatenate([k_run * jnp.exp(beta - beta_prev), k_new], axis=0)
            store_rows(kh_ref, h, 0, cols, k_run)
            beta_prev = beta
        scores.append(lax.dot_general(qh_ref[h], kh_ref[h], _NT, preferred_element_type=F32))

    for h in hs:
        decay = jnp.exp(g[h])
        w = k[h]
        for d in range(sub):
            if d > 0:
                w = decay * pltpu.roll(w, 1, 0)
            scores[h] = jnp.where(diff == d, jnp.sum(q[h] * w, axis=-1, keepdims=True), scores[h])

    v = [head_cols(v_ref, h, dv) for h in hs]
    o = [o[h] + _dot(scores[h].astype(BF), v[h]) for h in hs]

    for h in hs:
        b_last = b[h][chunk - 1:chunk, :]
        k_dec = (k[h] * jnp.exp(b_last - b[h])).astype(BF)
        st_ref[h] = st[h] * jnp.exp(b_last) + lax.dot_general(v[h], k_dec, _TN, preferred_element_type=F32)

    for h in hs:
        r = head_cols(r_ref, h, dv).astype(F32)
        o_ref[0, :, h * dv:(h + 1) * dv] = (_rms(o[h], gw_ref[...]) * _silu(r)).astype(o_ref.dtype)


def _gla_core(proj, la, gw, cast_jobs, *, chunk, sub, heads):
    bsz, s, _ = proj.shape
    dk = la.shape[-1] // GLA_HEADS
    dv = gw.shape[-1]
    groups = GLA_HEADS // heads
    k_blocks = groups
    v_blocks = 2 * GLA_HEADS * dk // (heads * dv)
    r_blocks = v_blocks + groups
    expanded = (chunk // sub - 1) * dk
    grid = (bsz, groups, s // chunk)
    win = _Windows()
    c_ops, c_in, c_out, c_shapes, c_parts = _cast_specs(win, cast_jobs, grid)
    scratch = [((heads, dv, dk), F32), ((heads, chunk, expanded), BF), ((heads, chunk, expanded), BF)]
    temps = heads * (6 * _nbytes((chunk, dk), F32) + 2 * _nbytes((chunk, dv), F32))
    body = functools.partial(_gla_kernel, chunk=chunk, sub=sub, heads=heads, scale=dk ** -0.5)
    outs = pl.pallas_call(
        _with_casts(body, 6, 1, c_parts),
        grid=grid,
        in_specs=[
            win.spec((1, chunk, heads * dk), BF, lambda b, h, n: (b, n, h)),
            win.spec((1, chunk, heads * dk), BF, lambda b, h, n: (b, n, k_blocks + h)),
            win.spec((1, chunk, heads * dv), BF, lambda b, h, n: (b, n, v_blocks + h)),
            win.spec((1, chunk, heads * dv), BF, lambda b, h, n: (b, n, r_blocks + h)),
            win.spec((1, chunk, heads * dk), F32, lambda b, h, n: (b, n, h)),
            win.spec((1, dv), F32, lambda b, h, n: (0, 0)),
        ] + c_in,
        out_specs=[win.spec((1, chunk, heads * dv), BF, lambda b, h, n: (b, n, h))] + c_out,
        out_shape=[jax.ShapeDtypeStruct((bsz, s, GLA_HEADS * dv), BF)] + c_shapes,
        scratch_shapes=_vmem(scratch),
        compiler_params=win.params(("arbitrary", "arbitrary", "arbitrary"), scratch, temps),
        name="gla_core",
    )(proj, proj, proj, proj, la, gw, *c_ops)
    return outs[0], outs[1:]


def _proj_res_kernel(x_ref, w_ref, res_ref, o_ref):
    o_ref[...] = res_ref[...] + _dot(x_ref[...], w_ref[...])


def _proj_res(x, w, res, *, tm):
    t, kdim = x.shape
    n = w.shape[1]
    win = _Windows()
    return pl.pallas_call(
        _proj_res_kernel,
        grid=(t // tm,),
        in_specs=[
            win.spec((tm, kdim), BF, lambda i: (i, 0)),
            win.spec((kdim, n), BF, lambda i: (0, 0)),
            win.spec((tm, n), F32, lambda i: (i, 0)),
        ],
        out_specs=win.spec((tm, n), F32, lambda i: (i, 0)),
        out_shape=jax.ShapeDtypeStruct((t, n), F32),
        compiler_params=win.params(("parallel",), temps=_nbytes((tm, n), F32)),
        name="proj_res",
    )(x, w, res)


def _ffn_kernel(h_ref, nw_ref, wg_ref, wu_ref, wd_ref, fw_ref, o_ref, xn_ref, *, final_norm):
    f = pl.program_id(1)

    def step(acc_ref):
        xn = xn_ref[...]
        act = (_silu(_dot(xn, wg_ref[...])) * _dot(xn, wu_ref[...])).astype(BF)
        o_ref[...] = acc_ref[...] + _dot(act, wd_ref[...])

    @pl.when(f == 0)
    def _():
        xn_ref[...] = _rms(h_ref[...], nw_ref[...]).astype(BF)
        step(h_ref)

    @pl.when(f > 0)
    def _():
        step(o_ref)

    if final_norm:
        @pl.when(f == pl.num_programs(1) - 1)
        def _():
            o_ref[...] = _rms(o_ref[...], fw_ref[...])


def _ffn(h, nw, w_gate_up, w_down, fw, *, tm, tf, final_norm):
    t, d = h.shape
    d_ff = w_down.shape[0]
    nf = d_ff // tf
    win = _Windows()
    scratch = [((tm, d), BF)]
    temps = 2 * _nbytes((tm, tf), F32) + _nbytes((tm, tf), BF)
    return pl.pallas_call(
        functools.partial(_ffn_kernel, final_norm=final_norm),
        grid=(t // tm, nf),
        in_specs=[
            win.spec((tm, d), F32, lambda i, f: (i, 0)),
            win.spec((1, d), F32, lambda i, f: (0, 0)),
            win.spec((d, tf), BF, lambda i, f: (0, f)),
            win.spec((d, tf), BF, lambda i, f: (0, nf + f)),
            win.spec((tf, d), BF, lambda i, f: (f, 0)),
            win.spec((1, d), F32, lambda i, f: (0, 0)),
        ],
        out_specs=win.spec((tm, d), F32, lambda i, f: (i, 0)),
        out_shape=jax.ShapeDtypeStruct((t, d), F32),
        scratch_shapes=_vmem(scratch),
        compiler_params=win.params(("parallel", "arbitrary"), scratch, temps),
        name="ffn",
    )(h, nw, w_gate_up, w_gate_up, w_down, fw)


_EXP2_UNDERFLOW = -127.0
LOG2_E = 1.4426950408889634


def _sb_kernel(q_ref, k_ref, v_ref, o_ref, *, tq, group):
    s = q_ref.shape[1]
    tw = 2 * tq
    def later(n):
        r = lax.broadcasted_iota(jnp.int32, (n, n), 0)
        c = lax.broadcasted_iota(jnp.int32, (n, n), 1)
        return jnp.where(r > c, 1.0, 0.0).astype(BF)

    def causal(n_keys, offset):
        qr = lax.broadcasted_iota(jnp.int32, (tq, n_keys), 0)
        kc = lax.broadcasted_iota(jnp.int32, (tq, n_keys), 1)
        return kc < qr + offset

    later_w, later_b = later(tw), later(tq)
    diag_mask = causal(tq, 0)
    win_mask = causal(tw, tq)

    def scores(q, start, width, mask):
        kb = k_ref[0, pl.ds(start, width), :]
        z = lax.dot_general(q, kb, _NT, preferred_element_type=F32)
        nz = -z
        lf = jnp.minimum(nz, 0.0) - jnp.log2(1.0 + jnp.exp2(jnp.minimum(z, nz)))
        if mask is not None:
            lf = jnp.where(mask, lf, 0.0)
        return z, lf

    def suffix(lf, later, carry):
        after = _dot(lf.astype(BF), later)
        if carry is not None:
            after = after + carry
        return after, after[:, :1] + lf[:, :1]

    def weighted(z, lf, after, start, width, mask):
        a = jnp.exp2(z + lf + after)
        if mask is not None:
            a = jnp.where(mask, a, 0.0)
        return _dot(a.astype(BF), v_ref[0, pl.ds(start, width), :])

    def window(tile):
        if tile == 0:
            return 0, tq, later_b, diag_mask
        return (tile - 1) * tq, tw, later_w, win_mask

    def extend(tile, q, carry, top, acc):
        def cond(st):
            return jnp.logical_and(st[0] >= 0, st[1] > _EXP2_UNDERFLOW)

        def body(st):
            j, _, carry, acc = st
            start = pl.multiple_of(j * tq, tq)
            z, lf = scores(q, start, tq, None)
            after, carry = suffix(lf, later_b, carry)
            acc = acc + weighted(z, lf, after, start, tq, None)
            return j - 1, jnp.max(carry), carry, acc

        init = (jnp.asarray(tile - 2, jnp.int32), top, carry, acc)
        return lax.while_loop(cond, body, init)[3]

    def first_passes(base):
        tiles = [base + g for g in range(group)]
        wins = [window(t) for t in tiles]
        qs = [q_ref[0, t * tq:(t + 1) * tq, :] for t in tiles]
        zl = [scores(q, w[0], w[1], w[3]) for q, w in zip(qs, wins)]
        ac = [suffix(lf, w[2], None) for (_, lf), w in zip(zl, wins)]
        accs = [weighted(z, lf, after, w[0], w[1], w[3]) for (z, lf), (after, _), w in zip(zl, ac, wins)]
        tops = [jnp.max(carry) for _, carry in ac]
        return list(zip(tiles, qs, [carry for _, carry in ac], tops, accs))

    todo = []
    for base in range(0, s // tq, group):
        todo += first_passes(base)
    for t, q, carry, top, acc in todo:
        acc = extend(t, q, carry, top, acc)
        o_ref[0, t * tq:(t + 1) * tq, :] = acc.astype(o_ref.dtype)


def _sb_attention(qkv, cast_jobs, *, tq, group):
    bsz, s, w3 = qkv.shape
    w = w3 // 3
    hd = w // SB_HEADS
    grid = (bsz, SB_HEADS)
    win = _Windows()
    c_ops, c_in, c_out, c_shapes, c_parts = _cast_specs(win, cast_jobs, grid)
    temps = group * 8 * _nbytes((tq, 2 * tq), F32)
    outs = pl.pallas_call(
        _with_casts(functools.partial(_sb_kernel, tq=tq, group=group), 3, 1, c_parts),
        grid=grid,
        in_specs=[
            win.spec((1, s, hd), BF, lambda b, h: (b, 0, h)),
            win.spec((1, s, hd), BF, lambda b, h: (b, 0, SB_HEADS + h)),
            win.spec((1, s, hd), BF, lambda b, h: (b, 0, 2 * SB_HEADS + h)),
        ] + c_in,
        out_specs=[win.spec((1, s, hd), BF, lambda b, h: (b, 0, h))] + c_out,
        out_shape=[jax.ShapeDtypeStruct((bsz, s, w), BF)] + c_shapes,
        compiler_params=win.params(("arbitrary", "arbitrary"), temps=temps),
        name="sb_attn",
    )(qkv, qkv, qkv, *c_ops)
    return outs[0], outs[1:]


def kernel(x, attn_norm_w, ffn_norm_w, gla_w_in, gla_w_gate_up, gla_b_gate, gla_gnorm_w, gla_w_out,
           kv_norm_w, sb_w_kv, sb_w_q, sb_w_out, ffn_w_gate_up, ffn_w_down, final_norm_w):
    bsz, s, d = x.shape
    t = bsz * s
    depth = attn_norm_w.shape[0]
    n_gla = gla_w_in.shape[0]
    rank = gla_w_gate_up.shape[1]
    main_w = gla_w_in.shape[2] - rank
    hd = d // SB_HEADS
    row = lambda v: v.reshape(1, -1)
    assert n_gla >= 1 and depth - n_gla == 1

    w_in_t = jnp.swapaxes(gla_w_in, 1, 2)

    h = x.reshape(t, d)
    for layer in range(depth):
        casts = [[(ffn_w_gate_up, layer)], [(ffn_w_down, layer)]]
        if layer < n_gla:
            proj, la = _gla_inproj(h, row(attn_norm_w[layer]), w_in_t, layer, main_w, rank,
                                   gla_w_gate_up[layer], row(gla_b_gate[layer]), **TILES["inproj"])
            casts.append([(gla_w_out, layer)])
            if layer == n_gla - 1:
                casts.append([(sb_w_q, 0), (sb_w_kv, None)])
            o, cast = _gla_core(proj.reshape(bsz, s, main_w), la.reshape(bsz, s, -1),
                                row(gla_gnorm_w[layer]), casts, **TILES["gla"])
            if layer == n_gla - 1:
                w_qkv = cast[3]
        else:
            j = layer - n_gla
            qkv = _norm_matmul2(h, row(attn_norm_w[layer]), row(kv_norm_w), w_qkv, sb_w_q.shape[2],
                                scale_a=hd ** -0.5 * LOG2_E, **TILES["qkv"])
            casts.append([(sb_w_out, j)])
            o, cast = _sb_attention(qkv.reshape(bsz, s, -1), casts, **TILES["sb"])
        w_gate_up, w_down, w_out = cast[:3]
        h = _proj_res(o.reshape(t, -1), w_out, h, **TILES["proj"])
        h = _ffn(h, row(ffn_norm_w[layer]), w_gate_up, w_down, row(final_norm_w),
                 final_norm=layer == depth - 1, **TILES["ffn"])
    return h.reshape(bsz, s, d)
```

```python
import functools

import jax
import jax.numpy as jnp
from jax import lax
from jax.experimental import pallas as pl
from jax.experimental.pallas import tpu as pltpu

EPS = 1e-6
GLA_HEADS = 4
GLA_GATE_TAU = 16.0
SB_HEADS = 16
BF = jnp.bfloat16
F32 = jnp.float32
LANES = 128
BF16_SUBLANES = 16
MIB = 2 ** 20
V7X_VMEM_BYTES = 64 * MIB
VMEM_HEADROOM = 4 * MIB
VMEM_SPILL_ALLOWANCE = 2 * MIB

TILES = dict(
    inproj=dict(tm=1024, tn=768),
    gla=dict(chunk=128, sub=8, heads=GLA_HEADS),
    proj=dict(tm=512),
    ffn=dict(tm=1024, tf=512),
    qkv=dict(tm=1024, tn=1024),
    sb=dict(tq=128, group=8),
)

_NT = (((1,), (1,)), ((), ()))
_TN = (((0,), (0,)), ((), ()))


def _nbytes(shape, dtype):
    n = jnp.dtype(dtype).itemsize
    for dim in shape:
        n *= dim or 1
    return n


class _Windows:
    def __init__(self):
        self.bytes = 0

    def spec(self, block, dtype, index_map):
        self.bytes += _nbytes(block, dtype)
        return pl.BlockSpec(block, index_map)

    def params(self, sem, scratch=(), temps=0):
        need = 2 * self.bytes + sum(_nbytes(shape, dtype) for shape, dtype in scratch) + temps
        need = -(-(need + VMEM_SPILL_ALLOWANCE) // MIB) * MIB
        assert need <= V7X_VMEM_BYTES - VMEM_HEADROOM, need
        return pltpu.CompilerParams(dimension_semantics=sem, vmem_limit_bytes=need)


def _vmem(scratch):
    return [pltpu.VMEM(shape, dtype) for shape, dtype in scratch]


def _row_tile_halves(win, tm, d, n_tiles, n_steps):
    half = tm // 2
    last = 2 * n_tiles - 2

    def top(i, j):
        return jnp.minimum(2 * (i + (j >= n_steps - 2).astype(jnp.int32)), last), 0

    def bottom(i, j):
        return jnp.minimum(2 * (i + (j >= n_steps - 1).astype(jnp.int32)), last) + 1, 0

    return [win.spec((half, d), F32, top), win.spec((half, d), F32, bottom)]


def _dot(a, b):
    return jnp.dot(a, b, preferred_element_type=F32)


def _rms(x, w):
    return x * lax.rsqrt(jnp.mean(x * x, axis=-1, keepdims=True) + EPS) * w


def _split2(x):
    hi = x.astype(BF)
    lo = (x - hi.astype(F32)).astype(BF)
    return hi, lo


def _log_sigmoid(x):
    return jnp.minimum(x, 0.0) - jnp.log(1.0 + jnp.exp(-jnp.abs(x)))


def _silu(x):
    return x * (1.0 / (1.0 + jnp.exp(-x)))


def _cast_specs(win, jobs, grid):
    steps = 1
    for g in grid:
        steps *= g

    def linear(idx):
        s = 0
        for g, i in zip(grid, idx):
            s = s * g + i
        return s

    operands, in_specs, out_specs, out_shapes, parts = [], [], [], [], []
    for job in jobs:
        n_rows = job[0][0].shape[-2]
        n_blk = max(n for n in range(1, steps + 1)
                    if n_rows % n == 0 and (n_rows // n) % BF16_SUBLANES == 0)
        rows = n_rows // n_blk
        blk = lambda *idx, n_blk=n_blk: linear(idx) * n_blk // steps
        for arr, lead in job:
            assert arr.shape[-2] == n_rows
            operands.append(arr)
            if lead is None:
                in_specs.append(win.spec((rows, arr.shape[-1]), F32, lambda *idx, blk=blk: (blk(*idx), 0)))
            else:
                in_specs.append(win.spec((None, rows, arr.shape[-1]), F32,
                                         lambda *idx, blk=blk, lead=lead: (lead, blk(*idx), 0)))
        width = sum(arr.shape[-1] for arr, _ in job)
        out_specs.append(win.spec((rows, width), BF, lambda *idx, blk=blk: (blk(*idx), 0)))
        out_shapes.append(jax.ShapeDtypeStruct((n_rows, width), BF))
        parts.append(len(job))
    return operands, in_specs, out_specs, out_shapes, parts


def _with_casts(body, n_in, n_out, parts, inline=False):
    n_src = sum(parts)

    def kernel(*refs):
        ins, rest = refs[:n_in], refs[n_in:]
        srcs, rest = rest[:n_src], rest[n_src:]
        outs, rest = rest[:n_out], rest[n_out:]
        dsts, scratch = rest[:len(parts)], rest[len(parts):]

        def cast():
            todo = list(srcs)
            for dst, n_parts in zip(dsts, parts):
                col = 0
                for _ in range(n_parts):
                    src = todo.pop(0)
                    dst[:, col:col + src.shape[-1]] = src[...].astype(BF)
                    col += src.shape[-1]

        if inline:
            body(*ins, *outs, *scratch, cast=cast)
        else:
            body(*ins, *outs, *scratch)
            cast()

    return kernel


def _norm_matmul2_kernel(xt_ref, xb_ref, nwa_ref, nwb_ref, w_ref, o_ref, xn_ref, *, na, scale_a, cast):
    j = pl.program_id(1)

    @pl.when(j == 0)
    def _():
        cast()
        half = xt_ref.shape[0]
        for r, ref in enumerate((xt_ref, xb_ref)):
            rows = slice(r * half, (r + 1) * half)
            x = ref[...]
            xhat = x * lax.rsqrt(jnp.mean(x * x, axis=-1, keepdims=True) + EPS)
            xa = (xhat * nwa_ref[...]).astype(BF)
            xn_ref[0, rows] = xa
            xn_ref[1, rows] = (xhat * nwb_ref[...]).astype(BF)
            o_ref[rows] = (_dot(xa, w_ref[...]) * scale_a).astype(o_ref.dtype)

    @pl.when((j > 0) & (j < na))
    def _():
        cast()
        o_ref[...] = (_dot(xn_ref[0], w_ref[...]) * scale_a).astype(o_ref.dtype)

    @pl.when(j >= na)
    def _():
        cast()
        o_ref[...] = _dot(xn_ref[1], w_ref[...]).astype(o_ref.dtype)


def _norm_matmul2(x, nwa, nwb, w, n_a, cast_jobs, *, tm, tn, scale_a):
    t, d = x.shape
    n = w.shape[1]
    grid = (t // tm, n // tn)
    win = _Windows()
    c_ops, c_in, c_out, c_shapes, c_parts = _cast_specs(win, cast_jobs, grid)
    scratch = [((2, tm, d), BF)]
    temps = 2 * _nbytes((tm // 2, d), F32) + _nbytes((tm, tn), F32)
    body = functools.partial(_norm_matmul2_kernel, na=n_a // tn, scale_a=scale_a)
    outs = pl.pallas_call(
        _with_casts(body, 5, 1, c_parts, inline=True),
        grid=grid,
        in_specs=_row_tile_halves(win, tm, d, t // tm, n // tn) + [
            win.spec((1, d), F32, lambda i, j: (0, 0)),
            win.spec((1, d), F32, lambda i, j: (0, 0)),
            win.spec((d, tn), BF, lambda i, j: (0, j)),
        ] + c_in,
        out_specs=[win.spec((tm, tn), BF, lambda i, j: (i, j))] + c_out,
        out_shape=[jax.ShapeDtypeStruct((t, n), BF)] + c_shapes,
        scratch_shapes=_vmem(scratch),
        compiler_params=win.params(("arbitrary", "arbitrary"), scratch, temps),
        name="norm_matmul2",
    )(x, x, nwa, nwb, w, *c_ops)
    return outs[0], outs[1:]


def _gla_inproj_kernel(xt_ref, xb_ref, nw_ref, w_ref, wgl_ref, wup_ref, bg_ref, o_ref, la_ref, xn_ref, gl_ref,
                       *, rank):
    def dot_t(a, w_t):
        return lax.dot_general(a, w_t, _NT, preferred_element_type=F32)

    def gate(gl):
        up_hi, up_lo = _split2(wup_ref[...])
        group = lax.broadcasted_iota(jnp.int32, up_hi.shape, 0) // rank
        g = _dot(gl, jnp.where(group == 2, up_lo, up_hi)) + bg_ref[...]
        la_ref[...] = _log_sigmoid(g) * (1.0 / GLA_GATE_TAU)

    @pl.when(pl.program_id(1) == 0)
    def _():
        x = jnp.concatenate([xt_ref[...], xb_ref[...]], axis=0)
        xn = _rms(x, nw_ref[...]).astype(BF)
        xn_ref[...] = xn
        o_ref[...] = dot_t(xn, w_ref[...].astype(BF)).astype(o_ref.dtype)
        gl_hi, gl_lo = _split2(dot_t(xn, wgl_ref[...]))
        group = lax.broadcasted_iota(jnp.int32, gl_hi.shape, 1) // rank
        gl = jnp.where(group == 1, gl_lo, gl_hi)
        gl_ref[...] = gl
        gate(gl)

    @pl.when(pl.program_id(1) > 0)
    def _():
        gate(gl_ref[...])
        o_ref[...] = dot_t(xn_ref[...], w_ref[...].astype(BF)).astype(o_ref.dtype)


def _gla_inproj(x, nw, w_t_all, layer, n, rank, w_gate_up, bg, *, tm, tn):
    t, d = x.shape
    steps = n // tn
    qk_w = w_gate_up.shape[1]
    slab = qk_w // steps
    assert slab * steps == qk_w and slab % LANES == 0 and 3 * rank <= LANES
    three = lambda a: jnp.pad(jnp.tile(a, (3, 1)), ((0, LANES - 3 * rank), (0, 0)))
    wgl_t = three(w_t_all[layer, n:n + rank]).astype(BF)
    wup = three(w_gate_up)
    win = _Windows()
    scratch = [((tm, d), BF), ((tm, LANES), BF)]
    temps = _nbytes((tm, tn), F32) + _nbytes((tn, d), BF)
    return pl.pallas_call(
        functools.partial(_gla_inproj_kernel, rank=rank),
        grid=(t // tm, steps),
        in_specs=_row_tile_halves(win, tm, d, t // tm, steps) + [
            win.spec((1, d), F32, lambda i, j: (0, 0)),
            win.spec((None, tn, d), F32, lambda i, j: (layer, j, 0)),
            win.spec((LANES, d), BF, lambda i, j: (0, 0)),
            win.spec((LANES, slab), F32, lambda i, j: (0, j)),
            win.spec((1, slab), F32, lambda i, j: (0, j)),
        ],
        out_specs=[
            win.spec((tm, tn), BF, lambda i, j: (i, j)),
            win.spec((tm, slab), F32, lambda i, j: (i, j)),
        ],
        out_shape=[
            jax.ShapeDtypeStruct((t, n), BF),
            jax.ShapeDtypeStruct((t, qk_w), F32),
        ],
        scratch_shapes=_vmem(scratch),
        compiler_params=win.params(("parallel", "arbitrary"), scratch, temps),
        name="gla_inproj",
    )(x, x, nw, w_t_all, wgl_t, wup, bg)


def _gla_kernel(q_ref, k_ref, v_ref, r_ref, g_ref, gw_ref, o_ref, st_ref, qh_ref, kh_ref,
                *, chunk, sub, heads, scale):
    n_sub = chunk // sub
    dk = q_ref.shape[-1] // heads
    dv = v_ref.shape[-1] // heads
    hs = range(heads)

    @pl.when(pl.program_id(2) == 0)
    def _():
        st_ref[...] = jnp.zeros_like(st_ref)
        qh_ref[...] = jnp.zeros_like(qh_ref)
        kh_ref[...] = jnp.zeros_like(kh_ref)

    def head_cols(ref, h, width):
        return ref[0, :, h * width:(h + 1) * width]

    q = [head_cols(q_ref, h, dk).astype(F32) * scale for h in hs]
    k = [head_cols(k_ref, h, dk).astype(F32) for h in hs]
    g = [head_cols(g_ref, h, dk) for h in hs]

    row = lax.broadcasted_iota(jnp.int32, (chunk, chunk), 0)
    col = lax.broadcasted_iota(jnp.int32, (chunk, chunk), 1)
    tri = jnp.where(col <= row, 1.0, 0.0).astype(BF)
    diff = jnp.where(row // sub == col // sub, row - col, -1)

    b = []
    for h in hs:
        g_hi, g_lo = _split2(g[h])
        b.append(_dot(tri, g_hi) + _dot(tri, g_lo))

    st = [st_ref[h] for h in hs]
    o = [lax.dot_general((q[h] * jnp.exp(b[h])).astype(BF), st[h].astype(BF), _NT,
                         preferred_element_type=F32) for h in hs]

    def store_rows(ref, h, start, cols, x):
        lo = start // BF16_SUBLANES * BF16_SUBLANES
        hi = -(-(start + x.shape[0]) // BF16_SUBLANES) * BF16_SUBLANES
        pieces = [jnp.zeros((start - lo, x.shape[1]), F32)] if start > lo else []
        pieces.append(x)
        if hi > start + x.shape[0]:
            pieces.append(jnp.zeros((hi - start - x.shape[0], x.shape[1]), F32))
        x = jnp.concatenate(pieces, axis=0) if len(pieces) > 1 else x
        ref[h, lo:hi, cols] = x.astype(BF)

    scores = []
    for h in hs:
        k_run = None
        for i in range(1, n_sub):
            beta = b[h][i * sub - 1:i * sub, :]
            rows = slice(i * sub, (i + 1) * sub)
            prev = slice((i - 1) * sub, i * sub)
            cols = slice((i - 1) * dk, i * dk)
            store_rows(qh_ref, h, i * sub, cols, q[h][rows] * jnp.exp(b[h][rows] - beta))
            k_new = k[h][prev] * jnp.exp(beta - b[h][prev])
            if k_run is None:
                k_run = k_new
            else:
                k_run = jnp.concatenate([k_run * jnp.exp(beta - beta_prev), k_new], axis=0)
            store_rows(kh_ref, h, 0, cols, k_run)
            beta_prev = beta
        scores.append(lax.dot_general(qh_ref[h], kh_ref[h], _NT, preferred_element_type=F32))

    for h in hs:
        decay = jnp.exp(g[h])
        w = k[h]
        for d in range(sub):
            if d > 0:
                w = decay * pltpu.roll(w, 1, 0)
            scores[h] = jnp.where(diff == d, jnp.sum(q[h] * w, axis=-1, keepdims=True), scores[h])

    v = [head_cols(v_ref, h, dv) for h in hs]
    o = [o[h] + _dot(scores[h].astype(BF), v[h]) for h in hs]

    for h in hs:
        b_last = b[h][chunk - 1:chunk, :]
        k_dec = (k[h] * jnp.exp(b_last - b[h])).astype(BF)
        st_ref[h] = st[h] * jnp.exp(b_last) + lax.dot_general(v[h], k_dec, _TN, preferred_element_type=F32)

    for h in hs:
        r = head_cols(r_ref, h, dv).astype(F32)
        o_ref[0, :, h * dv:(h + 1) * dv] = (_rms(o[h], gw_ref[...]) * _silu(r)).astype(o_ref.dtype)


def _gla_core(proj, la, gw, cast_jobs, *, chunk, sub, heads):
    bsz, s, _ = proj.shape
    dk = la.shape[-1] // GLA_HEADS
    dv = gw.shape[-1]
    groups = GLA_HEADS // heads
    k_blocks = groups
    v_blocks = 2 * GLA_HEADS * dk // (heads * dv)
    r_blocks = v_blocks + groups
    expanded = (chunk // sub - 1) * dk
    grid = (bsz, groups, s // chunk)
    win = _Windows()
    c_ops, c_in, c_out, c_shapes, c_parts = _cast_specs(win, cast_jobs, grid)
    scratch = [((heads, dv, dk), F32), ((heads, chunk, expanded), BF), ((heads, chunk, expanded), BF)]
    temps = heads * (6 * _nbytes((chunk, dk), F32) + 2 * _nbytes((chunk, dv), F32))
    body = functools.partial(_gla_kernel, chunk=chunk, sub=sub, heads=heads, scale=dk ** -0.5)
    outs = pl.pallas_call(
        _with_casts(body, 6, 1, c_parts),
        grid=grid,
        in_specs=[
            win.spec((1, chunk, heads * dk), BF, lambda b, h, n: (b, n, h)),
            win.spec((1, chunk, heads * dk), BF, lambda b, h, n: (b, n, k_blocks + h)),
            win.spec((1, chunk, heads * dv), BF, lambda b, h, n: (b, n, v_blocks + h)),
            win.spec((1, chunk, heads * dv), BF, lambda b, h, n: (b, n, r_blocks + h)),
            win.spec((1, chunk, heads * dk), F32, lambda b, h, n: (b, n, h)),
            win.spec((1, dv), F32, lambda b, h, n: (0, 0)),
        ] + c_in,
        out_specs=[win.spec((1, chunk, heads * dv), BF, lambda b, h, n: (b, n, h))] + c_out,
        out_shape=[jax.ShapeDtypeStruct((bsz, s, GLA_HEADS * dv), BF)] + c_shapes,
        scratch_shapes=_vmem(scratch),
        compiler_params=win.params(("arbitrary", "arbitrary", "arbitrary"), scratch, temps),
        name="gla_core",
    )(proj, proj, proj, proj, la, gw, *c_ops)
    return outs[0], outs[1:]


def _proj_res_kernel(x_ref, w_ref, res_ref, o_ref):
    o_ref[...] = res_ref[...] + _dot(x_ref[...], w_ref[...])


def _proj_res(x, w, res, *, tm):
    t, kdim = x.shape
    n = w.shape[1]
    win = _Windows()
    return pl.pallas_call(
        _proj_res_kernel,
        grid=(t // tm,),
        in_specs=[
            win.spec((tm, kdim), BF, lambda i: (i, 0)),
            win.spec((kdim, n), BF, lambda i: (0, 0)),
            win.spec((tm, n), F32, lambda i: (i, 0)),
        ],
        out_specs=win.spec((tm, n), F32, lambda i: (i, 0)),
        out_shape=jax.ShapeDtypeStruct((t, n), F32),
        compiler_params=win.params(("parallel",), temps=_nbytes((tm, n), F32)),
        name="proj_res",
    )(x, w, res)


def _ffn_kernel(h_ref, nw_ref, wg_ref, wu_ref, wd_ref, fw_ref, o_ref, xn_ref, *, final_norm):
    f = pl.program_id(1)

    def step(acc_ref):
        xn = xn_ref[...]
        act = (_silu(_dot(xn, wg_ref[...])) * _dot(xn, wu_ref[...])).astype(BF)
        o_ref[...] = acc_ref[...] + _dot(act, wd_ref[...])

    @pl.when(f == 0)
    def _():
        xn_ref[...] = _rms(h_ref[...], nw_ref[...]).astype(BF)
        step(h_ref)

    @pl.when(f > 0)
    def _():
        step(o_ref)

    if final_norm:
        @pl.when(f == pl.num_programs(1) - 1)
        def _():
            o_ref[...] = _rms(o_ref[...], fw_ref[...])


def _ffn(h, nw, w_gate_up, w_down, fw, *, tm, tf, final_norm):
    t, d = h.shape
    d_ff = w_down.shape[0]
    nf = d_ff // tf
    win = _Windows()
    scratch = [((tm, d), BF)]
    temps = 2 * _nbytes((tm, tf), F32) + _nbytes((tm, tf), BF)
    return pl.pallas_call(
        functools.partial(_ffn_kernel, final_norm=final_norm),
        grid=(t // tm, nf),
        in_specs=[
            win.spec((tm, d), F32, lambda i, f: (i, 0)),
            win.spec((1, d), F32, lambda i, f: (0, 0)),
            win.spec((d, tf), BF, lambda i, f: (0, f)),
            win.spec((d, tf), BF, lambda i, f: (0, nf + f)),
            win.spec((tf, d), BF, lambda i, f: (f, 0)),
            win.spec((1, d), F32, lambda i, f: (0, 0)),
        ],
        out_specs=win.spec((tm, d), F32, lambda i, f: (i, 0)),
        out_shape=jax.ShapeDtypeStruct((t, d), F32),
        scratch_shapes=_vmem(scratch),
        compiler_params=win.params(("parallel", "arbitrary"), scratch, temps),
        name="ffn",
    )(h, nw, w_gate_up, w_gate_up, w_down, fw)


_EXP2_UNDERFLOW = -127.0
LOG2_E = 1.4426950408889634


def _sb_kernel(q_ref, k_ref, v_ref, o_ref, *, tq, group):
    s = q_ref.shape[1]
    tw = 2 * tq
    def later(n):
        r = lax.broadcasted_iota(jnp.int32, (n, n), 0)
        c = lax.broadcasted_iota(jnp.int32, (n, n), 1)
        return jnp.where(r > c, 1.0, 0.0).astype(BF)

    def causal(n_keys, offset):
        qr = lax.broadcasted_iota(jnp.int32, (tq, n_keys), 0)
        kc = lax.broadcasted_iota(jnp.int32, (tq, n_keys), 1)
        return kc < qr + offset

    later_w, later_b = later(tw), later(tq)
    diag_mask = causal(tq, 0)
    win_mask = causal(tw, tq)

    def scores(q, start, width, mask):
        kb = k_ref[0, pl.ds(start, width), :]
        z = lax.dot_general(q, kb, _NT, preferred_element_type=F32)
        nz = -z
        lf = jnp.minimum(nz, 0.0) - jnp.log2(1.0 + jnp.exp2(jnp.minimum(z, nz)))
        if mask is not None:
            lf = jnp.where(mask, lf, 0.0)
        return z, lf

    def suffix(lf, later, carry):
        after = _dot(lf.astype(BF), later)
        if carry is not None:
            after = after + carry
        return after, after[:, :1] + lf[:, :1]

    def weighted(z, lf, after, start, width, mask):
        a = jnp.exp2(z + lf + after)
        if mask is not None:
            a = jnp.where(mask, a, 0.0)
        return _dot(a.astype(BF), v_ref[0, pl.ds(start, width), :])

    def window(tile):
        if tile == 0:
            return 0, tq, later_b, diag_mask
        return (tile - 1) * tq, tw, later_w, win_mask

    def extend(tile, q, carry, top, acc):
        def cond(st):
            return jnp.logical_and(st[0] >= 0, st[1] > _EXP2_UNDERFLOW)

        def body(st):
            j, _, carry, acc = st
            start = pl.multiple_of(j * tq, tq)
            z, lf = scores(q, start, tq, None)
            after, carry = suffix(lf, later_b, carry)
            acc = acc + weighted(z, lf, after, start, tq, None)
            return j - 1, jnp.max(carry), carry, acc

        init = (jnp.asarray(tile - 2, jnp.int32), top, carry, acc)
        return lax.while_loop(cond, body, init)[3]

    def first_passes(base):
        tiles = [base + g for g in range(group)]
        wins = [window(t) for t in tiles]
        qs = [q_ref[0, t * tq:(t + 1) * tq, :] for t in tiles]
        zl = [scores(q, w[0], w[1], w[3]) for q, w in zip(qs, wins)]
        ac = [suffix(lf, w[2], None) for (_, lf), w in zip(zl, wins)]
        accs = [weighted(z, lf, after, w[0], w[1], w[3]) for (z, lf), (after, _), w in zip(zl, ac, wins)]
        tops = [jnp.max(carry) for _, carry in ac]
        return list(zip(tiles, qs, [carry for _, carry in ac], tops, accs))

    todo = []
    for base in range(0, s // tq, group):
        todo += first_passes(base)
    for t, q, carry, top, acc in todo:
        acc = extend(t, q, carry, top, acc)
        o_ref[0, t * tq:(t + 1) * tq, :] = acc.astype(o_ref.dtype)


def _sb_attention(qkv, cast_jobs, *, tq, group):
    bsz, s, w3 = qkv.shape
    w = w3 // 3
    hd = w // SB_HEADS
    grid = (bsz, SB_HEADS)
    win = _Windows()
    c_ops, c_in, c_out, c_shapes, c_parts = _cast_specs(win, cast_jobs, grid)
    temps = group * 8 * _nbytes((tq, 2 * tq), F32)
    outs = pl.pallas_call(
        _with_casts(functools.partial(_sb_kernel, tq=tq, group=group), 3, 1, c_parts),
        grid=grid,
        in_specs=[
            win.spec((1, s, hd), BF, lambda b, h: (b, 0, h)),
            win.spec((1, s, hd), BF, lambda b, h: (b, 0, SB_HEADS + h)),
            win.spec((1, s, hd), BF, lambda b, h: (b, 0, 2 * SB_HEADS + h)),
        ] + c_in,
        out_specs=[win.spec((1, s, hd), BF, lambda b, h: (b, 0, h))] + c_out,
        out_shape=[jax.ShapeDtypeStruct((bsz, s, w), BF)] + c_shapes,
        compiler_params=win.params(("arbitrary", "arbitrary"), temps=temps),
        name="sb_attn",
    )(qkv, qkv, qkv, *c_ops)
    return outs[0], outs[1:]


def kernel(x, attn_norm_w, ffn_norm_w, gla_w_in, gla_w_gate_up, gla_b_gate, gla_gnorm_w, gla_w_out,
           kv_norm_w, sb_w_kv, sb_w_q, sb_w_out, ffn_w_gate_up, ffn_w_down, final_norm_w):
    bsz, s, d = x.shape
    t = bsz * s
    depth = attn_norm_w.shape[0]
    n_gla = gla_w_in.shape[0]
    rank = gla_w_gate_up.shape[1]
    main_w = gla_w_in.shape[2] - rank
    hd = d // SB_HEADS
    row = lambda v: v.reshape(1, -1)
    assert n_gla >= 1 and depth - n_gla == 1

    w_in_t = jnp.swapaxes(gla_w_in, 1, 2)

    h = x.reshape(t, d)
    for layer in range(depth):
        if layer < n_gla:
            proj, la = _gla_inproj(h, row(attn_norm_w[layer]), w_in_t, layer, main_w, rank,
                                   gla_w_gate_up[layer], row(gla_b_gate[layer]), **TILES["inproj"])
            casts = [[(ffn_w_gate_up, layer)], [(ffn_w_down, layer)], [(gla_w_out, layer)]]
            if layer == n_gla - 1:
                casts.append([(sb_w_q, 0), (sb_w_kv, None)])
            o, cast = _gla_core(proj.reshape(bsz, s, main_w), la.reshape(bsz, s, -1),
                                row(gla_gnorm_w[layer]), casts, **TILES["gla"])
            w_gate_up, w_down, w_out = cast[:3]
            if layer == n_gla - 1:
                w_qkv = cast[3]
        else:
            j = layer - n_gla
            qkv, (w_down,) = _norm_matmul2(h, row(attn_norm_w[layer]), row(kv_norm_w), w_qkv, sb_w_q.shape[2],
                                           [[(ffn_w_down, layer)]], scale_a=hd ** -0.5 * LOG2_E, **TILES["qkv"])
            o, (w_gate_up, w_out) = _sb_attention(qkv.reshape(bsz, s, -1),
                                                  [[(ffn_w_gate_up, layer)], [(sb_w_out, j)]], **TILES["sb"])
        h = _proj_res(o.reshape(t, -1), w_out, h, **TILES["proj"])
        h = _ffn(h, row(ffn_norm_w[layer]), w_gate_up, w_down, row(final_norm_w),
                 final_norm=layer == depth - 1, **TILES["ffn"])
    return h.reshape(bsz, s, d)
```

```python
import functools

import jax
import jax.numpy as jnp
from jax import lax
from jax.experimental import pallas as pl
from jax.experimental.pallas import tpu as pltpu

EPS = 1e-6
GLA_HEADS = 4
GLA_GATE_TAU = 16.0
SB_HEADS = 16
BF = jnp.bfloat16
F32 = jnp.float32
LANES = 128
BF16_SUBLANES = 16
MIB = 2 ** 20
V7X_VMEM_BYTES = 64 * MIB
VMEM_HEADROOM = 4 * MIB
VMEM_SPILL_ALLOWANCE = 2 * MIB

TILES = dict(
    inproj=dict(tm=1024, tn=768),
    gla=dict(chunk=128, sub=8, heads=GLA_HEADS),
    proj=dict(tm=512),
    ffn=dict(tm=1024, tf=512),
    qkv=dict(tm=1024, tn=1024),
    sb=dict(tq=128, group=8),
)

_NT = (((1,), (1,)), ((), ()))
_TN = (((0,), (0,)), ((), ()))


def _nbytes(shape, dtype):
    n = jnp.dtype(dtype).itemsize
    for dim in shape:
        n *= dim or 1
    return n


class _Windows:
    def __init__(self):
        self.bytes = 0

    def spec(self, block, dtype, index_map):
        self.bytes += _nbytes(block, dtype)
        return pl.BlockSpec(block, index_map)

    def params(self, sem, scratch=(), temps=0):
        need = 2 * self.bytes + sum(_nbytes(shape, dtype) for shape, dtype in scratch) + temps
        need = -(-(need + VMEM_SPILL_ALLOWANCE) // MIB) * MIB
        assert need <= V7X_VMEM_BYTES - VMEM_HEADROOM, need
        return pltpu.CompilerParams(dimension_semantics=sem, vmem_limit_bytes=need)


def _vmem(scratch):
    return [pltpu.VMEM(shape, dtype) for shape, dtype in scratch]


def _row_tile_halves(win, tm, d, n_tiles, n_steps):
    half = tm // 2
    last = 2 * n_tiles - 2

    def top(i, j):
        return jnp.minimum(2 * (i + (j >= n_steps - 2).astype(jnp.int32)), last), 0

    def bottom(i, j):
        return jnp.minimum(2 * (i + (j >= n_steps - 1).astype(jnp.int32)), last) + 1, 0

    return [win.spec((half, d), F32, top), win.spec((half, d), F32, bottom)]


def _dot(a, b):
    return jnp.dot(a, b, preferred_element_type=F32)


def _rms(x, w):
    return x * lax.rsqrt(jnp.mean(x * x, axis=-1, keepdims=True) + EPS) * w


def _split2(x):
    hi = x.astype(BF)
    lo = (x - hi.astype(F32)).astype(BF)
    return hi, lo


def _log_sigmoid(x):
    return jnp.minimum(x, 0.0) - jnp.log(1.0 + jnp.exp(-jnp.abs(x)))


def _silu(x):
    return x * (1.0 / (1.0 + jnp.exp(-x)))


def _cast_specs(win, jobs, grid):
    steps = 1
    for g in grid:
        steps *= g

    def linear(idx):
        s = 0
        for g, i in zip(grid, idx):
            s = s * g + i
        return s

    operands, in_specs, out_specs, out_shapes, parts = [], [], [], [], []
    for job in jobs:
        n_rows = job[0][0].shape[-2]
        n_blk = max(n for n in range(1, steps + 1)
                    if n_rows % n == 0 and (n_rows // n) % BF16_SUBLANES == 0)
        rows = n_rows // n_blk
        blk = lambda *idx, n_blk=n_blk: linear(idx) * n_blk // steps
        for arr, lead in job:
            assert arr.shape[-2] == n_rows
            operands.append(arr)
            if lead is None:
                in_specs.append(win.spec((rows, arr.shape[-1]), F32, lambda *idx, blk=blk: (blk(*idx), 0)))
            else:
                in_specs.append(win.spec((None, rows, arr.shape[-1]), F32,
                                         lambda *idx, blk=blk, lead=lead: (lead, blk(*idx), 0)))
        width = sum(arr.shape[-1] for arr, _ in job)
        out_specs.append(win.spec((rows, width), BF, lambda *idx, blk=blk: (blk(*idx), 0)))
        out_shapes.append(jax.ShapeDtypeStruct((n_rows, width), BF))
        parts.append(len(job))
    return operands, in_specs, out_specs, out_shapes, parts


def _with_casts(body, n_in, n_out, parts, inline=False):
    n_src = sum(parts)

    def kernel(*refs):
        ins, rest = refs[:n_in], refs[n_in:]
        srcs, rest = rest[:n_src], rest[n_src:]
        outs, rest = rest[:n_out], rest[n_out:]
        dsts, scratch = rest[:len(parts)], rest[len(parts):]

        def cast():
            todo = list(srcs)
            for dst, n_parts in zip(dsts, parts):
                col = 0
                for _ in range(n_parts):
                    src = todo.pop(0)
                    dst[:, col:col + src.shape[-1]] = src[...].astype(BF)
                    col += src.shape[-1]

        if inline:
            body(*ins, *outs, *scratch, cast=cast)
        else:
            body(*ins, *outs, *scratch)
            cast()

    return kernel


def _norm_matmul2_kernel(xt_ref, xb_ref, nwa_ref, nwb_ref, w_ref, o_ref, xn_ref, *, na, scale_a, cast):
    j = pl.program_id(1)

    @pl.when(j == 0)
    def _():
        cast()
        half = xt_ref.shape[0]
        for r, ref in enumerate((xt_ref, xb_ref)):
            rows = slice(r * half, (r + 1) * half)
            x = ref[...]
            xhat = x * lax.rsqrt(jnp.mean(x * x, axis=-1, keepdims=True) + EPS)
            xa = (xhat * nwa_ref[...]).astype(BF)
            xn_ref[0, rows] = xa
            xn_ref[1, rows] = (xhat * nwb_ref[...]).astype(BF)
            o_ref[rows] = (_dot(xa, w_ref[...]) * scale_a).astype(o_ref.dtype)

    @pl.when((j > 0) & (j < na))
    def _():
        cast()
        o_ref[...] = (_dot(xn_ref[0], w_ref[...]) * scale_a).astype(o_ref.dtype)

    @pl.when(j >= na)
    def _():
        cast()
        o_ref[...] = _dot(xn_ref[1], w_ref[...]).astype(o_ref.dtype)


def _norm_matmul2(x, nwa, nwb, w, n_a, cast_jobs, *, tm, tn, scale_a):
    t, d = x.shape
    n = w.shape[1]
    grid = (t // tm, n // tn)
    win = _Windows()
    c_ops, c_in, c_out, c_shapes, c_parts = _cast_specs(win, cast_jobs, grid)
    scratch = [((2, tm, d), BF)]
    temps = 2 * _nbytes((tm // 2, d), F32) + _nbytes((tm, tn), F32)
    body = functools.partial(_norm_matmul2_kernel, na=n_a // tn, scale_a=scale_a)
    outs = pl.pallas_call(
        _with_casts(body, 5, 1, c_parts, inline=True),
        grid=grid,
        in_specs=_row_tile_halves(win, tm, d, t // tm, n // tn) + [
            win.spec((1, d), F32, lambda i, j: (0, 0)),
            win.spec((1, d), F32, lambda i, j: (0, 0)),
            win.spec((d, tn), BF, lambda i, j: (0, j)),
        ] + c_in,
        out_specs=[win.spec((tm, tn), BF, lambda i, j: (i, j))] + c_out,
        out_shape=[jax.ShapeDtypeStruct((t, n), BF)] + c_shapes,
        scratch_shapes=_vmem(scratch),
        compiler_params=win.params(("arbitrary", "arbitrary"), scratch, temps),
        name="norm_matmul2",
    )(x, x, nwa, nwb, w, *c_ops)
    return outs[0], outs[1:]


def _gla_inproj_kernel(xt_ref, xb_ref, nw_ref, w_ref, wgl_ref, wup_ref, bg_ref, o_ref, la_ref, xn_ref, gl_ref,
                       *, rank):
    def dot_t(a, w_t):
        return lax.dot_general(a, w_t, _NT, preferred_element_type=F32)

    def gate(gl):
        up_hi, up_lo = _split2(wup_ref[...])
        group = lax.broadcasted_iota(jnp.int32, up_hi.shape, 0) // rank
        g = _dot(gl, jnp.where(group == 2, up_lo, up_hi)) + bg_ref[...]
        la_ref[...] = _log_sigmoid(g) * (1.0 / GLA_GATE_TAU)

    @pl.when(pl.program_id(1) == 0)
    def _():
        x = jnp.concatenate([xt_ref[...], xb_ref[...]], axis=0)
        xn = _rms(x, nw_ref[...]).astype(BF)
        xn_ref[...] = xn
        o_ref[...] = dot_t(xn, w_ref[...].astype(BF)).astype(o_ref.dtype)
        gl_hi, gl_lo = _split2(dot_t(xn, wgl_ref[...]))
        group = lax.broadcasted_iota(jnp.int32, gl_hi.shape, 1) // rank
        gl = jnp.where(group == 1, gl_lo, gl_hi)
        gl_ref[...] = gl
        gate(gl)

    @pl.when(pl.program_id(1) > 0)
    def _():
        gate(gl_ref[...])
        o_ref[...] = dot_t(xn_ref[...], w_ref[...].astype(BF)).astype(o_ref.dtype)


def _gla_inproj(x, nw, w_t_all, layer, n, rank, w_gate_up, bg, *, tm, tn):
    t, d = x.shape
    steps = n // tn
    qk_w = w_gate_up.shape[1]
    slab = qk_w // steps
    assert slab * steps == qk_w and slab % LANES == 0 and 3 * rank <= LANES
    three = lambda a: jnp.pad(jnp.tile(a, (3, 1)), ((0, LANES - 3 * rank), (0, 0)))
    wgl_t = three(w_t_all[layer, n:n + rank]).astype(BF)
    wup = three(w_gate_up)
    win = _Windows()
    scratch = [((tm, d), BF), ((tm, LANES), BF)]
    temps = _nbytes((tm, tn), F32) + _nbytes((tn, d), BF)
    return pl.pallas_call(
        functools.partial(_gla_inproj_kernel, rank=rank),
        grid=(t // tm, steps),
        in_specs=_row_tile_halves(win, tm, d, t // tm, steps) + [
            win.spec((1, d), F32, lambda i, j: (0, 0)),
            win.spec((None, tn, d), F32, lambda i, j: (layer, j, 0)),
            win.spec((LANES, d), BF, lambda i, j: (0, 0)),
            win.spec((LANES, slab), F32, lambda i, j: (0, j)),
            win.spec((1, slab), F32, lambda i, j: (0, j)),
        ],
        out_specs=[
            win.spec((tm, tn), BF, lambda i, j: (i, j)),
            win.spec((tm, slab), F32, lambda i, j: (i, j)),
        ],
        out_shape=[
            jax.ShapeDtypeStruct((t, n), BF),
            jax.ShapeDtypeStruct((t, qk_w), F32),
        ],
        scratch_shapes=_vmem(scratch),
        compiler_params=win.params(("parallel", "arbitrary"), scratch, temps),
        name="gla_inproj",
    )(x, x, nw, w_t_all, wgl_t, wup, bg)


def _gla_kernel(q_ref, k_ref, v_ref, r_ref, g_ref, gw_ref, o_ref, st_ref, qh_ref, kh_ref,
                *, chunk, sub, heads, scale):
    n_sub = chunk // sub
    dk = q_ref.shape[-1] // heads
    dv = v_ref.shape[-1] // heads
    hs = range(heads)

    @pl.when(pl.program_id(2) == 0)
    def _():
        st_ref[...] = jnp.zeros_like(st_ref)
        qh_ref[...] = jnp.zeros_like(qh_ref)
        kh_ref[...] = jnp.zeros_like(kh_ref)

    def head_cols(ref, h, width):
        return ref[0, :, h * width:(h + 1) * width]

    q = [head_cols(q_ref, h, dk).astype(F32) * scale for h in hs]
    k = [head_cols(k_ref, h, dk).astype(F32) for h in hs]
    g = [head_cols(g_ref, h, dk) for h in hs]

    row = lax.broadcasted_iota(jnp.int32, (chunk, chunk), 0)
    col = lax.broadcasted_iota(jnp.int32, (chunk, chunk), 1)
    tri = jnp.where(col <= row, 1.0, 0.0).astype(BF)
    diff = jnp.where(row // sub == col // sub, row - col, -1)

    b = []
    for h in hs:
        g_hi, g_lo = _split2(g[h])
        b.append(_dot(tri, g_hi) + _dot(tri, g_lo))

    st = [st_ref[h] for h in hs]
    o = [lax.dot_general((q[h] * jnp.exp(b[h])).astype(BF), st[h].astype(BF), _NT,
                         preferred_element_type=F32) for h in hs]

    def store_rows(ref, h, start, cols, x):
        lo = start // BF16_SUBLANES * BF16_SUBLANES
        hi = -(-(start + x.shape[0]) // BF16_SUBLANES) * BF16_SUBLANES
        pieces = [jnp.zeros((start - lo, x.shape[1]), F32)] if start > lo else []
        pieces.append(x)
        if hi > start + x.shape[0]:
            pieces.append(jnp.zeros((hi - start - x.shape[0], x.shape[1]), F32))
        x = jnp.concatenate(pieces, axis=0) if len(pieces) > 1 else x
        ref[h, lo:hi, cols] = x.astype(BF)

    scores = []
    for h in hs:
        k_run = None
        for i in range(1, n_sub):
            beta = b[h][i * sub - 1:i * sub, :]
            rows = slice(i * sub, (i + 1) * sub)
            prev = slice((i - 1) * sub, i * sub)
            cols = slice((i - 1) * dk, i * dk)
            store_rows(qh_ref, h, i * sub, cols, q[h][rows] * jnp.exp(b[h][rows] - beta))
            k_new = k[h][prev] * jnp.exp(beta - b[h][prev])
            if k_run is None:
                k_run = k_new
            else:
                k_run = jnp.concatenate([k_run * jnp.exp(beta - beta_prev), k_new], axis=0)
            store_rows(kh_ref, h, 0, cols, k_run)
            beta_prev = beta
        scores.append(lax.dot_general(qh_ref[h], kh_ref[h], _NT, preferred_element_type=F32))

    for h in hs:
        decay = jnp.exp(g[h])
        w = k[h]
        for d in range(sub):
            if d > 0:
                w = decay * pltpu.roll(w, 1, 0)
            scores[h] = jnp.where(diff == d, jnp.sum(q[h] * w, axis=-1, keepdims=True), scores[h])

    v = [head_cols(v_ref, h, dv) for h in hs]
    o = [o[h] + _dot(scores[h].astype(BF), v[h]) for h in hs]

    for h in hs:
        b_last = b[h][chunk - 1:chunk, :]
        k_dec = (k[h] * jnp.exp(b_last - b[h])).astype(BF)
        st_ref[h] = st[h] * jnp.exp(b_last) + lax.dot_general(v[h], k_dec, _TN, preferred_element_type=F32)

    for h in hs:
        r = head_cols(r_ref, h, dv).astype(F32)
        o_ref[0, :, h * dv:(h + 1) * dv] = (_rms(o[h], gw_ref[...]) * _silu(r)).astype(o_ref.dtype)


def _gla_core(proj, la, gw, cast_jobs, *, chunk, sub, heads):
    bsz, s, _ = proj.shape
    dk = la.shape[-1] // GLA_HEADS
    dv = gw.shape[-1]
    groups = GLA_HEADS // heads
    k_blocks = groups
    v_blocks = 2 * GLA_HEADS * dk // (heads * dv)
    r_blocks = v_blocks + groups
    expanded = (chunk // sub - 1) * dk
    grid = (bsz, groups, s // chunk)
    win = _Windows()
    c_ops, c_in, c_out, c_shapes, c_parts = _cast_specs(win, cast_jobs, grid)
    scratch = [((heads, dv, dk), F32), ((heads, chunk, expanded), BF), ((heads, chunk, expanded), BF)]
    temps = heads * (6 * _nbytes((chunk, dk), F32) + 2 * _nbytes((chunk, dv), F32))
    body = functools.partial(_gla_kernel, chunk=chunk, sub=sub, heads=heads, scale=dk ** -0.5)
    outs = pl.pallas_call(
        _with_casts(body, 6, 1, c_parts),
        grid=grid,
        in_specs=[
            win.spec((1, chunk, heads * dk), BF, lambda b, h, n: (b, n, h)),
            win.spec((1, chunk, heads * dk), BF, lambda b, h, n: (b, n, k_blocks + h)),
            win.spec((1, chunk, heads * dv), BF, lambda b, h, n: (b, n, v_blocks + h)),
            win.spec((1, chunk, heads * dv), BF, lambda b, h, n: (b, n, r_blocks + h)),
            win.spec((1, chunk, heads * dk), F32, lambda b, h, n: (b, n, h)),
            win.spec((1, dv), F32, lambda b, h, n: (0, 0)),
        ] + c_in,
        out_specs=[win.spec((1, chunk, heads * dv), BF, lambda b, h, n: (b, n, h))] + c_out,
        out_shape=[jax.ShapeDtypeStruct((bsz, s, GLA_HEADS * dv), BF)] + c_shapes,
        scratch_shapes=_vmem(scratch),
        compiler_params=win.params(("arbitrary", "arbitrary", "arbitrary"), scratch, temps),
        name="gla_core",
    )(proj, proj, proj, proj, la, gw, *c_ops)
    return outs[0], outs[1:]


def _proj_res_kernel(x_ref, w_ref, res_ref, o_ref):
    o_ref[...] = res_ref[...] + _dot(x_ref[...], w_ref[...])


def _proj_res(x, w, res, *, tm):
    t, kdim = x.shape
    n = w.shape[1]
    win = _Windows()
    return pl.pallas_call(
        _proj_res_kernel,
        grid=(t // tm,),
        in_specs=[
            win.spec((tm, kdim), BF, lambda i: (i, 0)),
            win.spec((kdim, n), BF, lambda i: (0, 0)),
            win.spec((tm, n), F32, lambda i: (i, 0)),
        ],
        out_specs=win.spec((tm, n), F32, lambda i: (i, 0)),
        out_shape=jax.ShapeDtypeStruct((t, n), F32),
        compiler_params=win.params(("parallel",), temps=_nbytes((tm, n), F32)),
        name="proj_res",
    )(x, w, res)


def _ffn_kernel(h_ref, nw_ref, wg_ref, wu_ref, wd_ref, fw_ref, o_ref, xn_ref, *, final_norm):
    f = pl.program_id(1)

    def step(acc_ref):
        xn = xn_ref[...]
        act = (_silu(_dot(xn, wg_ref[...])) * _dot(xn, wu_ref[...])).astype(BF)
        o_ref[...] = acc_ref[...] + _dot(act, wd_ref[...])

    @pl.when(f == 0)
    def _():
        xn_ref[...] = _rms(h_ref[...], nw_ref[...]).astype(BF)
        step(h_ref)

    @pl.when(f > 0)
    def _():
        step(o_ref)

    if final_norm:
        @pl.when(f == pl.num_programs(1) - 1)
        def _():
            o_ref[...] = _rms(o_ref[...], fw_ref[...])


def _ffn(h, nw, w_gate_up, w_down, fw, *, tm, tf, final_norm):
    t, d = h.shape
    d_ff = w_down.shape[0]
    nf = d_ff // tf
    win = _Windows()
    scratch = [((tm, d), BF)]
    temps = 2 * _nbytes((tm, tf), F32) + _nbytes((tm, tf), BF)
    return pl.pallas_call(
        functools.partial(_ffn_kernel, final_norm=final_norm),
        grid=(t // tm, nf),
        in_specs=[
            win.spec((tm, d), F32, lambda i, f: (i, 0)),
            win.spec((1, d), F32, lambda i, f: (0, 0)),
            win.spec((d, tf), BF, lambda i, f: (0, f)),
            win.spec((d, tf), BF, lambda i, f: (0, nf + f)),
            win.spec((tf, d), BF, lambda i, f: (f, 0)),
            win.spec((1, d), F32, lambda i, f: (0, 0)),
        ],
        out_specs=win.spec((tm, d), F32, lambda i, f: (i, 0)),
        out_shape=jax.ShapeDtypeStruct((t, d), F32),
        scratch_shapes=_vmem(scratch),
        compiler_params=win.params(("parallel", "arbitrary"), scratch, temps),
        name="ffn",
    )(h, nw, w_gate_up, w_gate_up, w_down, fw)


_EXP2_UNDERFLOW = -127.0
LOG2_E = 1.4426950408889634


def _sb_kernel(q_ref, k_ref, v_ref, o_ref, *, tq, group):
    s = q_ref.shape[1]
    tw = 2 * tq
    def later(n):
        r = lax.broadcasted_iota(jnp.int32, (n, n), 0)
        c = lax.broadcasted_iota(jnp.int32, (n, n), 1)
        return jnp.where(r > c, 1.0, 0.0).astype(BF)

    def causal(n_keys, offset):
        qr = lax.broadcasted_iota(jnp.int32, (tq, n_keys), 0)
        kc = lax.broadcasted_iota(jnp.int32, (tq, n_keys), 1)
        return kc < qr + offset

    later_w, later_b = later(tw), later(tq)
    diag_mask = causal(tq, 0)
    win_mask = causal(tw, tq)

    def scores(q, start, width, mask):
        kb = k_ref[0, pl.ds(start, width), :]
        z = lax.dot_general(q, kb, _NT, preferred_element_type=F32)
        nz = -z
        lf = jnp.minimum(nz, 0.0) - jnp.log2(1.0 + jnp.exp2(jnp.minimum(z, nz)))
        if mask is not None:
            lf = jnp.where(mask, lf, 0.0)
        return z, lf

    def suffix(lf, later, carry):
        after = _dot(lf.astype(BF), later)
        if carry is not None:
            after = after + carry
        return after, after[:, :1] + lf[:, :1]

    def weighted(z, lf, after, start, width, mask):
        a = jnp.exp2(z + lf + after)
        if mask is not None:
            a = jnp.where(mask, a, 0.0)
        return _dot(a.astype(BF), v_ref[0, pl.ds(start, width), :])

    def window(tile):
        if tile == 0:
            return 0, tq, later_b, diag_mask
        return (tile - 1) * tq, tw, later_w, win_mask

    def extend(tile, q, carry, top, acc):
        def cond(st):
            return jnp.logical_and(st[0] >= 0, st[1] > _EXP2_UNDERFLOW)

        def body(st):
            j, _, carry, acc = st
            start = pl.multiple_of(j * tq, tq)
            z, lf = scores(q, start, tq, None)
            after, carry = suffix(lf, later_b, carry)
            acc = acc + weighted(z, lf, after, start, tq, None)
            return j - 1, jnp.max(carry), carry, acc

        init = (jnp.asarray(tile - 2, jnp.int32), top, carry, acc)
        return lax.while_loop(cond, body, init)[3]

    def first_passes(base):
        tiles = [base + g for g in range(group)]
        wins = [window(t) for t in tiles]
        qs = [q_ref[0, t * tq:(t + 1) * tq, :] for t in tiles]
        zl = [scores(q, w[0], w[1], w[3]) for q, w in zip(qs, wins)]
        ac = [suffix(lf, w[2], None) for (_, lf), w in zip(zl, wins)]
        accs = [weighted(z, lf, after, w[0], w[1], w[3]) for (z, lf), (after, _), w in zip(zl, ac, wins)]
        tops = [jnp.max(carry) for _, carry in ac]
        return list(zip(tiles, qs, [carry for _, carry in ac], tops, accs))

    todo = []
    for base in range(0, s // tq, group):
        todo += first_passes(base)
    for t, q, carry, top, acc in todo:
        acc = extend(t, q, carry, top, acc)
        o_ref[0, t * tq:(t + 1) * tq, :] = acc.astype(o_ref.dtype)


def _sb_attention(qkv, cast_jobs, *, tq, group):
    bsz, s, w3 = qkv.shape
    w = w3 // 3
    hd = w // SB_HEADS
    grid = (bsz, SB_HEADS)
    win = _Windows()
    c_ops, c_in, c_out, c_shapes, c_parts = _cast_specs(win, cast_jobs, grid)
    temps = group * 8 * _nbytes((tq, 2 * tq), F32)
    outs = pl.pallas_call(
        _with_casts(functools.partial(_sb_kernel, tq=tq, group=group), 3, 1, c_parts),
        grid=grid,
        in_specs=[
            win.spec((1, s, hd), BF, lambda b, h: (b, 0, h)),
            win.spec((1, s, hd), BF, lambda b, h: (b, 0, SB_HEADS + h)),
            win.spec((1, s, hd), BF, lambda b, h: (b, 0, 2 * SB_HEADS + h)),
        ] + c_in,
        out_specs=[win.spec((1, s, hd), BF, lambda b, h: (b, 0, h))] + c_out,
        out_shape=[jax.ShapeDtypeStruct((bsz, s, w), BF)] + c_shapes,
        compiler_params=win.params(("arbitrary", "arbitrary"), temps=temps),
        name="sb_attn",
    )(qkv, qkv, qkv, *c_ops)
    return outs[0], outs[1:]


def kernel(x, attn_norm_w, ffn_norm_w, gla_w_in, gla_w_gate_up, gla_b_gate, gla_gnorm_w, gla_w_out,
           kv_norm_w, sb_w_kv, sb_w_q, sb_w_out, ffn_w_gate_up, ffn_w_down, final_norm_w):
    bsz, s, d = x.shape
    t = bsz * s
    depth = attn_norm_w.shape[0]
    n_gla = gla_w_in.shape[0]
    rank = gla_w_gate_up.shape[1]
    main_w = gla_w_in.shape[2] - rank
    hd = d // SB_HEADS
    row = lambda v: v.reshape(1, -1)
    assert n_gla >= 1 and depth - n_gla == 1

    w_in_t = jnp.swapaxes(gla_w_in, 1, 2)

    h = x.reshape(t, d)
    for layer in range(depth):
        if layer < n_gla:
            proj, la = _gla_inproj(h, row(attn_norm_w[layer]), w_in_t, layer, main_w, rank,
                                   gla_w_gate_up[layer], row(gla_b_gate[layer]), **TILES["inproj"])
            casts = [[(ffn_w_gate_up, layer)], [(ffn_w_down, layer)], [(gla_w_out, layer)]]
            if layer == n_gla - 1:
                casts.append([(sb_w_q, 0), (sb_w_kv, None)])
            o, cast = _gla_core(proj.reshape(bsz, s, main_w), la.reshape(bsz, s, -1),
                                row(gla_gnorm_w[layer]), casts, **TILES["gla"])
            w_gate_up, w_down, w_out = cast[:3]
            if layer == n_gla - 1:
                w_qkv = cast[3]
        else:
            j = layer - n_gla
            qkv, (w_down, w_out) = _norm_matmul2(
                h, row(attn_norm_w[layer]), row(kv_norm_w), w_qkv, sb_w_q.shape[2],
                [[(ffn_w_down, layer)], [(sb_w_out, j)]], scale_a=hd ** -0.5 * LOG2_E, **TILES["qkv"])
            o, (w_gate_up,) = _sb_attention(qkv.reshape(bsz, s, -1), [[(ffn_w_gate_up, layer)]], **TILES["sb"])
        h = _proj_res(o.reshape(t, -1), w_out, h, **TILES["proj"])
        h = _ffn(h, row(ffn_norm_w[layer]), w_gate_up, w_down, row(final_norm_w),
                 final_norm=layer == depth - 1, **TILES["ffn"])
    return h.reshape(bsz, s, d)
```

```python
import functools

import jax
import jax.numpy as jnp
from jax import lax
from jax.experimental import pallas as pl
from jax.experimental.pallas import tpu as pltpu

EPS = 1e-6
GLA_HEADS = 4
GLA_GATE_TAU = 16.0
SB_HEADS = 16
BF = jnp.bfloat16
F32 = jnp.float32
LANES = 128
BF16_SUBLANES = 16
MIB = 2 ** 20
V7X_VMEM_BYTES = 64 * MIB
VMEM_HEADROOM = 4 * MIB
VMEM_SPILL_ALLOWANCE = 2 * MIB

TILES = dict(
    inproj=dict(tm=1024, tn=768),
    gla=dict(chunk=128, sub=8, heads=GLA_HEADS),
    proj=dict(tm=512),
    ffn=dict(tm=1024, tf=512),
    qkv=dict(tm=1024, tn=1024),
    sb=dict(tq=128, group=8),
)

_NT = (((1,), (1,)), ((), ()))
_TN = (((0,), (0,)), ((), ()))


def _nbytes(shape, dtype):
    n = jnp.dtype(dtype).itemsize
    for dim in shape:
        n *= dim or 1
    return n


class _Windows:
    def __init__(self):
        self.bytes = 0

    def spec(self, block, dtype, index_map):
        self.bytes += _nbytes(block, dtype)
        return pl.BlockSpec(block, index_map)

    def params(self, sem, scratch=(), temps=0):
        need = 2 * self.bytes + sum(_nbytes(shape, dtype) for shape, dtype in scratch) + temps
        need = -(-(need + VMEM_SPILL_ALLOWANCE) // MIB) * MIB
        assert need <= V7X_VMEM_BYTES - VMEM_HEADROOM, need
        return pltpu.CompilerParams(dimension_semantics=sem, vmem_limit_bytes=need)


def _vmem(scratch):
    return [pltpu.VMEM(shape, dtype) for shape, dtype in scratch]


def _row_tile_halves(win, tm, d, n_tiles, n_steps):
    half = tm // 2
    last = 2 * n_tiles - 2

    def top(i, j):
        return jnp.minimum(2 * (i + (j >= n_steps - 2).astype(jnp.int32)), last), 0

    def bottom(i, j):
        return jnp.minimum(2 * (i + (j >= n_steps - 1).astype(jnp.int32)), last) + 1, 0

    return [win.spec((half, d), F32, top), win.spec((half, d), F32, bottom)]


def _dot(a, b):
    return jnp.dot(a, b, preferred_element_type=F32)


def _rms(x, w):
    return x * lax.rsqrt(jnp.mean(x * x, axis=-1, keepdims=True) + EPS) * w


def _split2(x):
    hi = x.astype(BF)
    lo = (x - hi.astype(F32)).astype(BF)
    return hi, lo


def _log_sigmoid(x):
    return jnp.minimum(x, 0.0) - jnp.log(1.0 + jnp.exp(-jnp.abs(x)))


def _silu(x):
    return x * (1.0 / (1.0 + jnp.exp(-x)))


def _cast_specs(win, jobs, grid):
    steps = 1
    for g in grid:
        steps *= g

    def linear(idx):
        s = 0
        for g, i in zip(grid, idx):
            s = s * g + i
        return s

    operands, in_specs, out_specs, out_shapes, parts = [], [], [], [], []
    for job in jobs:
        n_rows = job[0][0].shape[-2]
        n_blk = max(n for n in range(1, steps + 1)
                    if n_rows % n == 0 and (n_rows // n) % BF16_SUBLANES == 0)
        rows = n_rows // n_blk
        blk = lambda *idx, n_blk=n_blk: linear(idx) * n_blk // steps
        for arr, lead in job:
            assert arr.shape[-2] == n_rows
            operands.append(arr)
            if lead is None:
                in_specs.append(win.spec((rows, arr.shape[-1]), F32, lambda *idx, blk=blk: (blk(*idx), 0)))
            else:
                in_specs.append(win.spec((None, rows, arr.shape[-1]), F32,
                                         lambda *idx, blk=blk, lead=lead: (lead, blk(*idx), 0)))
        width = sum(arr.shape[-1] for arr, _ in job)
        out_specs.append(win.spec((rows, width), BF, lambda *idx, blk=blk: (blk(*idx), 0)))
        out_shapes.append(jax.ShapeDtypeStruct((n_rows, width), BF))
        parts.append(len(job))
    return operands, in_specs, out_specs, out_shapes, parts


def _with_casts(body, n_in, n_out, parts, inline=False):
    n_src = sum(parts)

    def kernel(*refs):
        ins, rest = refs[:n_in], refs[n_in:]
        srcs, rest = rest[:n_src], rest[n_src:]
        outs, rest = rest[:n_out], rest[n_out:]
        dsts, scratch = rest[:len(parts)], rest[len(parts):]

        def cast():
            todo = list(srcs)
            for dst, n_parts in zip(dsts, parts):
                col = 0
                for _ in range(n_parts):
                    src = todo.pop(0)
                    dst[:, col:col + src.shape[-1]] = src[...].astype(BF)
                    col += src.shape[-1]

        if inline:
            body(*ins, *outs, *scratch, cast=cast)
        else:
            body(*ins, *outs, *scratch)
            cast()

    return kernel


def _norm_matmul2_kernel(xt_ref, xb_ref, nwa_ref, nwb_ref, w_ref, o_ref, xn_ref, *, na, scale_a, cast):
    j = pl.program_id(1)

    @pl.when(j == 0)
    def _():
        cast()
        half = xt_ref.shape[0]
        for r, ref in enumerate((xt_ref, xb_ref)):
            rows = slice(r * half, (r + 1) * half)
            x = ref[...]
            xhat = x * lax.rsqrt(jnp.mean(x * x, axis=-1, keepdims=True) + EPS)
            xa = (xhat * nwa_ref[...]).astype(BF)
            xn_ref[0, rows] = xa
            xn_ref[1, rows] = (xhat * nwb_ref[...]).astype(BF)
            o_ref[rows] = (_dot(xa, w_ref[...]) * scale_a).astype(o_ref.dtype)

    @pl.when((j > 0) & (j < na))
    def _():
        cast()
        o_ref[...] = (_dot(xn_ref[0], w_ref[...]) * scale_a).astype(o_ref.dtype)

    @pl.when(j >= na)
    def _():
        cast()
        o_ref[...] = _dot(xn_ref[1], w_ref[...]).astype(o_ref.dtype)


def _norm_matmul2(x, nwa, nwb, w, n_a, cast_jobs, *, tm, tn, scale_a):
    t, d = x.shape
    n = w.shape[1]
    grid = (t // tm, n // tn)
    win = _Windows()
    c_ops, c_in, c_out, c_shapes, c_parts = _cast_specs(win, cast_jobs, grid)
    scratch = [((2, tm, d), BF)]
    temps = 2 * _nbytes((tm // 2, d), F32) + _nbytes((tm, tn), F32)
    body = functools.partial(_norm_matmul2_kernel, na=n_a // tn, scale_a=scale_a)
    outs = pl.pallas_call(
        _with_casts(body, 5, 1, c_parts, inline=True),
        grid=grid,
        in_specs=_row_tile_halves(win, tm, d, t // tm, n // tn) + [
            win.spec((1, d), F32, lambda i, j: (0, 0)),
            win.spec((1, d), F32, lambda i, j: (0, 0)),
            win.spec((d, tn), BF, lambda i, j: (0, j)),
        ] + c_in,
        out_specs=[win.spec((tm, tn), BF, lambda i, j: (i, j))] + c_out,
        out_shape=[jax.ShapeDtypeStruct((t, n), BF)] + c_shapes,
        scratch_shapes=_vmem(scratch),
        compiler_params=win.params(("arbitrary", "arbitrary"), scratch, temps),
        name="norm_matmul2",
    )(x, x, nwa, nwb, w, *c_ops)
    return outs[0], outs[1:]


def _gla_inproj_kernel(xt_ref, xb_ref, nw_ref, w_ref, wgl_ref, wup_ref, bg_ref, o_ref, la_ref, xn_ref, gl_ref,
                       *, rank):
    def dot_t(a, w_t):
        return lax.dot_general(a, w_t, _NT, preferred_element_type=F32)

    def gate(gl):
        up_hi, up_lo = _split2(wup_ref[...])
        group = lax.broadcasted_iota(jnp.int32, up_hi.shape, 0) // rank
        g = _dot(gl, jnp.where(group == 2, up_lo, up_hi)) + bg_ref[...]
        la_ref[...] = _log_sigmoid(g) * (1.0 / GLA_GATE_TAU)

    @pl.when(pl.program_id(1) == 0)
    def _():
        x = jnp.concatenate([xt_ref[...], xb_ref[...]], axis=0)
        xn = _rms(x, nw_ref[...]).astype(BF)
        xn_ref[...] = xn
        o_ref[...] = dot_t(xn, w_ref[...].astype(BF)).astype(o_ref.dtype)
        gl_hi, gl_lo = _split2(dot_t(xn, wgl_ref[...]))
        group = lax.broadcasted_iota(jnp.int32, gl_hi.shape, 1) // rank
        gl = jnp.where(group == 1, gl_lo, gl_hi)
        gl_ref[...] = gl
        gate(gl)

    @pl.when(pl.program_id(1) > 0)
    def _():
        gate(gl_ref[...])
        o_ref[...] = dot_t(xn_ref[...], w_ref[...].astype(BF)).astype(o_ref.dtype)


def _gla_inproj(x, nw, w_t_all, layer, n, rank, w_gate_up, bg, *, tm, tn):
    t, d = x.shape
    steps = n // tn
    qk_w = w_gate_up.shape[1]
    slab = qk_w // steps
    assert slab * steps == qk_w and slab % LANES == 0 and 3 * rank <= LANES
    three = lambda a: jnp.pad(jnp.tile(a, (3, 1)), ((0, LANES - 3 * rank), (0, 0)))
    wgl_t = three(w_t_all[layer, n:n + rank]).astype(BF)
    wup = three(w_gate_up)
    win = _Windows()
    scratch = [((tm, d), BF), ((tm, LANES), BF)]
    temps = _nbytes((tm, tn), F32) + _nbytes((tn, d), BF)
    return pl.pallas_call(
        functools.partial(_gla_inproj_kernel, rank=rank),
        grid=(t // tm, steps),
        in_specs=_row_tile_halves(win, tm, d, t // tm, steps) + [
            win.spec((1, d), F32, lambda i, j: (0, 0)),
            win.spec((None, tn, d), F32, lambda i, j: (layer, j, 0)),
            win.spec((LANES, d), BF, lambda i, j: (0, 0)),
            win.spec((LANES, slab), F32, lambda i, j: (0, j)),
            win.spec((1, slab), F32, lambda i, j: (0, j)),
        ],
        out_specs=[
            win.spec((tm, tn), BF, lambda i, j: (i, j)),
            win.spec((tm, slab), F32, lambda i, j: (i, j)),
        ],
        out_shape=[
            jax.ShapeDtypeStruct((t, n), BF),
            jax.ShapeDtypeStruct((t, qk_w), F32),
        ],
        scratch_shapes=_vmem(scratch),
        compiler_params=win.params(("parallel", "arbitrary"), scratch, temps),
        name="gla_inproj",
    )(x, x, nw, w_t_all, wgl_t, wup, bg)


def _gla_kernel(q_ref, k_ref, v_ref, r_ref, g_ref, gw_ref, o_ref, st_ref, qh_ref, kh_ref,
                *, chunk, sub, heads, scale):
    n_sub = chunk // sub
    dk = q_ref.shape[-1] // heads
    dv = v_ref.shape[-1] // heads
    hs = range(heads)

    @pl.when(pl.program_id(2) == 0)
    def _():
        st_ref[...] = jnp.zeros_like(st_ref)
        qh_ref[...] = jnp.zeros_like(qh_ref)
        kh_ref[...] = jnp.zeros_like(kh_ref)

    def head_cols(ref, h, width):
        return ref[0, :, h * width:(h + 1) * width]

    q = [head_cols(q_ref, h, dk).astype(F32) * scale for h in hs]
    k = [head_cols(k_ref, h, dk).astype(F32) for h in hs]
    g = [head_cols(g_ref, h, dk) for h in hs]

    row = lax.broadcasted_iota(jnp.int32, (chunk, chunk), 0)
    col = lax.broadcasted_iota(jnp.int32, (chunk, chunk), 1)
    tri = jnp.where(col <= row, 1.0, 0.0).astype(BF)
    diff = jnp.where(row // sub == col // sub, row - col, -1)

    b = []
    for h in hs:
        g_hi, g_lo = _split2(g[h])
        b.append(_dot(tri, g_hi) + _dot(tri, g_lo))

    st = [st_ref[h] for h in hs]
    o = [lax.dot_general((q[h] * jnp.exp(b[h])).astype(BF), st[h].astype(BF), _NT,
                         preferred_element_type=F32) for h in hs]

    def store_rows(ref, h, start, cols, x):
        lo = start // BF16_SUBLANES * BF16_SUBLANES
        hi = -(-(start + x.shape[0]) // BF16_SUBLANES) * BF16_SUBLANES
        pieces = [jnp.zeros((start - lo, x.shape[1]), F32)] if start > lo else []
        pieces.append(x)
        if hi > start + x.shape[0]:
            pieces.append(jnp.zeros((hi - start - x.shape[0], x.shape[1]), F32))
        x = jnp.concatenate(pieces, axis=0) if len(pieces) > 1 else x
        ref[h, lo:hi, cols] = x.astype(BF)

    scores = []
    for h in hs:
        k_run = None
        for i in range(1, n_sub):
            beta = b[h][i * sub - 1:i * sub, :]
            rows = slice(i * sub, (i + 1) * sub)
            prev = slice((i - 1) * sub, i * sub)
            cols = slice((i - 1) * dk, i * dk)
            store_rows(qh_ref, h, i * sub, cols, q[h][rows] * jnp.exp(b[h][rows] - beta))
            k_new = k[h][prev] * jnp.exp(beta - b[h][prev])
            if k_run is None:
                k_run = k_new
            else:
                k_run = jnp.concatenate([k_run * jnp.exp(beta - beta_prev), k_new], axis=0)
            store_rows(kh_ref, h, 0, cols, k_run)
            beta_prev = beta
        scores.append(lax.dot_general(qh_ref[h], kh_ref[h], _NT, preferred_element_type=F32))

    for h in hs:
        decay = jnp.exp(g[h])
        w = k[h]
        for d in range(sub):
            if d > 0:
                w = decay * pltpu.roll(w, 1, 0)
            scores[h] = jnp.where(diff == d, jnp.sum(q[h] * w, axis=-1, keepdims=True), scores[h])

    v = [head_cols(v_ref, h, dv) for h in hs]
    o = [o[h] + _dot(scores[h].astype(BF), v[h]) for h in hs]

    for h in hs:
        b_last = b[h][chunk - 1:chunk, :]
        k_dec = (k[h] * jnp.exp(b_last - b[h])).astype(BF)
        st_ref[h] = st[h] * jnp.exp(b_last) + lax.dot_general(v[h], k_dec, _TN, preferred_element_type=F32)

    for h in hs:
        r = head_cols(r_ref, h, dv).astype(F32)
        o_ref[0, :, h * dv:(h + 1) * dv] = (_rms(o[h], gw_ref[...]) * _silu(r)).astype(o_ref.dtype)


def _gla_core(proj, la, gw, cast_jobs, *, chunk, sub, heads):
    bsz, s, _ = proj.shape
    dk = la.shape[-1] // GLA_HEADS
    dv = gw.shape[-1]
    groups = GLA_HEADS // heads
    k_blocks = groups
    v_blocks = 2 * GLA_HEADS * dk // (heads * dv)
    r_blocks = v_blocks + groups
    expanded = (chunk // sub - 1) * dk
    grid = (bsz, groups, s // chunk)
    win = _Windows()
    c_ops, c_in, c_out, c_shapes, c_parts = _cast_specs(win, cast_jobs, grid)
    scratch = [((heads, dv, dk), F32), ((heads, chunk, expanded), BF), ((heads, chunk, expanded), BF)]
    temps = heads * (6 * _nbytes((chunk, dk), F32) + 2 * _nbytes((chunk, dv), F32))
    body = functools.partial(_gla_kernel, chunk=chunk, sub=sub, heads=heads, scale=dk ** -0.5)
    outs = pl.pallas_call(
        _with_casts(body, 6, 1, c_parts),
        grid=grid,
        in_specs=[
            win.spec((1, chunk, heads * dk), BF, lambda b, h, n: (b, n, h)),
            win.spec((1, chunk, heads * dk), BF, lambda b, h, n: (b, n, k_blocks + h)),
            win.spec((1, chunk, heads * dv), BF, lambda b, h, n: (b, n, v_blocks + h)),
            win.spec((1, chunk, heads * dv), BF, lambda b, h, n: (b, n, r_blocks + h)),
            win.spec((1, chunk, heads * dk), F32, lambda b, h, n: (b, n, h)),
            win.spec((1, dv), F32, lambda b, h, n: (0, 0)),
        ] + c_in,
        out_specs=[win.spec((1, chunk, heads * dv), BF, lambda b, h, n: (b, n, h))] + c_out,
        out_shape=[jax.ShapeDtypeStruct((bsz, s, GLA_HEADS * dv), BF)] + c_shapes,
        scratch_shapes=_vmem(scratch),
        compiler_params=win.params(("arbitrary", "arbitrary", "arbitrary"), scratch, temps),
        name="gla_core",
    )(proj, proj, proj, proj, la, gw, *c_ops)
    return outs[0], outs[1:]


def _proj_res_kernel(x_ref, w_ref, res_ref, o_ref):
    o_ref[...] = res_ref[...] + _dot(x_ref[...], w_ref[...])


def _proj_res(x, w, res, *, tm):
    t, kdim = x.shape
    n = w.shape[1]
    win = _Windows()
    return pl.pallas_call(
        _proj_res_kernel,
        grid=(t // tm,),
        in_specs=[
            win.spec((tm, kdim), BF, lambda i: (i, 0)),
            win.spec((kdim, n), BF, lambda i: (0, 0)),
            win.spec((tm, n), F32, lambda i: (i, 0)),
        ],
        out_specs=win.spec((tm, n), F32, lambda i: (i, 0)),
        out_shape=jax.ShapeDtypeStruct((t, n), F32),
        compiler_params=win.params(("parallel",), temps=_nbytes((tm, n), F32)),
        name="proj_res",
    )(x, w, res)


def _ffn_kernel(h_ref, nw_ref, wg_ref, wu_ref, wd_ref, fw_ref, o_ref, xn_ref, *, final_norm, cast):
    f = pl.program_id(1)

    def step(acc_ref):
        cast()
        xn = xn_ref[...]
        act = (_silu(_dot(xn, wg_ref[...])) * _dot(xn, wu_ref[...])).astype(BF)
        o_ref[...] = acc_ref[...] + _dot(act, wd_ref[...])

    @pl.when(f == 0)
    def _():
        xn_ref[...] = _rms(h_ref[...], nw_ref[...]).astype(BF)
        step(h_ref)

    @pl.when(f > 0)
    def _():
        step(o_ref)

    if final_norm:
        @pl.when(f == pl.num_programs(1) - 1)
        def _():
            o_ref[...] = _rms(o_ref[...], fw_ref[...])


def _ffn(h, nw, w_gate_up, w_down, fw, cast_jobs, *, tm, tf, final_norm):
    t, d = h.shape
    d_ff = w_down.shape[0]
    nf = d_ff // tf
    grid = (t // tm, nf)
    win = _Windows()
    c_ops, c_in, c_out, c_shapes, c_parts = _cast_specs(win, cast_jobs, grid)
    scratch = [((tm, d), BF)]
    temps = 2 * _nbytes((tm, tf), F32) + _nbytes((tm, tf), BF)
    outs = pl.pallas_call(
        _with_casts(functools.partial(_ffn_kernel, final_norm=final_norm), 6, 1, c_parts, inline=True),
        grid=grid,
        in_specs=[
            win.spec((tm, d), F32, lambda i, f: (i, 0)),
            win.spec((1, d), F32, lambda i, f: (0, 0)),
            win.spec((d, tf), BF, lambda i, f: (0, f)),
            win.spec((d, tf), BF, lambda i, f: (0, nf + f)),
            win.spec((tf, d), BF, lambda i, f: (f, 0)),
            win.spec((1, d), F32, lambda i, f: (0, 0)),
        ] + c_in,
        out_specs=[win.spec((tm, d), F32, lambda i, f: (i, 0))] + c_out,
        out_shape=[jax.ShapeDtypeStruct((t, d), F32)] + c_shapes,
        scratch_shapes=_vmem(scratch),
        compiler_params=win.params(("arbitrary", "arbitrary"), scratch, temps),
        name="ffn",
    )(h, nw, w_gate_up, w_gate_up, w_down, fw, *c_ops)
    return outs[0], outs[1:]


_EXP2_UNDERFLOW = -127.0
LOG2_E = 1.4426950408889634


def _sb_kernel(q_ref, k_ref, v_ref, o_ref, *, tq, group):
    s = q_ref.shape[1]
    tw = 2 * tq
    def later(n):
        r = lax.broadcasted_iota(jnp.int32, (n, n), 0)
        c = lax.broadcasted_iota(jnp.int32, (n, n), 1)
        return jnp.where(r > c, 1.0, 0.0).astype(BF)

    def causal(n_keys, offset):
        qr = lax.broadcasted_iota(jnp.int32, (tq, n_keys), 0)
        kc = lax.broadcasted_iota(jnp.int32, (tq, n_keys), 1)
        return kc < qr + offset

    later_w, later_b = later(tw), later(tq)
    diag_mask = causal(tq, 0)
    win_mask = causal(tw, tq)

    def scores(q, start, width, mask):
        kb = k_ref[0, pl.ds(start, width), :]
        z = lax.dot_general(q, kb, _NT, preferred_element_type=F32)
        nz = -z
        lf = jnp.minimum(nz, 0.0) - jnp.log2(1.0 + jnp.exp2(jnp.minimum(z, nz)))
        if mask is not None:
            lf = jnp.where(mask, lf, 0.0)
        return z, lf

    def suffix(lf, later, carry):
        after = _dot(lf.astype(BF), later)
        if carry is not None:
            after = after + carry
        return after, after[:, :1] + lf[:, :1]

    def weighted(z, lf, after, start, width, mask):
        a = jnp.exp2(z + lf + after)
        if mask is not None:
            a = jnp.where(mask, a, 0.0)
        return _dot(a.astype(BF), v_ref[0, pl.ds(start, width), :])

    def window(tile):
        if tile == 0:
            return 0, tq, later_b, diag_mask
        return (tile - 1) * tq, tw, later_w, win_mask

    def extend(tile, q, carry, top, acc):
        def cond(st):
            return jnp.logical_and(st[0] >= 0, st[1] > _EXP2_UNDERFLOW)

        def body(st):
            j, _, carry, acc = st
            start = pl.multiple_of(j * tq, tq)
            z, lf = scores(q, start, tq, None)
            after, carry = suffix(lf, later_b, carry)
            acc = acc + weighted(z, lf, after, start, tq, None)
            return j - 1, jnp.max(carry), carry, acc

        init = (jnp.asarray(tile - 2, jnp.int32), top, carry, acc)
        return lax.while_loop(cond, body, init)[3]

    def first_passes(base):
        tiles = [base + g for g in range(group)]
        wins = [window(t) for t in tiles]
        qs = [q_ref[0, t * tq:(t + 1) * tq, :] for t in tiles]
        zl = [scores(q, w[0], w[1], w[3]) for q, w in zip(qs, wins)]
        ac = [suffix(lf, w[2], None) for (_, lf), w in zip(zl, wins)]
        accs = [weighted(z, lf, after, w[0], w[1], w[3]) for (z, lf), (after, _), w in zip(zl, ac, wins)]
        tops = [jnp.max(carry) for _, carry in ac]
        return list(zip(tiles, qs, [carry for _, carry in ac], tops, accs))

    todo = []
    for base in range(0, s // tq, group):
        todo += first_passes(base)
    for t, q, carry, top, acc in todo:
        acc = extend(t, q, carry, top, acc)
        o_ref[0, t * tq:(t + 1) * tq, :] = acc.astype(o_ref.dtype)


def _sb_attention(qkv, cast_jobs, *, tq, group):
    bsz, s, w3 = qkv.shape
    w = w3 // 3
    hd = w // SB_HEADS
    grid = (bsz, SB_HEADS)
    win = _Windows()
    c_ops, c_in, c_out, c_shapes, c_parts = _cast_specs(win, cast_jobs, grid)
    temps = group * 8 * _nbytes((tq, 2 * tq), F32)
    outs = pl.pallas_call(
        _with_casts(functools.partial(_sb_kernel, tq=tq, group=group), 3, 1, c_parts),
        grid=grid,
        in_specs=[
            win.spec((1, s, hd), BF, lambda b, h: (b, 0, h)),
            win.spec((1, s, hd), BF, lambda b, h: (b, 0, SB_HEADS + h)),
            win.spec((1, s, hd), BF, lambda b, h: (b, 0, 2 * SB_HEADS + h)),
        ] + c_in,
        out_specs=[win.spec((1, s, hd), BF, lambda b, h: (b, 0, h))] + c_out,
        out_shape=[jax.ShapeDtypeStruct((bsz, s, w), BF)] + c_shapes,
        compiler_params=win.params(("arbitrary", "arbitrary"), temps=temps),
        name="sb_attn",
    )(qkv, qkv, qkv, *c_ops)
    return outs[0], outs[1:]


def kernel(x, attn_norm_w, ffn_norm_w, gla_w_in, gla_w_gate_up, gla_b_gate, gla_gnorm_w, gla_w_out,
           kv_norm_w, sb_w_kv, sb_w_q, sb_w_out, ffn_w_gate_up, ffn_w_down, final_norm_w):
    bsz, s, d = x.shape
    t = bsz * s
    depth = attn_norm_w.shape[0]
    n_gla = gla_w_in.shape[0]
    rank = gla_w_gate_up.shape[1]
    main_w = gla_w_in.shape[2] - rank
    hd = d // SB_HEADS
    row = lambda v: v.reshape(1, -1)
    assert n_gla >= 1 and depth - n_gla == 1

    w_in_t = jnp.swapaxes(gla_w_in, 1, 2)

    h = x.reshape(t, d)
    for layer in range(depth):
        if layer < n_gla:
            proj, la = _gla_inproj(h, row(attn_norm_w[layer]), w_in_t, layer, main_w, rank,
                                   gla_w_gate_up[layer], row(gla_b_gate[layer]), **TILES["inproj"])
            casts = [[(ffn_w_gate_up, layer)], [(ffn_w_down, layer)], [(gla_w_out, layer)]]
            o, (w_gate_up, w_down, w_out) = _gla_core(proj.reshape(bsz, s, main_w), la.reshape(bsz, s, -1),
                                                      row(gla_gnorm_w[layer]), casts, **TILES["gla"])
            ffn_casts = [[(sb_w_q, 0), (sb_w_kv, None)]] if layer == n_gla - 1 else []
        else:
            j = layer - n_gla
            qkv, (w_down, w_out) = _norm_matmul2(
                h, row(attn_norm_w[layer]), row(kv_norm_w), w_qkv, sb_w_q.shape[2],
                [[(ffn_w_down, layer)], [(sb_w_out, j)]], scale_a=hd ** -0.5 * LOG2_E, **TILES["qkv"])
            o, (w_gate_up,) = _sb_attention(qkv.reshape(bsz, s, -1), [[(ffn_w_gate_up, layer)]], **TILES["sb"])
            ffn_casts = []
        h = _proj_res(o.reshape(t, -1), w_out, h, **TILES["proj"])
        h, ffn_cast = _ffn(h, row(ffn_norm_w[layer]), w_gate_up, w_down, row(final_norm_w), ffn_casts,
                           final_norm=layer == depth - 1, **TILES["ffn"])
        if ffn_casts:
            (w_qkv,) = ffn_cast
    return h.reshape(bsz, s, d)
```

```python
import functools

import jax
import jax.numpy as jnp
from jax import lax
from jax.experimental import pallas as pl
from jax.experimental.pallas import tpu as pltpu

EPS = 1e-6
GLA_HEADS = 4
GLA_GATE_TAU = 16.0
SB_HEADS = 16
BF = jnp.bfloat16
F32 = jnp.float32
LANES = 128
BF16_SUBLANES = 16
MIB = 2 ** 20
V7X_VMEM_BYTES = 64 * MIB
VMEM_HEADROOM = 4 * MIB
VMEM_SPILL_ALLOWANCE = 2 * MIB

TILES = dict(
    inproj=dict(tm=1024, tn=768),
    gla=dict(chunk=128, sub=8, heads=GLA_HEADS),
    proj=dict(tm=512),
    ffn=dict(tm=1024, tf=512),
    qkv=dict(tm=1024, tn=1024),
    sb=dict(tq=128, group=8),
)

_NT = (((1,), (1,)), ((), ()))
_TN = (((0,), (0,)), ((), ()))


def _nbytes(shape, dtype):
    n = jnp.dtype(dtype).itemsize
    for dim in shape:
        n *= dim or 1
    return n


class _Windows:
    def __init__(self):
        self.bytes = 0

    def spec(self, block, dtype, index_map):
        self.bytes += _nbytes(block, dtype)
        return pl.BlockSpec(block, index_map)

    def params(self, sem, scratch=(), temps=0):
        need = 2 * self.bytes + sum(_nbytes(shape, dtype) for shape, dtype in scratch) + temps
        need = -(-(need + VMEM_SPILL_ALLOWANCE) // MIB) * MIB
        assert need <= V7X_VMEM_BYTES - VMEM_HEADROOM, need
        return pltpu.CompilerParams(dimension_semantics=sem, vmem_limit_bytes=need)


def _vmem(scratch):
    return [pltpu.VMEM(shape, dtype) for shape, dtype in scratch]


def _row_tile_halves(win, tm, d, n_tiles, n_steps):
    half = tm // 2
    last = 2 * n_tiles - 2

    def top(i, j):
        return jnp.minimum(2 * (i + (j >= n_steps - 2).astype(jnp.int32)), last), 0

    def bottom(i, j):
        return jnp.minimum(2 * (i + (j >= n_steps - 1).astype(jnp.int32)), last) + 1, 0

    return [win.spec((half, d), F32, top), win.spec((half, d), F32, bottom)]


def _dot(a, b):
    return jnp.dot(a, b, preferred_element_type=F32)


def _rms(x, w):
    return x * lax.rsqrt(jnp.mean(x * x, axis=-1, keepdims=True) + EPS) * w


def _split2(x):
    hi = x.astype(BF)
    lo = (x - hi.astype(F32)).astype(BF)
    return hi, lo


def _log_sigmoid(x):
    return jnp.minimum(x, 0.0) - jnp.log(1.0 + jnp.exp(-jnp.abs(x)))


def _silu(x):
    return x * (1.0 / (1.0 + jnp.exp(-x)))


def _cast_specs(win, jobs, grid):
    steps = 1
    for g in grid:
        steps *= g

    def linear(idx):
        s = 0
        for g, i in zip(grid, idx):
            s = s * g + i
        return s

    operands, in_specs, out_specs, out_shapes, parts = [], [], [], [], []
    for job in jobs:
        n_rows = job[0][0].shape[-2]
        n_blk = max(n for n in range(1, steps + 1)
                    if n_rows % n == 0 and (n_rows // n) % BF16_SUBLANES == 0)
        rows = n_rows // n_blk
        blk = lambda *idx, n_blk=n_blk: linear(idx) * n_blk // steps
        for arr, lead in job:
            assert arr.shape[-2] == n_rows
            operands.append(arr)
            if lead is None:
                in_specs.append(win.spec((rows, arr.shape[-1]), F32, lambda *idx, blk=blk: (blk(*idx), 0)))
            else:
                in_specs.append(win.spec((None, rows, arr.shape[-1]), F32,
                                         lambda *idx, blk=blk, lead=lead: (lead, blk(*idx), 0)))
        width = sum(arr.shape[-1] for arr, _ in job)
        out_specs.append(win.spec((rows, width), BF, lambda *idx, blk=blk: (blk(*idx), 0)))
        out_shapes.append(jax.ShapeDtypeStruct((n_rows, width), BF))
        parts.append(len(job))
    return operands, in_specs, out_specs, out_shapes, parts


def _with_casts(body, n_in, n_out, parts, inline=False):
    n_src = sum(parts)

    def kernel(*refs):
        ins, rest = refs[:n_in], refs[n_in:]
        srcs, rest = rest[:n_src], rest[n_src:]
        outs, rest = rest[:n_out], rest[n_out:]
        dsts, scratch = rest[:len(parts)], rest[len(parts):]

        def cast():
            todo = list(srcs)
            for dst, n_parts in zip(dsts, parts):
                col = 0
                for _ in range(n_parts):
                    src = todo.pop(0)
                    dst[:, col:col + src.shape[-1]] = src[...].astype(BF)
                    col += src.shape[-1]

        if inline:
            body(*ins, *outs, *scratch, cast=cast)
        else:
            body(*ins, *outs, *scratch)
            cast()

    return kernel


def _norm_matmul2_kernel(xt_ref, xb_ref, nwa_ref, nwb_ref, w_ref, o_ref, xn_ref, *, na, scale_a, cast):
    j = pl.program_id(1)

    @pl.when(j == 0)
    def _():
        cast()
        half = xt_ref.shape[0]
        for r, ref in enumerate((xt_ref, xb_ref)):
            rows = slice(r * half, (r + 1) * half)
            x = ref[...]
            xhat = x * lax.rsqrt(jnp.mean(x * x, axis=-1, keepdims=True) + EPS)
            xa = (xhat * nwa_ref[...]).astype(BF)
            xn_ref[0, rows] = xa
            xn_ref[1, rows] = (xhat * nwb_ref[...]).astype(BF)
            o_ref[rows] = (_dot(xa, w_ref[...]) * scale_a).astype(o_ref.dtype)

    @pl.when((j > 0) & (j < na))
    def _():
        cast()
        o_ref[...] = (_dot(xn_ref[0], w_ref[...]) * scale_a).astype(o_ref.dtype)

    @pl.when(j >= na)
    def _():
        cast()
        o_ref[...] = _dot(xn_ref[1], w_ref[...]).astype(o_ref.dtype)


def _norm_matmul2(x, nwa, nwb, w, n_a, cast_jobs, *, tm, tn, scale_a):
    t, d = x.shape
    n = w.shape[1]
    grid = (t // tm, n // tn)
    win = _Windows()
    c_ops, c_in, c_out, c_shapes, c_parts = _cast_specs(win, cast_jobs, grid)
    scratch = [((2, tm, d), BF)]
    temps = 2 * _nbytes((tm // 2, d), F32) + _nbytes((tm, tn), F32)
    body = functools.partial(_norm_matmul2_kernel, na=n_a // tn, scale_a=scale_a)
    outs = pl.pallas_call(
        _with_casts(body, 5, 1, c_parts, inline=True),
        grid=grid,
        in_specs=_row_tile_halves(win, tm, d, t // tm, n // tn) + [
            win.spec((1, d), F32, lambda i, j: (0, 0)),
            win.spec((1, d), F32, lambda i, j: (0, 0)),
            win.spec((d, tn), BF, lambda i, j: (0, j)),
        ] + c_in,
        out_specs=[win.spec((tm, tn), BF, lambda i, j: (i, j))] + c_out,
        out_shape=[jax.ShapeDtypeStruct((t, n), BF)] + c_shapes,
        scratch_shapes=_vmem(scratch),
        compiler_params=win.params(("arbitrary", "arbitrary"), scratch, temps),
        name="norm_matmul2",
    )(x, x, nwa, nwb, w, *c_ops)
    return outs[0], outs[1:]


def _gla_inproj_kernel(x_hbm, nw_ref, w_ref, wgl_ref, wup_ref, bg_ref, o_ref, la_ref, xn_ref, gl_ref,
                       xbuf_ref, xsem, *, rank):
    tm = xbuf_ref.shape[1]
    n_tiles = pl.num_programs(0)

    def x_copy(tile, slot):
        rows = pl.ds(pl.multiple_of(tile * tm, tm), tm)
        return pltpu.make_async_copy(x_hbm.at[rows], xbuf_ref.at[slot], xsem.at[slot])

    def dot_t(a, w_t):
        return lax.dot_general(a, w_t, _NT, preferred_element_type=F32)

    def gate(gl):
        up_hi, up_lo = _split2(wup_ref[...])
        group = lax.broadcasted_iota(jnp.int32, up_hi.shape, 0) // rank
        g = _dot(gl, jnp.where(group == 2, up_lo, up_hi)) + bg_ref[...]
        la_ref[...] = _log_sigmoid(g) * (1.0 / GLA_GATE_TAU)

    @pl.when(pl.program_id(1) == 0)
    def _():
        i = pl.program_id(0)
        slot = i % 2

        @pl.when(i == 0)
        def _():
            x_copy(0, 0).start()

        x_copy(i, slot).wait()

        @pl.when(i + 1 < n_tiles)
        def _():
            x_copy(i + 1, 1 - slot).start()

        xn = _rms(xbuf_ref[slot], nw_ref[...]).astype(BF)
        xn_ref[...] = xn
        o_ref[...] = dot_t(xn, w_ref[...].astype(BF)).astype(o_ref.dtype)
        gl_hi, gl_lo = _split2(dot_t(xn, wgl_ref[...]))
        group = lax.broadcasted_iota(jnp.int32, gl_hi.shape, 1) // rank
        gl = jnp.where(group == 1, gl_lo, gl_hi)
        gl_ref[...] = gl
        gate(gl)

    @pl.when(pl.program_id(1) > 0)
    def _():
        gate(gl_ref[...])
        o_ref[...] = dot_t(xn_ref[...], w_ref[...].astype(BF)).astype(o_ref.dtype)


def _gla_inproj(x, nw, w_t_all, layer, n, rank, w_gate_up, bg, *, tm, tn):
    t, d = x.shape
    steps = n // tn
    qk_w = w_gate_up.shape[1]
    slab = qk_w // steps
    assert slab * steps == qk_w and slab % LANES == 0 and 3 * rank <= LANES
    three = lambda a: jnp.pad(jnp.tile(a, (3, 1)), ((0, LANES - 3 * rank), (0, 0)))
    wgl_t = three(w_t_all[layer, n:n + rank]).astype(BF)
    wup = three(w_gate_up)
    win = _Windows()
    scratch = [((tm, d), BF), ((tm, LANES), BF), ((2, tm, d), F32)]
    temps = _nbytes((tm, tn), F32) + _nbytes((tn, d), BF)
    return pl.pallas_call(
        functools.partial(_gla_inproj_kernel, rank=rank),
        grid=(t // tm, steps),
        in_specs=[
            pl.BlockSpec(memory_space=pl.ANY),
            win.spec((1, d), F32, lambda i, j: (0, 0)),
            win.spec((None, tn, d), F32, lambda i, j: (layer, j, 0)),
            win.spec((LANES, d), BF, lambda i, j: (0, 0)),
            win.spec((LANES, slab), F32, lambda i, j: (0, j)),
            win.spec((1, slab), F32, lambda i, j: (0, j)),
        ],
        out_specs=[
            win.spec((tm, tn), BF, lambda i, j: (i, j)),
            win.spec((tm, slab), F32, lambda i, j: (i, j)),
        ],
        out_shape=[
            jax.ShapeDtypeStruct((t, n), BF),
            jax.ShapeDtypeStruct((t, qk_w), F32),
        ],
        scratch_shapes=_vmem(scratch) + [pltpu.SemaphoreType.DMA((2,))],
        compiler_params=win.params(("arbitrary", "arbitrary"), scratch, temps),
        name="gla_inproj",
    )(x, nw, w_t_all, wgl_t, wup, bg)


def _gla_kernel(q_ref, k_ref, v_ref, r_ref, g_ref, gw_ref, o_ref, st_ref, qh_ref, kh_ref,
                *, chunk, sub, heads, scale):
    n_sub = chunk // sub
    dk = q_ref.shape[-1] // heads
    dv = v_ref.shape[-1] // heads
    hs = range(heads)

    @pl.when(pl.program_id(2) == 0)
    def _():
        st_ref[...] = jnp.zeros_like(st_ref)
        qh_ref[...] = jnp.zeros_like(qh_ref)
        kh_ref[...] = jnp.zeros_like(kh_ref)

    def head_cols(ref, h, width):
        return ref[0, :, h * width:(h + 1) * width]

    q = [head_cols(q_ref, h, dk).astype(F32) * scale for h in hs]
    k = [head_cols(k_ref, h, dk).astype(F32) for h in hs]
    g = [head_cols(g_ref, h, dk) for h in hs]

    row = lax.broadcasted_iota(jnp.int32, (chunk, chunk), 0)
    col = lax.broadcasted_iota(jnp.int32, (chunk, chunk), 1)
    tri = jnp.where(col <= row, 1.0, 0.0).astype(BF)
    diff = jnp.where(row // sub == col // sub, row - col, -1)

    b = []
    for h in hs:
        g_hi, g_lo = _split2(g[h])
        b.append(_dot(tri, g_hi) + _dot(tri, g_lo))

    st = [st_ref[h] for h in hs]
    o = [lax.dot_general((q[h] * jnp.exp(b[h])).astype(BF), st[h].astype(BF), _NT,
                         preferred_element_type=F32) for h in hs]

    def store_rows(ref, h, start, cols, x):
        lo = start // BF16_SUBLANES * BF16_SUBLANES
        hi = -(-(start + x.shape[0]) // BF16_SUBLANES) * BF16_SUBLANES
        pieces = [jnp.zeros((start - lo, x.shape[1]), F32)] if start > lo else []
        pieces.append(x)
        if hi > start + x.shape[0]:
            pieces.append(jnp.zeros((hi - start - x.shape[0], x.shape[1]), F32))
        x = jnp.concatenate(pieces, axis=0) if len(pieces) > 1 else x
        ref[h, lo:hi, cols] = x.astype(BF)

    scores = []
    for h in hs:
        k_run = None
        for i in range(1, n_sub):
            beta = b[h][i * sub - 1:i * sub, :]
            rows = slice(i * sub, (i + 1) * sub)
            prev = slice((i - 1) * sub, i * sub)
            cols = slice((i - 1) * dk, i * dk)
            store_rows(qh_ref, h, i * sub, cols, q[h][rows] * jnp.exp(b[h][rows] - beta))
            k_new = k[h][prev] * jnp.exp(beta - b[h][prev])
            if k_run is None:
                k_run = k_new
            else:
                k_run = jnp.concatenate([k_run * jnp.exp(beta - beta_prev), k_new], axis=0)
            store_rows(kh_ref, h, 0, cols, k_run)
            beta_prev = beta
        scores.append(lax.dot_general(qh_ref[h], kh_ref[h], _NT, preferred_element_type=F32))

    for h in hs:
        decay = jnp.exp(g[h])
        w = k[h]
        for d in range(sub):
            if d > 0:
                w = decay * pltpu.roll(w, 1, 0)
            scores[h] = jnp.where(diff == d, jnp.sum(q[h] * w, axis=-1, keepdims=True), scores[h])

    v = [head_cols(v_ref, h, dv) for h in hs]
    o = [o[h] + _dot(scores[h].astype(BF), v[h]) for h in hs]

    for h in hs:
        b_last = b[h][chunk - 1:chunk, :]
        k_dec = (k[h] * jnp.exp(b_last - b[h])).astype(BF)
        st_ref[h] = st[h] * jnp.exp(b_last) + lax.dot_general(v[h], k_dec, _TN, preferred_element_type=F32)

    for h in hs:
        r = head_cols(r_ref, h, dv).astype(F32)
        o_ref[0, :, h * dv:(h + 1) * dv] = (_rms(o[h], gw_ref[...]) * _silu(r)).astype(o_ref.dtype)


def _gla_core(proj, la, gw, cast_jobs, *, chunk, sub, heads):
    bsz, s, _ = proj.shape
    dk = la.shape[-1] // GLA_HEADS
    dv = gw.shape[-1]
    groups = GLA_HEADS // heads
    k_blocks = groups
    v_blocks = 2 * GLA_HEADS * dk // (heads * dv)
    r_blocks = v_blocks + groups
    expanded = (chunk // sub - 1) * dk
    grid = (bsz, groups, s // chunk)
    win = _Windows()
    c_ops, c_in, c_out, c_shapes, c_parts = _cast_specs(win, cast_jobs, grid)
    scratch = [((heads, dv, dk), F32), ((heads, chunk, expanded), BF), ((heads, chunk, expanded), BF)]
    temps = heads * (6 * _nbytes((chunk, dk), F32) + 2 * _nbytes((chunk, dv), F32))
    body = functools.partial(_gla_kernel, chunk=chunk, sub=sub, heads=heads, scale=dk ** -0.5)
    outs = pl.pallas_call(
        _with_casts(body, 6, 1, c_parts),
        grid=grid,
        in_specs=[
            win.spec((1, chunk, heads * dk), BF, lambda b, h, n: (b, n, h)),
            win.spec((1, chunk, heads * dk), BF, lambda b, h, n: (b, n, k_blocks + h)),
            win.spec((1, chunk, heads * dv), BF, lambda b, h, n: (b, n, v_blocks + h)),
            win.spec((1, chunk, heads * dv), BF, lambda b, h, n: (b, n, r_blocks + h)),
            win.spec((1, chunk, heads * dk), F32, lambda b, h, n: (b, n, h)),
            win.spec((1, dv), F32, lambda b, h, n: (0, 0)),
        ] + c_in,
        out_specs=[win.spec((1, chunk, heads * dv), BF, lambda b, h, n: (b, n, h))] + c_out,
        out_shape=[jax.ShapeDtypeStruct((bsz, s, GLA_HEADS * dv), BF)] + c_shapes,
        scratch_shapes=_vmem(scratch),
        compiler_params=win.params(("arbitrary", "arbitrary", "arbitrary"), scratch, temps),
        name="gla_core",
    )(proj, proj, proj, proj, la, gw, *c_ops)
    return outs[0], outs[1:]


def _proj_res_kernel(x_ref, w_ref, res_ref, o_ref):
    o_ref[...] = res_ref[...] + _dot(x_ref[...], w_ref[...])


def _proj_res(x, w, res, *, tm):
    t, kdim = x.shape
    n = w.shape[1]
    win = _Windows()
    return pl.pallas_call(
        _proj_res_kernel,
        grid=(t // tm,),
        in_specs=[
            win.spec((tm, kdim), BF, lambda i: (i, 0)),
            win.spec((kdim, n), BF, lambda i: (0, 0)),
            win.spec((tm, n), F32, lambda i: (i, 0)),
        ],
        out_specs=win.spec((tm, n), F32, lambda i: (i, 0)),
        out_shape=jax.ShapeDtypeStruct((t, n), F32),
        compiler_params=win.params(("parallel",), temps=_nbytes((tm, n), F32)),
        name="proj_res",
    )(x, w, res)


def _ffn_kernel(h_ref, nw_ref, wg_ref, wu_ref, wd_ref, fw_ref, o_ref, xn_ref, *, final_norm):
    f = pl.program_id(1)

    def step(acc_ref):
        xn = xn_ref[...]
        act = (_silu(_dot(xn, wg_ref[...])) * _dot(xn, wu_ref[...])).astype(BF)
        o_ref[...] = acc_ref[...] + _dot(act, wd_ref[...])

    @pl.when(f == 0)
    def _():
        xn_ref[...] = _rms(h_ref[...], nw_ref[...]).astype(BF)
        step(h_ref)

    @pl.when(f > 0)
    def _():
        step(o_ref)

    if final_norm:
        @pl.when(f == pl.num_programs(1) - 1)
        def _():
            o_ref[...] = _rms(o_ref[...], fw_ref[...])


def _ffn(h, nw, w_gate_up, w_down, fw, *, tm, tf, final_norm):
    t, d = h.shape
    d_ff = w_down.shape[0]
    nf = d_ff // tf
    win = _Windows()
    scratch = [((tm, d), BF)]
    temps = 2 * _nbytes((tm, tf), F32) + _nbytes((tm, tf), BF)
    return pl.pallas_call(
        functools.partial(_ffn_kernel, final_norm=final_norm),
        grid=(t // tm, nf),
        in_specs=[
            win.spec((tm, d), F32, lambda i, f: (i, 0)),
            win.spec((1, d), F32, lambda i, f: (0, 0)),
            win.spec((d, tf), BF, lambda i, f: (0, f)),
            win.spec((d, tf), BF, lambda i, f: (0, nf + f)),
            win.spec((tf, d), BF, lambda i, f: (f, 0)),
            win.spec((1, d), F32, lambda i, f: (0, 0)),
        ],
        out_specs=win.spec((tm, d), F32, lambda i, f: (i, 0)),
        out_shape=jax.ShapeDtypeStruct((t, d), F32),
        scratch_shapes=_vmem(scratch),
        compiler_params=win.params(("parallel", "arbitrary"), scratch, temps),
        name="ffn",
    )(h, nw, w_gate_up, w_gate_up, w_down, fw)


_EXP2_UNDERFLOW = -127.0
LOG2_E = 1.4426950408889634


def _sb_kernel(q_ref, k_ref, v_ref, o_ref, *, tq, group):
    s = q_ref.shape[1]
    tw = 2 * tq
    def later(n):
        r = lax.broadcasted_iota(jnp.int32, (n, n), 0)
        c = lax.broadcasted_iota(jnp.int32, (n, n), 1)
        return jnp.where(r > c, 1.0, 0.0).astype(BF)

    def causal(n_keys, offset):
        qr = lax.broadcasted_iota(jnp.int32, (tq, n_keys), 0)
        kc = lax.broadcasted_iota(jnp.int32, (tq, n_keys), 1)
        return kc < qr + offset

    later_w, later_b = later(tw), later(tq)
    diag_mask = causal(tq, 0)
    win_mask = causal(tw, tq)

    def scores(q, start, width, mask):
        kb = k_ref[0, pl.ds(start, width), :]
        z = lax.dot_general(q, kb, _NT, preferred_element_type=F32)
        nz = -z
        lf = jnp.minimum(nz, 0.0) - jnp.log2(1.0 + jnp.exp2(jnp.minimum(z, nz)))
        if mask is not None:
            lf = jnp.where(mask, lf, 0.0)
        return z, lf

    def suffix(lf, later, carry):
        after = _dot(lf.astype(BF), later)
        if carry is not None:
            after = after + carry
        return after, after[:, :1] + lf[:, :1]

    def weighted(z, lf, after, start, width, mask):
        a = jnp.exp2(z + lf + after)
        if mask is not None:
            a = jnp.where(mask, a, 0.0)
        return _dot(a.astype(BF), v_ref[0, pl.ds(start, width), :])

    def window(tile):
        if tile == 0:
            return 0, tq, later_b, diag_mask
        return (tile - 1) * tq, tw, later_w, win_mask

    def extend(tile, q, carry, top, acc):
        def cond(st):
            return jnp.logical_and(st[0] >= 0, st[1] > _EXP2_UNDERFLOW)

        def body(st):
            j, _, carry, acc = st
            start = pl.multiple_of(j * tq, tq)
            z, lf = scores(q, start, tq, None)
            after, carry = suffix(lf, later_b, carry)
            acc = acc + weighted(z, lf, after, start, tq, None)
            return j - 1, jnp.max(carry), carry, acc

        init = (jnp.asarray(tile - 2, jnp.int32), top, carry, acc)
        return lax.while_loop(cond, body, init)[3]

    def first_passes(base):
        tiles = [base + g for g in range(group)]
        wins = [window(t) for t in tiles]
        qs = [q_ref[0, t * tq:(t + 1) * tq, :] for t in tiles]
        zl = [scores(q, w[0], w[1], w[3]) for q, w in zip(qs, wins)]
        ac = [suffix(lf, w[2], None) for (_, lf), w in zip(zl, wins)]
        accs = [weighted(z, lf, after, w[0], w[1], w[3]) for (z, lf), (after, _), w in zip(zl, ac, wins)]
        tops = [jnp.max(carry) for _, carry in ac]
        return list(zip(tiles, qs, [carry for _, carry in ac], tops, accs))

    todo = []
    for base in range(0, s // tq, group):
        todo += first_passes(base)
    for t, q, carry, top, acc in todo:
        acc = extend(t, q, carry, top, acc)
        o_ref[0, t * tq:(t + 1) * tq, :] = acc.astype(o_ref.dtype)


def _sb_attention(qkv, cast_jobs, *, tq, group):
    bsz, s, w3 = qkv.shape
    w = w3 // 3
    hd = w // SB_HEADS
    grid = (bsz, SB_HEADS)
    win = _Windows()
    c_ops, c_in, c_out, c_shapes, c_parts = _cast_specs(win, cast_jobs, grid)
    temps = group * 8 * _nbytes((tq, 2 * tq), F32)
    outs = pl.pallas_call(
        _with_casts(functools.partial(_sb_kernel, tq=tq, group=group), 3, 1, c_parts),
        grid=grid,
        in_specs=[
            win.spec((1, s, hd), BF, lambda b, h: (b, 0, h)),
            win.spec((1, s, hd), BF, lambda b, h: (b, 0, SB_HEADS + h)),
            win.spec((1, s, hd), BF, lambda b, h: (b, 0, 2 * SB_HEADS + h)),
        ] + c_in,
        out_specs=[win.spec((1, s, hd), BF, lambda b, h: (b, 0, h))] + c_out,
        out_shape=[jax.ShapeDtypeStruct((bsz, s, w), BF)] + c_shapes,
        compiler_params=win.params(("arbitrary", "arbitrary"), temps=temps),
        name="sb_attn",
    )(qkv, qkv, qkv, *c_ops)
    return outs[0], outs[1:]


def kernel(x, attn_norm_w, ffn_norm_w, gla_w_in, gla_w_gate_up, gla_b_gate, gla_gnorm_w, gla_w_out,
           kv_norm_w, sb_w_kv, sb_w_q, sb_w_out, ffn_w_gate_up, ffn_w_down, final_norm_w):
    bsz, s, d = x.shape
    t = bsz * s
    depth = attn_norm_w.shape[0]
    n_gla = gla_w_in.shape[0]
    rank = gla_w_gate_up.shape[1]
    main_w = gla_w_in.shape[2] - rank
    hd = d // SB_HEADS
    row = lambda v: v.reshape(1, -1)
    assert n_gla >= 1 and depth - n_gla == 1

    w_in_t = jnp.swapaxes(gla_w_in, 1, 2)

    h = x.reshape(t, d)
    for layer in range(depth):
        if layer < n_gla:
            proj, la = _gla_inproj(h, row(attn_norm_w[layer]), w_in_t, layer, main_w, rank,
                                   gla_w_gate_up[layer], row(gla_b_gate[layer]), **TILES["inproj"])
            casts = [[(ffn_w_gate_up, layer)], [(ffn_w_down, layer)], [(gla_w_out, layer)]]
            if layer == n_gla - 1:
                casts.append([(sb_w_q, 0), (sb_w_kv, None)])
            o, cast = _gla_core(proj.reshape(bsz, s, main_w), la.reshape(bsz, s, -1),
                                row(gla_gnorm_w[layer]), casts, **TILES["gla"])
            w_gate_up, w_down, w_out = cast[:3]
            if layer == n_gla - 1:
                w_qkv = cast[3]
        else:
            j = layer - n_gla
            qkv, (w_down, w_out) = _norm_matmul2(
                h, row(attn_norm_w[layer]), row(kv_norm_w), w_qkv, sb_w_q.shape[2],
                [[(ffn_w_down, layer)], [(sb_w_out, j)]], scale_a=hd ** -0.5 * LOG2_E, **TILES["qkv"])
            o, (w_gate_up,) = _sb_attention(qkv.reshape(bsz, s, -1), [[(ffn_w_gate_up, layer)]], **TILES["sb"])
        h = _proj_res(o.reshape(t, -1), w_out, h, **TILES["proj"])
        h = _ffn(h, row(ffn_norm_w[layer]), w_gate_up, w_down, row(final_norm_w),
                 final_norm=layer == depth - 1, **TILES["ffn"])
    return h.reshape(bsz, s, d)
```
